```python
import math
import jax, jax.numpy as jnp
from jax import lax
import numpy as np

D_MODEL = 1024
BATCH = 4
SEQ = 4096
DEPTH = 4
DEC_BATCH = 128
DEC_SEQ = 8
PAST_LEN = 2048
PAGE_SIZE = 128

N_EVEN = (DEPTH + 1) // 2
N_ODD = DEPTH // 2
EPS = 1e-6
CONV_CH = D_MODEL // 2
CONV_W = 31
HEAD_DIM = 64
HPG = 4
DIL_GROUPS = ((128, 1), (512, 4), (2048, 16))
N_DIL = len(DIL_GROUPS)
ATT_W = N_DIL * HPG * HEAD_DIM
ATT_OUT = HPG * HEAD_DIM
ROT_DIM = HEAD_DIM // 4
ROPE_THETA = 500000.0
BAND_BLOCK = 128
IN_COLS = 2 * CONV_CH + 3 * ATT_W
MIX_OUT = CONV_CH + ATT_OUT
S5_GROUP = 16
S5_GROUPS = D_MODEL // S5_GROUP
S5_STATE = 64
S5_CHUNK = 128
D_FF = int(math.ceil(8 * D_MODEL / 3 / 256)) * 256

kernel_name = 'hybrid_conv_dilattn_s5_adaln_step'


def rmsnorm(x, g):
    xf = x.astype(jnp.float32)
    y = xf * lax.rsqrt(jnp.mean(xf * xf, axis=-1, keepdims=True) + EPS)
    return (y * g.astype(jnp.float32)).astype(x.dtype)


def layernorm(x, g, b):
    xf = x.astype(jnp.float32)
    mu = jnp.mean(xf, axis=-1, keepdims=True)
    xc = xf - mu
    y = xc * lax.rsqrt(jnp.mean(xc * xc, axis=-1, keepdims=True) + EPS)
    return (y * g.astype(jnp.float32) + b.astype(jnp.float32)).astype(x.dtype)


def rope(x, pos):
    half = ROT_DIM // 2
    inv = jnp.float32(ROPE_THETA) ** (-(2.0 / ROT_DIM) * jnp.arange(half, dtype=jnp.float32))
    ang = pos.astype(jnp.float32)[:, None] * inv[None, :]
    bshape = (1, pos.shape[0]) + (1,) * (x.ndim - 3) + (half,)
    cos = jnp.cos(ang).reshape(bshape)
    sin = jnp.sin(ang).reshape(bshape)
    xr = x[..., :ROT_DIM].astype(jnp.float32)
    x1, x2 = xr[..., :half], xr[..., half:]
    rot = jnp.concatenate([x1 * cos - x2 * sin, x2 * cos + x1 * sin], axis=-1).astype(x.dtype)
    return jnp.concatenate([rot, x[..., ROT_DIM:]], axis=-1)


def dilated_attn_prompt(q, k, v, window, dil):
    n, s, h, dh = q.shape
    ls = s // dil
    bb = BAND_BLOCK
    nb = -(-ls // bb)
    lp = nb * bb
    span = window // dil

    def strided(t):
        t = t.reshape(n, ls, dil, h, dh).transpose(0, 2, 1, 3, 4)
        t = jnp.pad(t, ((0, 0), (0, 0), (0, lp - ls), (0, 0), (0, 0)))
        return t.reshape(n, dil, nb, bb, h, dh)

    def with_prev(t):
        prev = jnp.pad(t, ((0, 0), (0, 0), (1, 0), (0, 0), (0, 0), (0, 0)))[:, :, :nb]
        return jnp.concatenate([prev, t], axis=3)

    qs = strided(q)
    kb = with_prev(strided(k))
    vb = with_prev(strided(v))
    sc = jnp.einsum('nrbqhd,nrbkhd->nrbhqk', qs, kb).astype(jnp.float32)
    blk = jnp.arange(nb)[:, None, None] * bb
    qi = blk + jnp.arange(bb)[None, :, None]
    ki = blk - bb + jnp.arange(2 * bb)[None, None, :]
    dist = qi - ki
    mask = (ki >= 0) & (dist >= 0) & (dist <= span)
    sc = jnp.where(mask[None, None, :, None], sc, -jnp.inf)
    m = jnp.max(sc, axis=-1, keepdims=True)
    p = jnp.exp(sc - m)
    l = jnp.sum(p, axis=-1, keepdims=True)
    o = jnp.einsum('nrbhqk,nrbkhd->nrbqhd', p / l, vb.astype(jnp.float32))
    lse = (m + jnp.log(l))[..., 0]
    o = o.reshape(n, dil, lp, h, dh)[:, :, :ls].transpose(0, 2, 1, 3, 4).reshape(n, s, h, dh)
    lse = lse.transpose(0, 1, 2, 4, 3).reshape(n, dil, lp, h)[:, :, :ls]
    lse = lse.transpose(0, 2, 1, 3).reshape(n, s, h)
    return o, lse


def dilated_attn_sample(q, k, v, buf, window, dil):
    n, t, h, dh = q.shape
    wb = buf.shape[1]
    span = window // dil
    kf = jnp.concatenate([buf[:, :, 0].astype(k.dtype), k], axis=1)
    vf = jnp.concatenate([buf[:, :, 1].astype(v.dtype), v], axis=1)
    idx = wb + jnp.arange(t)[:, None] - dil * jnp.arange(span + 1)[None, :]
    valid = idx >= 0
    idx = jnp.maximum(idx, 0)
    kg = kf[:, idx]
    vg = vf[:, idx]
    sc = jnp.einsum('nthd,ntkhd->nthk', q, kg).astype(jnp.float32)
    sc = jnp.where(valid[None, :, None, :], sc, -jnp.inf)
    m = jnp.max(sc, axis=-1, keepdims=True)
    p = jnp.exp(sc - m)
    l = jnp.sum(p, axis=-1, keepdims=True)
    o = jnp.einsum('nthk,ntkhd->nthd', p / l, vg.astype(jnp.float32))
    lse = (m + jnp.log(l))[..., 0]
    return o, lse


def even_mixer(h, pos, conv_hist, kv_bufs, w_in, conv_w, conv_b, ln_g, ln_b, w_o):
    n, L, _ = h.shape
    z = h @ w_in
    a_val = z[..., :CONV_CH]
    a_gate = z[..., CONV_CH:2 * CONV_CH]
    qkv = z[..., 2 * CONV_CH:].reshape(n, L, 3, N_DIL, HPG, HEAD_DIM)
    u = a_val * jax.nn.sigmoid(a_gate)
    full = jnp.concatenate([conv_hist.astype(u.dtype), u], axis=1)
    yc = lax.conv_general_dilated(full, conv_w[:, None, :].astype(full.dtype), (1,), 'VALID',
                                  dimension_numbers=('NWC', 'WIO', 'NWC'),
                                  feature_group_count=CONV_CH) + conv_b
    yc = jax.nn.silu(layernorm(yc, ln_g, ln_b))
    new_hist = full[:, -(CONV_W - 1):]
    q = rope(qkv[:, :, 0], pos) * (HEAD_DIM ** -0.5)
    k = rope(qkv[:, :, 1], pos)
    v = qkv[:, :, 2]
    outs, lses, kv_new = [], [], []
    for gi, (win, dil) in enumerate(DIL_GROUPS):
        qg, kg, vg = q[:, :, gi], k[:, :, gi], v[:, :, gi]
        if kv_bufs is None:
            o, lse = dilated_attn_prompt(qg, kg, vg, win, dil)
            keep = min(win, L)
            kv_new.append(jnp.stack([kg[:, L - keep:], vg[:, L - keep:]], axis=2))
        else:
            o, lse = dilated_attn_sample(qg, kg, vg, kv_bufs[gi], win, dil)
            kv_new.append(jnp.stack([kg, vg], axis=2))
        outs.append(o)
        lses.append(lse)
    wts = jax.nn.softmax(jnp.stack(lses), axis=0)[..., None]
    att = jnp.sum(wts * jnp.stack(outs), axis=0).astype(h.dtype).reshape(n, L, ATT_OUT)
    mix = jnp.concatenate([yc.astype(h.dtype), att], axis=-1) @ w_o
    return mix, new_hist, kv_new


def complex_affine_combine(e1, e2):
    a1r, a1i, b1r, b1i = e1
    a2r, a2i, b2r, b2i = e2
    return (a2r * a1r - a2i * a1i, a2r * a1i + a2i * a1r,
            a2r * b1r - a2i * b1i + b2r, a2r * b1i + a2i * b1r + b2i)


def s5_mixer(u, s0, lam_re, lam_im, log_dt, b_re, b_im, c_re, c_im, d_skip, w_glu, b_glu):
    f32 = jnp.float32
    n, L, _ = u.shape
    ug = u.astype(f32).reshape(n, L, S5_GROUPS, S5_GROUP)
    lr = lam_re.astype(f32)
    li = lam_im.astype(f32)
    dt = jnp.exp(log_dt.astype(f32))[:, None]
    mag = jnp.exp(lr * dt)
    ph = li * dt
    ab_re = mag * jnp.cos(ph)
    ab_im = mag * jnp.sin(ph)
    den = lr * lr + li * li
    nr = ab_re - 1.0
    f_re = (nr * lr + ab_im * li) / den
    f_im = (ab_im * lr - nr * li) / den
    br = b_re.astype(f32)
    bi = b_im.astype(f32)
    bb_re = f_re[..., None] * br - f_im[..., None] * bi
    bb_im = f_re[..., None] * bi + f_im[..., None] * br
    cr = c_re.astype(f32)
    ci = c_im.astype(f32)
    chunk = S5_CHUNK if L % S5_CHUNK == 0 else L
    nc = L // chunk
    uc = ug.reshape(n, nc, chunk, S5_GROUPS, S5_GROUP).transpose(1, 0, 2, 3, 4)

    def step(carry, ub):
        sr, si = carry
        bur = jnp.einsum('ncgh,gph->ncgp', ub, bb_re)
        bui = jnp.einsum('ncgh,gph->ncgp', ub, bb_im)
        bur = bur.at[:, 0].add(ab_re * sr - ab_im * si)
        bui = bui.at[:, 0].add(ab_re * si + ab_im * sr)
        ar = jnp.broadcast_to(ab_re, bur.shape)
        ai = jnp.broadcast_to(ab_im, bur.shape)
        _, _, hr, hi = lax.associative_scan(complex_affine_combine, (ar, ai, bur, bui), axis=1)
        y = jnp.einsum('ncgp,ghp->ncgh', hr, cr) - jnp.einsum('ncgp,ghp->ncgh', hi, ci)
        return (hr[:, -1], hi[:, -1]), y

    s0f = s0.astype(f32)
    (sr, si), ys = lax.scan(step, (s0f[..., 0], s0f[..., 1]), uc)
    y = ys.transpose(1, 0, 2, 3, 4).reshape(n, L, D_MODEL) + d_skip.astype(f32) * u.astype(f32)
    zz = jax.nn.gelu(y)
    g = zz @ w_glu.astype(f32) + b_glu.astype(f32)
    out = g[..., :D_MODEL] * jax.nn.sigmoid(g[..., D_MODEL:])
    return out.astype(u.dtype), jnp.stack([sr, si], axis=-1)


def swiglu(h, wg, wu, wd):
    return (jax.nn.silu(h @ wg) * (h @ wu)) @ wd


def run_trunk(x, c, pos, conv_st, kv_st, s5_st, W):
    n = x.shape[0]
    new_conv, new_s5 = [], []
    new_kv = [[] for _ in range(N_DIL)]
    ei = 0
    oi = 0
    for layer in range(DEPTH):
        mod = jax.nn.silu(c.astype(jnp.float32)) @ W['w_ada'][layer].astype(jnp.float32)
        mod = (mod + W['b_ada'][layer].astype(jnp.float32)).astype(x.dtype)[:, None, :]
        sh1, sc1, g1, sh2, sc2, g2 = jnp.split(mod, 6, axis=-1)
        h = rmsnorm(x, W['norm_g'][layer, 0]) * (1 + sc1) + sh1
        if layer % 2 == 0:
            hist = jnp.zeros((n, CONV_W - 1, CONV_CH), x.dtype) if conv_st is None else conv_st[ei]
            bufs = None if kv_st is None else [kv[ei] for kv in kv_st]
            mix, hist_new, kv_new = even_mixer(h, pos, hist, bufs, W['w_in'][ei], W['conv_w'][ei],
                                               W['conv_b'][ei], W['conv_ln_g'][ei],
                                               W['conv_ln_b'][ei], W['w_o'][ei])
            new_conv.append(hist_new)
            for gi in range(N_DIL):
                new_kv[gi].append(kv_new[gi])
            ei += 1
        else:
            s0 = jnp.zeros((n, S5_GROUPS, S5_STATE, 2), jnp.float32) if s5_st is None else s5_st[oi]
            mix, s_new = s5_mixer(h, s0, W['s5_lam_re'][oi], W['s5_lam_im'][oi], W['s5_log_dt'][oi],
                                  W['s5_b_re'][oi], W['s5_b_im'][oi], W['s5_c_re'][oi],
                                  W['s5_c_im'][oi], W['s5_d'][oi], W['s5_w_glu'][oi],
                                  W['s5_b_glu'][oi])
            new_s5.append(s_new)
            oi += 1
        x = x + g1 * mix
        h = rmsnorm(x, W['norm_g'][layer, 1]) * (1 + sc2) + sh2
        x = x + g2 * swiglu(h, W['w_ff_gate'][layer], W['w_ff_up'][layer], W['w_ff_down'][layer])
    y = rmsnorm(x, W['final_g'])
    return y, jnp.stack(new_conv), [jnp.stack(kv) for kv in new_kv], jnp.stack(new_s5)


def setup_inputs(seed: int = 0) -> dict:
    key = jax.random.key(seed)
    ks = jax.random.split(key, 32)
    f32 = jnp.float32

    def nrm(i, shape, s):
        return jax.random.normal(ks[i], shape, f32) * s

    wb = [min(w, PAST_LEN) for (w, _) in DIL_GROUPS]
    lam_im0 = jnp.pi * jnp.arange(S5_STATE, dtype=f32)
    return {
        'x_prompt': nrm(0, (BATCH, SEQ, D_MODEL), 1.0),
        'x_sample': nrm(1, (DEC_BATCH, DEC_SEQ, D_MODEL), 1.0),
        'cache_conv': nrm(2, (N_EVEN, DEC_BATCH, CONV_W - 1, CONV_CH), 0.5),
        'cache_kv_g0': nrm(3, (N_EVEN, DEC_BATCH, wb[0], 2, HPG, HEAD_DIM), 1.0),
        'cache_kv_g1': nrm(4, (N_EVEN, DEC_BATCH, wb[1], 2, HPG, HEAD_DIM), 1.0),
        'cache_kv_g2': nrm(5, (N_EVEN, DEC_BATCH, wb[2], 2, HPG, HEAD_DIM), 1.0),
        'state_s5': nrm(6, (N_ODD, DEC_BATCH, S5_GROUPS, S5_STATE, 2), 1.0),
        'c_prompt': nrm(7, (BATCH, D_MODEL), 1.0),
        'c_sample': nrm(8, (DEC_BATCH, D_MODEL), 1.0),
        'norm_g': 1.0 + nrm(9, (DEPTH, 2, D_MODEL), 0.01),
        'final_g': 1.0 + nrm(10, (D_MODEL,), 0.01),
        'w_ada': nrm(11, (DEPTH, D_MODEL, 6 * D_MODEL), 0.5 * D_MODEL ** -0.5),
        'b_ada': nrm(12, (DEPTH, 6 * D_MODEL), 0.01),
        'w_in': nrm(13, (N_EVEN, D_MODEL, IN_COLS), D_MODEL ** -0.5),
        'conv_w': nrm(14, (N_EVEN, CONV_W, CONV_CH), CONV_W ** -0.5),
        'conv_b': nrm(15, (N_EVEN, CONV_CH), 0.01),
        'conv_ln_g': 1.0 + nrm(16, (N_EVEN, CONV_CH), 0.01),
        'conv_ln_b': nrm(17, (N_EVEN, CONV_CH), 0.01),
        'w_o': nrm(18, (N_EVEN, MIX_OUT, D_MODEL), MIX_OUT ** -0.5),
        's5_lam_re': -0.5 + nrm(19, (N_ODD, S5_GROUPS, S5_STATE), 0.01),
        's5_lam_im': lam_im0 + nrm(20, (N_ODD, S5_GROUPS, S5_STATE), 0.01),
        's5_log_dt': jax.random.uniform(ks[21], (N_ODD, S5_GROUPS), f32,
                                        minval=math.log(1e-3), maxval=math.log(1e-1)),
        's5_b_re': nrm(22, (N_ODD, S5_GROUPS, S5_STATE, S5_GROUP), (2 * S5_GROUP) ** -0.5),
        's5_b_im': nrm(23, (N_ODD, S5_GROUPS, S5_STATE, S5_GROUP), (2 * S5_GROUP) ** -0.5),
        's5_c_re': nrm(24, (N_ODD, S5_GROUPS, S5_GROUP, S5_STATE), S5_STATE ** -0.5),
        's5_c_im': nrm(25, (N_ODD, S5_GROUPS, S5_GROUP, S5_STATE), S5_STATE ** -0.5),
        's5_d': nrm(26, (N_ODD, D_MODEL), 1.0),
        's5_w_glu': nrm(27, (N_ODD, D_MODEL, 2 * D_MODEL), D_MODEL ** -0.5),
        's5_b_glu': nrm(28, (N_ODD, 2 * D_MODEL), 0.01),
        'w_ff_gate': nrm(29, (DEPTH, D_MODEL, D_FF), D_MODEL ** -0.5),
        'w_ff_up': nrm(30, (DEPTH, D_MODEL, D_FF), D_MODEL ** -0.5),
        'w_ff_down': nrm(31, (DEPTH, D_FF, D_MODEL), D_FF ** -0.5),
    }


def reference(x_prompt, x_sample, cache_conv, cache_kv_g0, cache_kv_g1, cache_kv_g2, state_s5,
              c_prompt, c_sample, norm_g, final_g, w_ada, b_ada, w_in, conv_w, conv_b, conv_ln_g,
              conv_ln_b, w_o, s5_lam_re, s5_lam_im, s5_log_dt, s5_b_re, s5_b_im, s5_c_re, s5_c_im,
              s5_d, s5_w_glu, s5_b_glu, w_ff_gate, w_ff_up, w_ff_down):
    W = dict(norm_g=norm_g, final_g=final_g, w_ada=w_ada, b_ada=b_ada, w_in=w_in, conv_w=conv_w,
             conv_b=conv_b, conv_ln_g=conv_ln_g, conv_ln_b=conv_ln_b, w_o=w_o,
             s5_lam_re=s5_lam_re, s5_lam_im=s5_lam_im, s5_log_dt=s5_log_dt, s5_b_re=s5_b_re,
             s5_b_im=s5_b_im, s5_c_re=s5_c_re, s5_c_im=s5_c_im, s5_d=s5_d, s5_w_glu=s5_w_glu,
             s5_b_glu=s5_b_glu, w_ff_gate=w_ff_gate, w_ff_up=w_ff_up, w_ff_down=w_ff_down)
    pos_p = jnp.arange(x_prompt.shape[1])
    pos_s = PAST_LEN + jnp.arange(x_sample.shape[1])
    y_prompt, conv_p, kvs_p, s5_p = run_trunk(x_prompt, c_prompt, pos_p, None, None, None, W)
    y_sample, conv_s, kvs_s, s5_s = run_trunk(x_sample, c_sample, pos_s, cache_conv,
                                              [cache_kv_g0, cache_kv_g1, cache_kv_g2], state_s5, W)
    kv0_p, kv1_p, kv2_p = kvs_p
    kv0_s, kv1_s, kv2_s = kvs_s
    return (y_prompt, y_sample, conv_p, kv0_p, kv1_p, kv2_p, s5_p,
            conv_s, kv0_s, kv1_s, kv2_s, s5_s)
```

```python
import functools
import math

import jax
import jax.numpy as jnp
from jax import lax
from jax.experimental import pallas as pl
from jax.experimental.pallas import tpu as pltpu

F32 = jnp.float32
BF16 = jnp.bfloat16

EPS = 1e-6
CONV_W = 31
HEAD_DIM = 64
HPG = 4
DIL_GROUPS = ((128, 1), (512, 4), (2048, 16))
N_DIL = len(DIL_GROUPS)
GRP_W = HPG * HEAD_DIM
ROT_DIM = HEAD_DIM // 4
ROPE_THETA = 500000.0
BAND_BLOCK = 128
S5_GROUP = 16
S5_STATE = 64
S5_JBLK = 4
PAST_LEN = 2048
LANES = 128
VMEM_LIMIT = 56 * 1024 * 1024


def _cparams(sem):
    return pltpu.CompilerParams(dimension_semantics=sem, vmem_limit_bytes=VMEM_LIMIT)


def _const_spec(shape):
    nd = len(shape)
    return pl.BlockSpec(shape, lambda *_: (0,) * nd)


def _row_spec(b0, b1, cols):
    return pl.BlockSpec((b0, b1, cols), lambda i, j: (i, j, 0))


def _mod_spec(b0, d, k):
    return pl.BlockSpec((b0, 1, d), lambda i, j: (i, 0, k))


def _norm_mod(x, g, sc, sh):
    y = x * lax.rsqrt(jnp.mean(x * x, axis=-1, keepdims=True) + EPS)
    return (y * g) * (1.0 + sc) + sh


def _silu(x):
    return x * jax.nn.sigmoid(x)


def _log2(n):
    assert n > 0 and n & (n - 1) == 0, n
    return n.bit_length() - 1


def _div_pow2(x, n):
    return x >> _log2(n)


def _mod_pow2(x, n):
    return x & ((1 << _log2(n)) - 1)


def _bdot(a, b):
    return jnp.dot(a.astype(BF16), b.astype(BF16), preferred_element_type=F32)


def _bdot_nt(a, b):
    return lax.dot_general(a.astype(BF16), b.astype(BF16), (((1,), (1,)), ((), ())),
                           preferred_element_type=F32)


def _ada_kernel(c_ref, w_ref, b_ref, o_ref):
    o_ref[0] = _bdot(_silu(c_ref[...]), w_ref[0]) + b_ref[0]


def _ada_call(c_all, w_ada, b_ada):
    depth, d, d6 = w_ada.shape
    m = c_all.shape[0]
    tn = 1536
    return pl.pallas_call(
        _ada_kernel,
        grid=(depth, d6 // tn),
        in_specs=[pl.BlockSpec((m, d), lambda l, j: (0, 0)),
                  pl.BlockSpec((1, d, tn), lambda l, j: (l, 0, j)),
                  pl.BlockSpec((1, 1, tn), lambda l, j: (l, 0, j))],
        out_specs=pl.BlockSpec((1, m, tn), lambda l, j: (l, 0, j)),
        out_shape=jax.ShapeDtypeStruct((depth, m, d6), F32),
        compiler_params=_cparams(("arbitrary", "arbitrary")),
        name="ada_mod",
    )(c_all, w_ada, b_ada.reshape(depth, 1, d6))


def _inproj_kernel(x_ref, sh_ref, sc_ref, ng_ref, w_ref, ta_ref, tb_ref, tc_ref,
                   u_ref, q_ref, k_ref, v_ref, *, conv_ch, att_w):
    b0, b1, d = x_ref.shape
    rows = b0 * b1
    h = _norm_mod(x_ref[...], ng_ref[...], sc_ref[...], sh_ref[...])
    hb = h.reshape(rows, d).astype(BF16)
    a_val = jnp.dot(hb, w_ref[:, 0:conv_ch], preferred_element_type=F32)
    a_gate = jnp.dot(hb, w_ref[:, conv_ch:2 * conv_ch], preferred_element_type=F32)
    u_ref[...] = (a_val * jax.nn.sigmoid(a_gate)).reshape(b0, b1, conv_ch)

    def table(t_ref):
        return jnp.broadcast_to(t_ref[...], (b0, b1, LANES)).reshape(rows, LANES)

    ta, tb, tc = table(ta_ref), table(tb_ref), table(tc_ref)
    half = ROT_DIM // 2

    def rope(z, scale):
        outs = []
        for c in range(att_w // LANES):
            seg = z[:, c * LANES:(c + 1) * LANES]
            fwd = pltpu.roll(seg, LANES - half, 1)
            bwd = pltpu.roll(seg, half, 1)
            r = seg * ta + fwd * tb + bwd * tc
            outs.append(r * scale if scale != 1.0 else r)
        return jnp.concatenate(outs, axis=-1)

    c0 = 2 * conv_ch
    q = jnp.dot(hb, w_ref[:, c0:c0 + att_w], preferred_element_type=F32)
    q_ref[...] = rope(q, HEAD_DIM ** -0.5).reshape(b0, b1, att_w)
    k = jnp.dot(hb, w_ref[:, c0 + att_w:c0 + 2 * att_w], preferred_element_type=F32)
    k_ref[...] = rope(k, 1.0).reshape(b0, b1, att_w)
    v = jnp.dot(hb, w_ref[:, c0 + 2 * att_w:c0 + 3 * att_w], preferred_element_type=F32)
    v_ref[...] = v.reshape(b0, b1, att_w)


def _inproj_call(x, mod, ng, w_in_bf, tabs, b0, b1, conv_ch, att_w):
    a, b, d = x.shape
    ta, tb, tc = tabs
    tspec = pl.BlockSpec((1, b1, LANES), lambda i, j: (0, j, 0))
    out_shapes = (jax.ShapeDtypeStruct((a, b, conv_ch), F32),) + \
        tuple(jax.ShapeDtypeStruct((a, b, att_w), F32) for _ in range(3))
    out_specs = (_row_spec(b0, b1, conv_ch),) + tuple(_row_spec(b0, b1, att_w) for _ in range(3))
    return pl.pallas_call(
        functools.partial(_inproj_kernel, conv_ch=conv_ch, att_w=att_w),
        grid=(a // b0, b // b1),
        in_specs=[_row_spec(b0, b1, d), _mod_spec(b0, d, 0), _mod_spec(b0, d, 1),
                  _const_spec((1, d)), _const_spec(w_in_bf.shape), tspec, tspec, tspec],
        out_specs=out_specs,
        out_shape=out_shapes,
        compiler_params=_cparams(("parallel", "parallel")),
        name="in_proj",
    )(x, mod, mod, ng.reshape(1, d), w_in_bf, ta, tb, tc)


def _rope_tables(pos):
    half = ROT_DIM // 2
    inv = jnp.float32(ROPE_THETA) ** (-(2.0 / ROT_DIM) * jnp.arange(half, dtype=F32))
    ang = pos.astype(F32)[:, None] * inv[None, :]
    cos, sin = jnp.cos(ang), jnp.sin(ang)
    n = pos.shape[0]
    ones = jnp.ones((n, HEAD_DIM - ROT_DIM), F32)
    zeros = jnp.zeros((n, HEAD_DIM - ROT_DIM), F32)
    zh = jnp.zeros((n, half), F32)
    ta = jnp.concatenate([cos, cos, ones], axis=-1)
    tb = jnp.concatenate([-sin, zh, zeros], axis=-1)
    tc = jnp.concatenate([zh, sin, zeros], axis=-1)
    rep = LANES // HEAD_DIM
    return tuple(jnp.tile(t, (1, rep))[None] for t in (ta, tb, tc))


def _ln_silu(acc, g, b):
    mu = jnp.mean(acc, axis=-1, keepdims=True)
    xc = acc - mu
    y = xc * lax.rsqrt(jnp.mean(xc * xc, axis=-1, keepdims=True) + EPS)
    return _silu(y * g + b)


def _conv_prompt_kernel(cur_ref, prev_ref, w_ref, cb_ref, g_ref, b_ref, o_ref, buf, *, halo, rb):
    j = pl.program_id(1)
    tt = cur_ref.shape[1]
    prev = prev_ref[0]
    buf[0:halo, :] = jnp.where(j == 0, jnp.zeros_like(prev), prev)
    buf[halo:halo + tt, :] = cur_ref[0]
    off = halo - (CONV_W - 1)
    for r0 in range(0, tt, rb):
        acc = w_ref[0:1, :] * buf[r0 + off:r0 + off + rb, :]
        for jt in range(1, CONV_W):
            acc = acc + w_ref[jt:jt + 1, :] * buf[r0 + off + jt:r0 + off + jt + rb, :]
        o_ref[0, r0:r0 + rb, :] = _ln_silu(acc + cb_ref[...], g_ref[...], b_ref[...])


def _conv_prompt_call(u, conv_w, conv_b, ln_g, ln_b):
    n, l, c = u.shape
    tt, halo, rb = 256, 32, 32
    ratio = tt // halo
    return pl.pallas_call(
        functools.partial(_conv_prompt_kernel, halo=halo, rb=rb),
        grid=(n, l // tt),
        in_specs=[pl.BlockSpec((1, tt, c), lambda i, j: (i, j, 0)),
                  pl.BlockSpec((1, halo, c), lambda i, j: (i, jnp.maximum(j * ratio - 1, 0), 0)),
                  _const_spec((CONV_W, c)), _const_spec((1, c)), _const_spec((1, c)), _const_spec((1, c))],
        out_specs=pl.BlockSpec((1, tt, c), lambda i, j: (i, j, 0)),
        out_shape=jax.ShapeDtypeStruct((n, l, c), F32),
        scratch_shapes=[pltpu.VMEM((tt + halo, c), F32)],
        compiler_params=_cparams(("parallel", "arbitrary")),
        name="conv_prompt",
    )(u, u, conv_w, conv_b.reshape(1, c), ln_g.reshape(1, c), ln_b.reshape(1, c))


def _conv_sample_kernel(full_ref, w_ref, cb_ref, g_ref, b_ref, o_ref):
    t_out = o_ref.shape[0]
    for t in range(t_out):
        acc = w_ref[0:1, :] * full_ref[t]
        for jt in range(1, CONV_W):
            acc = acc + w_ref[jt:jt + 1, :] * full_ref[t + jt]
        o_ref[t] = _ln_silu(acc + cb_ref[...], g_ref[...], b_ref[...])


def _conv_sample_call(full_t, conv_w, conv_b, ln_g, ln_b):
    tf, nb, c = full_t.shape
    t_out = tf - (CONV_W - 1)
    nbk = 32
    return pl.pallas_call(
        _conv_sample_kernel,
        grid=(nb // nbk,),
        in_specs=[pl.BlockSpec((tf, nbk, c), lambda i: (0, i, 0)),
                  _const_spec((CONV_W, c)), _const_spec((1, c)), _const_spec((1, c)), _const_spec((1, c))],
        out_specs=pl.BlockSpec((t_out, nbk, c), lambda i: (0, i, 0)),
        out_shape=jax.ShapeDtypeStruct((t_out, nb, c), F32),
        compiler_params=_cparams(("parallel",)),
        name="conv_sample",
    )(full_t, conv_w, conv_b.reshape(1, c), ln_g.reshape(1, c), ln_b.reshape(1, c))


def _attn_prompt_kernel(q_ref, kc_ref, kp_ref, vc_ref, vp_ref, o_ref, l_ref, *, span):
    b = pl.program_id(2)
    bb = q_ref.shape[1]
    q = q_ref[0].astype(BF16)
    k2 = jnp.concatenate([kp_ref[0], kc_ref[0]], axis=0).astype(BF16)
    v2 = jnp.concatenate([vp_ref[0], vc_ref[0]], axis=0).astype(BF16)
    qi = lax.broadcasted_iota(jnp.int32, (bb, 2 * bb), 0) + bb
    ki = lax.broadcasted_iota(jnp.int32, (bb, 2 * bb), 1)
    dist = qi - ki
    mask = (dist >= 0) & (dist <= span) & ((ki >= bb) | (b > 0))
    for h in range(HPG):
        cs = slice(h * HEAD_DIM, (h + 1) * HEAD_DIM)
        s = lax.dot_general(q[:, cs], k2[:, cs], (((1,), (1,)), ((), ())), preferred_element_type=F32)
        s = jnp.where(mask, s, -jnp.inf)
        m = jnp.max(s, axis=-1, keepdims=True)
        p = jnp.exp(s - m)
        lsum = jnp.sum(p, axis=-1, keepdims=True)
        o = jnp.dot(p.astype(BF16), v2[:, cs], preferred_element_type=F32) / lsum
        o_ref[0, :, cs] = o
        l_ref[0, :, cs] = jnp.broadcast_to(m + jnp.log(lsum), (bb, HEAD_DIM))


def _attn_prompt_call(q, k, v, gi):
    n, l, att_w = q.shape
    win, dil = DIL_GROUPS[gi]
    ls = l // dil
    bb = BAND_BLOCK
    assert l % dil == 0 and ls % bb == 0
    ncb = att_w // GRP_W
    qs, ks, vs = (t.reshape(n, ls, dil * att_w) for t in (q, k, v))
    cur = pl.BlockSpec((1, bb, GRP_W), lambda i, r, b: (i, b, r * ncb + gi))
    prev = pl.BlockSpec((1, bb, GRP_W), lambda i, r, b: (i, jnp.maximum(b - 1, 0), r * ncb + gi))
    ospec = pl.BlockSpec((1, bb, GRP_W), lambda i, r, b: (i, b, r))
    o, lse = pl.pallas_call(
        functools.partial(_attn_prompt_kernel, span=win // dil),
        grid=(n, dil, ls // bb),
        in_specs=[cur, cur, prev, cur, prev],
        out_specs=(ospec, ospec),
        out_shape=(jax.ShapeDtypeStruct((n, ls, dil * GRP_W), F32),) * 2,
        compiler_params=_cparams(("parallel", "parallel", "arbitrary")),
        name=f"attn_prompt_g{gi}",
    )(qs, ks, ks, vs, vs)
    return o.reshape(n, l, GRP_W), lse.reshape(n, l, GRP_W)


def _attn_sample_kernel(q_ref, kn_ref, vn_ref, c0_ref, c1_ref, c2_ref, *out_refs, res2):
    t_new = q_ref.shape[1]
    rows = HPG * t_new
    ri = lax.broadcasted_iota(jnp.int32, (rows, GRP_W), 0)
    ci = lax.broadcasted_iota(jnp.int32, (rows, GRP_W), 1)
    head_mask = _div_pow2(ci, HEAD_DIM) == _div_pow2(ri, t_new)
    caches = (c0_ref[0], c1_ref[0], c2_ref[0].reshape(-1, 2 * GRP_W))
    for gi, (win, dil) in enumerate(DIL_GROUPS):
        cs = slice(gi * GRP_W, (gi + 1) * GRP_W)
        cache = caches[gi]
        wc = cache.shape[0]
        qg = q_ref[0, :, cs]
        qb = jnp.where(head_mask, jnp.concatenate([qg] * HPG, axis=0), 0.0)
        kn, vn = kn_ref[0, :, cs], vn_ref[0, :, cs]
        s_c = _bdot_nt(qb, cache[:, 0:GRP_W])
        s_n = _bdot_nt(qb, kn)
        wb = win
        tq = _mod_pow2(lax.broadcasted_iota(jnp.int32, (rows, wc), 0), t_new)
        jc = lax.broadcasted_iota(jnp.int32, (rows, wc), 1)
        ki = jc if gi < 2 else _div_pow2(jc, res2) * dil + _mod_pow2(jc, res2)
        dist = wb + tq - ki
        valid_c = (dist >= 0) & (dist <= win) & (_mod_pow2(dist, dil) == 0)
        tqn = _mod_pow2(lax.broadcasted_iota(jnp.int32, (rows, t_new), 0), t_new)
        jn = lax.broadcasted_iota(jnp.int32, (rows, t_new), 1)
        dn = tqn - jn
        valid_n = (dn >= 0) & (dn <= win) & (_mod_pow2(dn, dil) == 0)
        s_c = jnp.where(valid_c, s_c, -jnp.inf)
        s_n = jnp.where(valid_n, s_n, -jnp.inf)
        m = jnp.maximum(jnp.max(s_c, axis=-1, keepdims=True), jnp.max(s_n, axis=-1, keepdims=True))
        p_c = jnp.exp(s_c - m)
        p_n = jnp.exp(s_n - m)
        lsum = jnp.sum(p_c, axis=-1, keepdims=True) + jnp.sum(p_n, axis=-1, keepdims=True)
        o_full = (_bdot(p_c, cache[:, GRP_W:2 * GRP_W]) + _bdot(p_n, vn)) / lsum
        lse_full = jnp.broadcast_to(m + jnp.log(lsum), (rows, GRP_W))
        o_full = jnp.where(head_mask, o_full, 0.0)
        lse_full = jnp.where(head_mask, lse_full, 0.0)
        o = o_full[0:t_new]
        lse = lse_full[0:t_new]
        for h in range(1, HPG):
            o = o + o_full[h * t_new:(h + 1) * t_new]
            lse = lse + lse_full[h * t_new:(h + 1) * t_new]
        out_refs[gi][0] = o
        out_refs[N_DIL + gi][0] = lse


def _attn_sample_call(q, k, v, caches):
    nb, t_new, att_w = q.shape
    c0, c1, c2 = caches
    dil2 = DIL_GROUPS[2][1]
    assert t_new <= dil2 and all(c.shape[1] == w for c, (w, _) in zip(caches, DIL_GROUPS))
    res2 = 8
    assert t_new <= res2 and dil2 % res2 == 0
    c0 = c0.reshape(nb, c0.shape[1], 2 * GRP_W)
    c1 = c1.reshape(nb, c1.shape[1], 2 * GRP_W)
    c2 = c2.reshape(nb, c2.shape[1] // dil2, dil2, 2 * GRP_W)
    qspec = pl.BlockSpec((1, t_new, att_w), lambda i: (i, 0, 0))
    ospec = pl.BlockSpec((1, t_new, GRP_W), lambda i: (i, 0, 0))
    outs = pl.pallas_call(
        functools.partial(_attn_sample_kernel, res2=res2),
        grid=(nb,),
        in_specs=[qspec, qspec, qspec,
                  pl.BlockSpec((1,) + c0.shape[1:], lambda i: (i, 0, 0)),
                  pl.BlockSpec((1,) + c1.shape[1:], lambda i: (i, 0, 0)),
                  pl.BlockSpec((1, c2.shape[1], res2, 2 * GRP_W), lambda i: (i, 0, 0, 0))],
        out_specs=(ospec,) * (2 * N_DIL),
        out_shape=(jax.ShapeDtypeStruct((nb, t_new, GRP_W), F32),) * (2 * N_DIL),
        compiler_params=_cparams(("parallel",)),
        name="attn_sample",
    )(q, k, v, c0, c1, c2)
    return outs[:N_DIL], outs[N_DIL:]


def _outproj_kernel(x_ref, g1_ref, yc_ref, o0_ref, o1_ref, o2_ref, l0_ref, l1_ref, l2_ref,
                    wa_ref, wb_ref, xo_ref):
    b0, b1, d = x_ref.shape
    rows = b0 * b1
    l0, l1, l2 = l0_ref[...], l1_ref[...], l2_ref[...]
    m = jnp.maximum(jnp.maximum(l0, l1), l2)
    e0, e1, e2 = jnp.exp(l0 - m), jnp.exp(l1 - m), jnp.exp(l2 - m)
    att = (e0 * o0_ref[...] + e1 * o1_ref[...] + e2 * o2_ref[...]) / (e0 + e1 + e2)
    mix = _bdot(yc_ref[...].reshape(rows, -1), wa_ref[...]) + _bdot(att.reshape(rows, -1), wb_ref[...])
    xo_ref[...] = x_ref[...] + g1_ref[...] * mix.reshape(b0, b1, d)


def _outproj_call(x, mod, yc, os_, ls_, wa_bf, wb_bf, b0, b1):
    a, b, d = x.shape
    gspec = _row_spec(b0, b1, GRP_W)
    return pl.pallas_call(
        _outproj_kernel,
        grid=(a // b0, b // b1),
        in_specs=[_row_spec(b0, b1, d), _mod_spec(b0, d, 2), _row_spec(b0, b1, yc.shape[-1])]
        + [gspec] * (2 * N_DIL) + [_const_spec(wa_bf.shape), _const_spec(wb_bf.shape)],
        out_specs=_row_spec(b0, b1, d),
        out_shape=jax.ShapeDtypeStruct((a, b, d), F32),
        compiler_params=_cparams(("parallel", "parallel")),
        name="out_proj",
    )(x, mod, yc, *os_, *ls_, wa_bf, wb_bf)


def _ffn_kernel(x_ref, sh_ref, sc_ref, g2_ref, ng_ref, wg_ref, wu_ref, wd_ref, fg_ref, o_ref, *, fchunk, final):
    b0, b1, d = x_ref.shape
    rows = b0 * b1
    x = x_ref[...]
    hb = _norm_mod(x, ng_ref[...], sc_ref[...], sh_ref[...]).reshape(rows, d).astype(BF16)
    dff = wg_ref.shape[1]
    acc = None
    for f0 in range(0, dff, fchunk):
        g = jnp.dot(hb, wg_ref[:, f0:f0 + fchunk], preferred_element_type=F32)
        u = jnp.dot(hb, wu_ref[:, f0:f0 + fchunk], preferred_element_type=F32)
        part = jnp.dot((_silu(g) * u).astype(BF16), wd_ref[f0:f0 + fchunk, :], preferred_element_type=F32)
        acc = part if acc is None else acc + part
    xn = x + g2_ref[...] * acc.reshape(b0, b1, d)
    if final:
        xn = xn * lax.rsqrt(jnp.mean(xn * xn, axis=-1, keepdims=True) + EPS) * fg_ref[...]
    o_ref[...] = xn


def _ffn_call(x, mod, ng, wg_bf, wu_bf, wd_bf, final_g, b0, b1, final):
    a, b, d = x.shape
    dff = wg_bf.shape[1]
    fchunk = dff // 2 if (dff // 2) % LANES == 0 else dff
    return pl.pallas_call(
        functools.partial(_ffn_kernel, fchunk=fchunk, final=final),
        grid=(a // b0, b // b1),
        in_specs=[_row_spec(b0, b1, d), _mod_spec(b0, d, 3), _mod_spec(b0, d, 4), _mod_spec(b0, d, 5),
                  _const_spec((1, d)), _const_spec(wg_bf.shape), _const_spec(wu_bf.shape),
                  _const_spec(wd_bf.shape), _const_spec((1, d))],
        out_specs=_row_spec(b0, b1, d),
        out_shape=jax.ShapeDtypeStruct((a, b, d), F32),
        compiler_params=_cparams(("parallel", "parallel")),
        name="ffn_final" if final else "ffn",
    )(x, mod, mod, mod, ng.reshape(1, d), wg_bf, wu_bf, wd_bf, final_g.reshape(1, d))


def _s5_param_kernel(lr_ref, li_ref, ldt_ref, br_ref, bi_ref, abr_ref, abi_ref, bbr_ref, bbi_ref):
    lr, li = lr_ref[0], li_ref[0]
    dt = jnp.exp(ldt_ref[0])
    mag = jnp.exp(lr * dt)
    ph = li * dt
    abr = mag * jnp.cos(ph)
    abi = mag * jnp.sin(ph)
    den = lr * lr + li * li
    nr = abr - 1.0
    fr = (nr * lr + abi * li) / den
    fi = (abi * lr - nr * li) / den
    abr_ref[0] = abr
    abi_ref[0] = abi
    for h in range(br_ref.shape[1]):
        br, bi = br_ref[0, h], bi_ref[0, h]
        bbr_ref[0, h] = fr * br - fi * bi
        bbi_ref[0, h] = fr * bi + fi * br


def _s5_param_call(lam_re, lam_im, log_dt, b_re, b_im):
    n_odd, g, p = lam_re.shape
    hh = b_re.shape[-1]
    bt_re = jnp.transpose(b_re, (0, 3, 1, 2))
    bt_im = jnp.transpose(b_im, (0, 3, 1, 2))
    gp = pl.BlockSpec((1, g, p), lambda l: (l, 0, 0))
    bs = pl.BlockSpec((1, hh, g, p), lambda l: (l, 0, 0, 0))
    return pl.pallas_call(
        _s5_param_kernel,
        grid=(n_odd,),
        in_specs=[gp, gp, pl.BlockSpec((1, g, 1), lambda l: (l, 0, 0)), bs, bs],
        out_specs=(gp, gp, bs, bs),
        out_shape=(jax.ShapeDtypeStruct((n_odd, g, p), F32),) * 2
        + (jax.ShapeDtypeStruct((n_odd, hh, g, p), F32),) * 2,
        compiler_params=_cparams(("arbitrary",)),
        name="s5_params",
    )(lam_re, lam_im, log_dt.reshape(n_odd, g, 1), bt_re, bt_im)


def _s5_block_weights(bbr, bbi, c_re, c_im):
    hh, g, p = bbr.shape
    gl = g // S5_JBLK
    eye = jnp.eye(gl, dtype=F32)

    def bblk(t):
        t = t.reshape(hh, S5_JBLK, gl, p).transpose(1, 2, 0, 3)
        t = t[:, :, :, None, :] * eye[None, :, None, :, None]
        return t.reshape(S5_JBLK, gl * hh, gl * p)

    def cblk(t):
        t = t.reshape(S5_JBLK, gl, hh, p).transpose(0, 1, 3, 2)
        t = t[:, :, :, None, :] * eye[None, :, None, :, None]
        return t.reshape(S5_JBLK, gl * p, gl * hh)

    bb = jnp.concatenate([bblk(bbr), bblk(bbi)], axis=-1).astype(BF16)
    cc = jnp.concatenate([cblk(c_re), cblk(-c_im)], axis=1).astype(BF16)
    return bb, cc


def _gelu_tanh(x):
    return 0.5 * x * (1.0 + jnp.tanh(math.sqrt(2.0 / math.pi) * (x + 0.044715 * (x * x * x))))


def _s5_kernel(x_ref, sh_ref, sc_ref, g1_ref, ng_ref, s0r_ref, s0i_ref, abr_ref, abi_ref, bb_ref, cc_ref,
               dsk_ref, wglu_ref, bglu_ref, xo_ref, sro_ref, sio_ref,
               st_r, st_i, bu_scr, h_scr, y_scr, *, lw, unroll):
    c = pl.program_id(1)
    nbk, tc, d = x_ref.shape
    rows = nbk * tc
    jw = d // S5_JBLK
    sw = bb_ref.shape[2] // 2

    @pl.when(c == 0)
    def _():
        st_r[...] = s0r_ref[...]
        st_i[...] = s0i_ref[...]

    x = x_ref[...]
    u2 = _norm_mod(x, ng_ref[...], sc_ref[...], sh_ref[...]).reshape(rows, d)
    ub = u2.astype(BF16)
    nch = sw // LANES
    for j in range(S5_JBLK):
        bu = jnp.dot(ub[:, j * jw:(j + 1) * jw], bb_ref[j], preferred_element_type=F32)
        for k in range(2 * nch):
            bu_scr[k] = bu[:, k * LANES:(k + 1) * LANES]
        for lc in range(0, nch, lw):
            chunks = range(lc, lc + lw)
            cols = [slice(j * sw + k * LANES, j * sw + (k + 1) * LANES) for k in chunks]
            ar = [jnp.broadcast_to(abr_ref[:, cs], (nbk, LANES)) for cs in cols]
            ai = [jnp.broadcast_to(abi_ref[:, cs], (nbk, LANES)) for cs in cols]

            def step(t, carry, chunks=chunks, ar=ar, ai=ai):
                rsel = pl.ds(t, nbk, stride=tc)
                new = []
                for idx, k in enumerate(chunks):
                    hr, hi = carry[2 * idx], carry[2 * idx + 1]
                    nhr = ar[idx] * hr - ai[idx] * hi + bu_scr[k, rsel, :]
                    nhi = ar[idx] * hi + ai[idx] * hr + bu_scr[nch + k, rsel, :]
                    h_scr[k, rsel, :] = nhr
                    h_scr[nch + k, rsel, :] = nhi
                    new += [nhr, nhi]
                return tuple(new)

            init = []
            for cs in cols:
                init += [st_r[:, cs], st_i[:, cs]]
            fin = lax.fori_loop(0, tc, step, tuple(init), unroll=unroll)
            for idx, cs in enumerate(cols):
                st_r[:, cs] = fin[2 * idx]
                st_i[:, cs] = fin[2 * idx + 1]
        hcat = jnp.concatenate([h_scr[k] for k in range(2 * nch)], axis=-1).astype(BF16)
        y_scr[:, j * jw:(j + 1) * jw] = jnp.dot(hcat, cc_ref[j], preferred_element_type=F32)

    y = y_scr[...] + dsk_ref[...] * u2
    g = _bdot(_gelu_tanh(y), wglu_ref[...]) + bglu_ref[...]
    out = g[:, 0:d] * jax.nn.sigmoid(g[:, d:2 * d])
    xo_ref[...] = x + g1_ref[...] * out.reshape(nbk, tc, d)

    @pl.when(c == pl.num_programs(1) - 1)
    def _():
        sro_ref[...] = st_r[...]
        sio_ref[...] = st_i[...]


def _s5_call(x, mod, ng, s0r, s0i, abr, abi, bb, cc, d_skip, wglu_bf, b_glu, nbk, tc, lw, unroll):
    a, b, d = x.shape
    ns = s0r.shape[1]
    rows = nbk * tc
    sspec = pl.BlockSpec((nbk, ns), lambda i, j: (i, 0))
    return pl.pallas_call(
        functools.partial(_s5_kernel, lw=lw, unroll=unroll),
        grid=(a // nbk, b // tc),
        in_specs=[_row_spec(nbk, tc, d), _mod_spec(nbk, d, 0), _mod_spec(nbk, d, 1), _mod_spec(nbk, d, 2),
                  _const_spec((1, d)), sspec, sspec, _const_spec((1, ns)), _const_spec((1, ns)),
                  _const_spec(bb.shape), _const_spec(cc.shape), _const_spec((1, d)),
                  _const_spec(wglu_bf.shape), _const_spec((1, 2 * d))],
        out_specs=(_row_spec(nbk, tc, d), sspec, sspec),
        out_shape=(jax.ShapeDtypeStruct((a, b, d), F32),
                   jax.ShapeDtypeStruct((a, ns), F32), jax.ShapeDtypeStruct((a, ns), F32)),
        scratch_shapes=[pltpu.VMEM((nbk, ns), F32), pltpu.VMEM((nbk, ns), F32),
                        pltpu.VMEM((bb.shape[2] // LANES, rows, LANES), F32),
                        pltpu.VMEM((bb.shape[2] // LANES, rows, LANES), F32),
                        pltpu.VMEM((rows, d), F32)],
        compiler_params=_cparams(("parallel", "arbitrary")),
        name="s5_mixer",
    )(x, mod, mod, mod, ng.reshape(1, d), s0r, s0i, abr.reshape(1, ns), abi.reshape(1, ns), bb, cc,
      d_skip.reshape(1, d), wglu_bf, b_glu.reshape(1, 2 * d))


def _run_trunk(x, mods, tabs, conv_st, kv_st, s5_st, W, b0, b1, s5_tiles):
    a, b, d = x.shape
    depth = len(mods)
    conv_ch = W['conv_w'].shape[-1]
    att_w = N_DIL * GRP_W
    new_conv, new_s5 = [], []
    new_kv = [[] for _ in range(N_DIL)]
    ei = oi = 0
    for layer in range(depth):
        mod = mods[layer]
        if layer % 2 == 0:
            u, q, k, v = _inproj_call(x, mod, W['norm_g'][layer, 0], W['w_in_bf'][ei], tabs, b0, b1, conv_ch, att_w)
            cw = (W['conv_w'][ei], W['conv_b'][ei], W['conv_ln_g'][ei], W['conv_ln_b'][ei])
            if conv_st is None:
                yc = _conv_prompt_call(u, *cw)
                new_conv.append(u[:, b - (CONV_W - 1):])
                res = [_attn_prompt_call(q, k, v, gi) for gi in range(N_DIL)]
                os_, ls_ = [r[0] for r in res], [r[1] for r in res]
                for gi, (win, _) in enumerate(DIL_GROUPS):
                    keep = min(win, b)
                    kg = k[:, b - keep:, gi * GRP_W:(gi + 1) * GRP_W].reshape(a, keep, HPG, HEAD_DIM)
                    vg = v[:, b - keep:, gi * GRP_W:(gi + 1) * GRP_W].reshape(a, keep, HPG, HEAD_DIM)
                    new_kv[gi].append(jnp.stack([kg, vg], axis=2))
            else:
                full = jnp.concatenate([conv_st[ei], u], axis=1)
                yc = jnp.transpose(_conv_sample_call(jnp.transpose(full, (1, 0, 2)), *cw), (1, 0, 2))
                new_conv.append(full[:, -(CONV_W - 1):])
                os_, ls_ = _attn_sample_call(q, k, v, [kv[ei] for kv in kv_st])
                for gi in range(N_DIL):
                    kg = k[:, :, gi * GRP_W:(gi + 1) * GRP_W].reshape(a, b, HPG, HEAD_DIM)
                    vg = v[:, :, gi * GRP_W:(gi + 1) * GRP_W].reshape(a, b, HPG, HEAD_DIM)
                    new_kv[gi].append(jnp.stack([kg, vg], axis=2))
            x = _outproj_call(x, mod, yc, os_, ls_, W['w_o_a_bf'][ei], W['w_o_b_bf'][ei], b0, b1)
            ei += 1
        else:
            ns = W['abr'].shape[1] * W['abr'].shape[2]
            if s5_st is None:
                s0r = jnp.zeros((a, ns), F32)
                s0i = jnp.zeros((a, ns), F32)
            else:
                s0r = s5_st[oi][..., 0].reshape(a, ns)
                s0i = s5_st[oi][..., 1].reshape(a, ns)
            nbk, tc, lw, unroll = s5_tiles
            x, sr, si = _s5_call(x, mod, W['norm_g'][layer, 0], s0r, s0i, W['abr'][oi], W['abi'][oi],
                                 W['bb'][oi], W['cc'][oi], W['s5_d'][oi], W['s5_w_glu_bf'][oi],
                                 W['s5_b_glu'][oi], nbk, tc, lw, unroll)
            g, p = W['abr'].shape[1:]
            new_s5.append(jnp.stack([sr.reshape(a, g, p), si.reshape(a, g, p)], axis=-1))
            oi += 1
        x = _ffn_call(x, mod, W['norm_g'][layer, 1], W['w_ff_gate_bf'][layer], W['w_ff_up_bf'][layer],
                      W['w_ff_down_bf'][layer], W['final_g'], b0, b1, final=(layer == depth - 1))
    return x, jnp.stack(new_conv), [jnp.stack(kv) for kv in new_kv], jnp.stack(new_s5)


def kernel(x_prompt, x_sample, cache_conv, cache_kv_g0, cache_kv_g1, cache_kv_g2, state_s5, c_prompt, c_sample, norm_g, final_g, w_ada, b_ada, w_in, conv_w, conv_b, conv_ln_g, conv_ln_b, w_o, s5_lam_re, s5_lam_im, s5_log_dt, s5_b_re, s5_b_im, s5_c_re, s5_c_im, s5_d, s5_w_glu, s5_b_glu, w_ff_gate, w_ff_up, w_ff_down):
    n, l, d = x_prompt.shape
    nb, t_new, _ = x_sample.shape
    depth = w_ada.shape[0]
    conv_ch = conv_w.shape[-1]

    m_all = n + nb
    m_pad = -(-m_all // 8) * 8
    c_all = jnp.concatenate([c_prompt, c_sample, jnp.zeros((m_pad - m_all, d), F32)], axis=0)
    mod_all = _ada_call(c_all, w_ada, b_ada)
    mods_p = [mod_all[i, :n].reshape(n, 1, 6 * d) for i in range(depth)]
    mods_s = [mod_all[i, n:m_all].reshape(nb, 1, 6 * d) for i in range(depth)]

    abr, abi, bbr, bbi = _s5_param_call(s5_lam_re, s5_lam_im, s5_log_dt, s5_b_re, s5_b_im)
    blocks = [_s5_block_weights(bbr[i], bbi[i], s5_c_re[i], s5_c_im[i]) for i in range(abr.shape[0])]

    W = dict(norm_g=norm_g, final_g=final_g, conv_w=conv_w, conv_b=conv_b, conv_ln_g=conv_ln_g,
             conv_ln_b=conv_ln_b, s5_d=s5_d, s5_b_glu=s5_b_glu, abr=abr, abi=abi,
             bb=[b[0] for b in blocks], cc=[b[1] for b in blocks],
             w_in_bf=w_in.astype(BF16), w_o_a_bf=w_o[:, :conv_ch].astype(BF16),
             w_o_b_bf=w_o[:, conv_ch:].astype(BF16), s5_w_glu_bf=s5_w_glu.astype(BF16),
             w_ff_gate_bf=w_ff_gate.astype(BF16), w_ff_up_bf=w_ff_up.astype(BF16),
             w_ff_down_bf=w_ff_down.astype(BF16))

    tabs_p = _rope_tables(jnp.arange(l))
    tabs_s = _rope_tables(PAST_LEN + jnp.arange(t_new))

    tm = min(512, l)
    y_p, conv_p, kvs_p, s5_p = _run_trunk(x_prompt, mods_p, tabs_p, None, None, None, W,
                                          1, tm, (n, 128, 8, 2))
    nbk = min(nb, tm // t_new)
    y_s, conv_s, kvs_s, s5_s = _run_trunk(x_sample, mods_s, tabs_s, cache_conv,
                                          [cache_kv_g0, cache_kv_g1, cache_kv_g2], state_s5, W,
                                          nbk, t_new, (nbk, t_new, 1, 2))
    return (y_p, y_s, conv_p, kvs_p[0], kvs_p[1], kvs_p[2], s5_p,
            conv_s, kvs_s[0], kvs_s[1], kvs_s[2], s5_s)
```

```python
import functools
import math

import jax
import jax.numpy as jnp
from jax import lax
from jax.experimental import pallas as pl
from jax.experimental.pallas import tpu as pltpu

F32 = jnp.float32
BF16 = jnp.bfloat16

EPS = 1e-6
CONV_W = 31
HEAD_DIM = 64
HPG = 4
DIL_GROUPS = ((128, 1), (512, 4), (2048, 16))
N_DIL = len(DIL_GROUPS)
GRP_W = HPG * HEAD_DIM
ROT_DIM = HEAD_DIM // 4
ROPE_THETA = 500000.0
BAND_BLOCK = 128
S5_GROUP = 16
S5_STATE = 64
S5_JBLK = 4
PAST_LEN = 2048
LANES = 128
VMEM_LIMIT = 56 * 1024 * 1024


def _cparams(sem):
    return pltpu.CompilerParams(dimension_semantics=sem, vmem_limit_bytes=VMEM_LIMIT)


def _const_spec(shape):
    nd = len(shape)
    return pl.BlockSpec(shape, lambda *_: (0,) * nd)


def _weight_spec(shape):
    nd = len(shape)
    return pl.BlockSpec(shape, lambda *_: (0,) * nd, pipeline_mode=pl.Buffered(1))


def _row_spec(b0, b1, cols):
    return pl.BlockSpec((b0, b1, cols), lambda i, j: (i, j, 0))


def _mod_spec(b0, d, k):
    return pl.BlockSpec((b0, 1, d), lambda i, j: (i, 0, k))


def _norm_mod(x, g, sc, sh):
    y = x * lax.rsqrt(jnp.mean(x * x, axis=-1, keepdims=True) + EPS)
    return (y * g) * (1.0 + sc) + sh


def _silu(x):
    return x * jax.nn.sigmoid(x)


def _log2(n):
    assert n > 0 and n & (n - 1) == 0, n
    return n.bit_length() - 1


def _div_pow2(x, n):
    return x >> _log2(n)


def _mod_pow2(x, n):
    return x & ((1 << _log2(n)) - 1)


def _bdot(a, b):
    return jnp.dot(a.astype(BF16), b.astype(BF16), preferred_element_type=F32)


def _bdot_nt(a, b):
    return lax.dot_general(a.astype(BF16), b.astype(BF16), (((1,), (1,)), ((), ())),
                           preferred_element_type=F32)


def _ada_kernel(c_ref, w_ref, b_ref, o_ref):
    o_ref[0] = _bdot(_silu(c_ref[...]), w_ref[0]) + b_ref[0]


def _ada_call(c_all, w_ada, b_ada):
    depth, d, d6 = w_ada.shape
    m = c_all.shape[0]
    tn = 1536
    return pl.pallas_call(
        _ada_kernel,
        grid=(depth, d6 // tn),
        in_specs=[pl.BlockSpec((m, d), lambda l, j: (0, 0)),
                  pl.BlockSpec((1, d, tn), lambda l, j: (l, 0, j)),
                  pl.BlockSpec((1, 1, tn), lambda l, j: (l, 0, j))],
        out_specs=pl.BlockSpec((1, m, tn), lambda l, j: (l, 0, j)),
        out_shape=jax.ShapeDtypeStruct((depth, m, d6), F32),
        compiler_params=_cparams(("arbitrary", "arbitrary")),
        name="ada_mod",
    )(c_all, w_ada, b_ada.reshape(depth, 1, d6))


def _inproj_kernel(x_ref, sh_ref, sc_ref, ng_ref, w_ref, ta_ref, tb_ref, tc_ref,
                   u_ref, q_ref, k_ref, v_ref, *, conv_ch, att_w):
    b0, b1, d = x_ref.shape
    rows = b0 * b1
    h = _norm_mod(x_ref[...], ng_ref[...], sc_ref[...], sh_ref[...])
    hb = h.reshape(rows, d).astype(BF16)
    a_val = jnp.dot(hb, w_ref[:, 0:conv_ch], preferred_element_type=F32)
    a_gate = jnp.dot(hb, w_ref[:, conv_ch:2 * conv_ch], preferred_element_type=F32)
    u_ref[...] = (a_val * jax.nn.sigmoid(a_gate)).reshape(b0, b1, conv_ch)

    def table(t_ref):
        return jnp.broadcast_to(t_ref[...], (b0, b1, LANES)).reshape(rows, LANES)

    ta, tb, tc = table(ta_ref), table(tb_ref), table(tc_ref)
    half = ROT_DIM // 2

    def rope(z, scale):
        outs = []
        for c in range(att_w // LANES):
            seg = z[:, c * LANES:(c + 1) * LANES]
            fwd = pltpu.roll(seg, LANES - half, 1)
            bwd = pltpu.roll(seg, half, 1)
            r = seg * ta + fwd * tb + bwd * tc
            outs.append(r * scale if scale != 1.0 else r)
        return jnp.concatenate(outs, axis=-1)

    c0 = 2 * conv_ch
    q = jnp.dot(hb, w_ref[:, c0:c0 + att_w], preferred_element_type=F32)
    q_ref[...] = rope(q, HEAD_DIM ** -0.5).reshape(b0, b1, att_w)
    k = jnp.dot(hb, w_ref[:, c0 + att_w:c0 + 2 * att_w], preferred_element_type=F32)
    k_ref[...] = rope(k, 1.0).reshape(b0, b1, att_w)
    v = jnp.dot(hb, w_ref[:, c0 + 2 * att_w:c0 + 3 * att_w], preferred_element_type=F32)
    v_ref[...] = v.reshape(b0, b1, att_w)


def _inproj_call(x, mod, ng, w_in_bf, tabs, b0, b1, conv_ch, att_w):
    a, b, d = x.shape
    ta, tb, tc = tabs
    tspec = pl.BlockSpec((1, b1, LANES), lambda i, j: (0, j, 0))
    out_shapes = (jax.ShapeDtypeStruct((a, b, conv_ch), F32),) + \
        tuple(jax.ShapeDtypeStruct((a, b, att_w), F32) for _ in range(3))
    out_specs = (_row_spec(b0, b1, conv_ch),) + tuple(_row_spec(b0, b1, att_w) for _ in range(3))
    return pl.pallas_call(
        functools.partial(_inproj_kernel, conv_ch=conv_ch, att_w=att_w),
        grid=(a // b0, b // b1),
        in_specs=[_row_spec(b0, b1, d), _mod_spec(b0, d, 0), _mod_spec(b0, d, 1),
                  _const_spec((1, d)), _weight_spec(w_in_bf.shape), tspec, tspec, tspec],
        out_specs=out_specs,
        out_shape=out_shapes,
        compiler_params=_cparams(("parallel", "parallel")),
        name="in_proj",
    )(x, mod, mod, ng.reshape(1, d), w_in_bf, ta, tb, tc)


def _rope_tables(pos):
    half = ROT_DIM // 2
    inv = jnp.float32(ROPE_THETA) ** (-(2.0 / ROT_DIM) * jnp.arange(half, dtype=F32))
    ang = pos.astype(F32)[:, None] * inv[None, :]
    cos, sin = jnp.cos(ang), jnp.sin(ang)
    n = pos.shape[0]
    ones = jnp.ones((n, HEAD_DIM - ROT_DIM), F32)
    zeros = jnp.zeros((n, HEAD_DIM - ROT_DIM), F32)
    zh = jnp.zeros((n, half), F32)
    ta = jnp.concatenate([cos, cos, ones], axis=-1)
    tb = jnp.concatenate([-sin, zh, zeros], axis=-1)
    tc = jnp.concatenate([zh, sin, zeros], axis=-1)
    rep = LANES // HEAD_DIM
    return tuple(jnp.tile(t, (1, rep))[None] for t in (ta, tb, tc))


def _ln_silu(acc, g, b):
    mu = jnp.mean(acc, axis=-1, keepdims=True)
    xc = acc - mu
    y = xc * lax.rsqrt(jnp.mean(xc * xc, axis=-1, keepdims=True) + EPS)
    return _silu(y * g + b)


def _conv_prompt_kernel(cur_ref, prev_ref, w_ref, cb_ref, g_ref, b_ref, o_ref, buf, *, halo, rb):
    j = pl.program_id(1)
    tt = cur_ref.shape[1]
    prev = prev_ref[0]
    buf[0:halo, :] = jnp.where(j == 0, jnp.zeros_like(prev), prev)
    buf[halo:halo + tt, :] = cur_ref[0]
    off = halo - (CONV_W - 1)
    for r0 in range(0, tt, rb):
        acc = w_ref[0:1, :] * buf[r0 + off:r0 + off + rb, :]
        for jt in range(1, CONV_W):
            acc = acc + w_ref[jt:jt + 1, :] * buf[r0 + off + jt:r0 + off + jt + rb, :]
        o_ref[0, r0:r0 + rb, :] = _ln_silu(acc + cb_ref[...], g_ref[...], b_ref[...])


def _conv_prompt_call(u, conv_w, conv_b, ln_g, ln_b):
    n, l, c = u.shape
    tt, halo, rb = 256, 32, 32
    ratio = tt // halo
    return pl.pallas_call(
        functools.partial(_conv_prompt_kernel, halo=halo, rb=rb),
        grid=(n, l // tt),
        in_specs=[pl.BlockSpec((1, tt, c), lambda i, j: (i, j, 0)),
                  pl.BlockSpec((1, halo, c), lambda i, j: (i, jnp.maximum(j * ratio - 1, 0), 0)),
                  _const_spec((CONV_W, c)), _const_spec((1, c)), _const_spec((1, c)), _const_spec((1, c))],
        out_specs=pl.BlockSpec((1, tt, c), lambda i, j: (i, j, 0)),
        out_shape=jax.ShapeDtypeStruct((n, l, c), F32),
        scratch_shapes=[pltpu.VMEM((tt + halo, c), F32)],
        compiler_params=_cparams(("parallel", "arbitrary")),
        name="conv_prompt",
    )(u, u, conv_w, conv_b.reshape(1, c), ln_g.reshape(1, c), ln_b.reshape(1, c))


def _conv_sample_kernel(full_ref, w_ref, cb_ref, g_ref, b_ref, o_ref):
    t_out = o_ref.shape[0]
    for t in range(t_out):
        acc = w_ref[0:1, :] * full_ref[t]
        for jt in range(1, CONV_W):
            acc = acc + w_ref[jt:jt + 1, :] * full_ref[t + jt]
        o_ref[t] = _ln_silu(acc + cb_ref[...], g_ref[...], b_ref[...])


def _conv_sample_call(full_t, conv_w, conv_b, ln_g, ln_b):
    tf, nb, c = full_t.shape
    t_out = tf - (CONV_W - 1)
    nbk = 32
    return pl.pallas_call(
        _conv_sample_kernel,
        grid=(nb // nbk,),
        in_specs=[pl.BlockSpec((tf, nbk, c), lambda i: (0, i, 0)),
                  _const_spec((CONV_W, c)), _const_spec((1, c)), _const_spec((1, c)), _const_spec((1, c))],
        out_specs=pl.BlockSpec((t_out, nbk, c), lambda i: (0, i, 0)),
        out_shape=jax.ShapeDtypeStruct((t_out, nb, c), F32),
        compiler_params=_cparams(("parallel",)),
        name="conv_sample",
    )(full_t, conv_w, conv_b.reshape(1, c), ln_g.reshape(1, c), ln_b.reshape(1, c))


def _merge_groups(os_, ls_):
    m = functools.reduce(jnp.maximum, ls_)
    es = [jnp.exp(l - m) for l in ls_]
    num = functools.reduce(lambda a, b: a + b, [e * o for e, o in zip(es, os_)])
    return num / functools.reduce(lambda a, b: a + b, es)


def _attn_prompt_kernel(*refs, tb):
    in_refs, att_ref, scr = refs[:5 * N_DIL], refs[5 * N_DIL], refs[5 * N_DIL + 1:]
    kbufs, vbufs, (o_scr, l_scr) = scr[:N_DIL], scr[N_DIL:2 * N_DIL], scr[2 * N_DIL:]
    jb = pl.program_id(1)
    bb = BAND_BLOCK
    qi = lax.broadcasted_iota(jnp.int32, (bb, 2 * bb), 0) + bb
    ki = lax.broadcasted_iota(jnp.int32, (bb, 2 * bb), 1)
    dist = qi - ki
    band = (dist >= 0) & (dist <= bb)
    older = ki < bb
    for g, (win, dil) in enumerate(DIL_GROUPS):
        q_ref, kc_ref, kp_ref, vc_ref, vp_ref = in_refs[5 * g:5 * g + 5]
        kbuf, vbuf = kbufs[g], vbufs[g]
        pb = bb * dil
        kbuf[0:pb, :] = kp_ref[0]
        kbuf[pb:pb + tb, :] = kc_ref[0]
        vbuf[0:pb, :] = vp_ref[0]
        vbuf[pb:pb + tb, :] = vc_ref[0]

        def unit(u, carry, g=g, dil=dil, pb=pb, q_ref=q_ref, kbuf=kbuf, vbuf=vbuf):
            i = _div_pow2(u, dil)
            r = _mod_pow2(u, dil)
            if dil == 1:
                start = pl.multiple_of(i * pb, bb)
                qsel, ksel = pl.ds(start, bb), pl.ds(start, 2 * bb)
            else:
                start = i * pb + r
                qsel, ksel = pl.ds(start, bb, stride=dil), pl.ds(start, 2 * bb, stride=dil)
            qu = q_ref[0, qsel, :].astype(BF16)
            ku = kbuf[ksel, :].astype(BF16)
            vu = vbuf[ksel, :].astype(BF16)
            first = jnp.logical_and(jb == 0, i == 0)
            mask = jnp.logical_and(band, jnp.logical_not(jnp.logical_and(older, first)))
            o_parts, l_parts = [], []
            for h in range(LANES // HEAD_DIM):
                cs = slice(h * HEAD_DIM, (h + 1) * HEAD_DIM)
                s = lax.dot_general(qu[:, cs], ku[:, cs], (((1,), (1,)), ((), ())), preferred_element_type=F32)
                s = jnp.where(mask, s, -jnp.inf)
                m = jnp.max(s, axis=-1, keepdims=True)
                p = jnp.exp(s - m)
                lsum = jnp.sum(p, axis=-1, keepdims=True)
                o_parts.append(jnp.dot(p.astype(BF16), vu[:, cs], preferred_element_type=F32) / lsum)
                l_parts.append(jnp.broadcast_to(m + jnp.log(lsum), (bb, HEAD_DIM)))
            o_scr[g, qsel, :] = jnp.concatenate(o_parts, axis=-1)
            l_scr[g, qsel, :] = jnp.concatenate(l_parts, axis=-1)
            return carry

        lax.fori_loop(0, tb // bb, unit, 0)

    mr = 256
    for r0 in range(0, tb, mr):
        att_ref[0, r0:r0 + mr, :] = _merge_groups([o_scr[g, r0:r0 + mr, :] for g in range(N_DIL)],
                                                  [l_scr[g, r0:r0 + mr, :] for g in range(N_DIL)])


def _attn_prompt_call(q, k, v):
    n, l, att_w = q.shape
    bb = BAND_BLOCK
    tb = bb * max(d for _, d in DIL_GROUPS)
    assert l % tb == 0 and all(w // d == bb for w, d in DIL_GROUPS) and att_w == N_DIL * GRP_W
    hp = GRP_W // LANES
    in_specs, args, kv_scr = [], [], []
    for g, (_, dil) in enumerate(DIL_GROUPS):
        pb = bb * dil
        ratio = tb // pb
        cur = pl.BlockSpec((1, tb, LANES), lambda i, j, h, g=g: (i, j, g * hp + h))
        prev = pl.BlockSpec((1, pb, LANES),
                            lambda i, j, h, g=g, ratio=ratio: (i, jnp.maximum(j * ratio - 1, 0), g * hp + h))
        in_specs += [cur, cur, prev, cur, prev]
        args += [q, k, k, v, v]
        kv_scr.append(pltpu.VMEM((pb + tb, LANES), F32))
    return pl.pallas_call(
        functools.partial(_attn_prompt_kernel, tb=tb),
        grid=(n, l // tb, hp),
        in_specs=in_specs,
        out_specs=pl.BlockSpec((1, tb, LANES), lambda i, j, h: (i, j, h)),
        out_shape=jax.ShapeDtypeStruct((n, l, GRP_W), F32),
        scratch_shapes=kv_scr + kv_scr + [pltpu.VMEM((N_DIL, tb, LANES), F32)] * 2,
        compiler_params=_cparams(("parallel", "arbitrary", "arbitrary")),
        name="attn_prompt",
    )(*args)


def _attn_sample_kernel(q_ref, kn_ref, vn_ref, c0_ref, c1_ref, c2_ref, att_ref):
    t_new = q_ref.shape[1]
    rows = HPG * t_new
    ri = lax.broadcasted_iota(jnp.int32, (rows, GRP_W), 0)
    ci = lax.broadcasted_iota(jnp.int32, (rows, GRP_W), 1)
    head_mask = _div_pow2(ci, HEAD_DIM) == _div_pow2(ri, t_new)
    os_, ls_ = [], []
    for gi, ((win, dil), c_ref) in enumerate(zip(DIL_GROUPS, (c0_ref, c1_ref, c2_ref))):
        cs = slice(gi * GRP_W, (gi + 1) * GRP_W)
        kt, vt = c_ref[0, 0, 0], c_ref[0, 0, 1]
        wc = kt.shape[1]
        qg = q_ref[0, :, cs]
        qb = jnp.where(head_mask, jnp.concatenate([qg] * HPG, axis=0), 0.0)
        kn, vn = kn_ref[0, :, cs], vn_ref[0, :, cs]
        s_c = _bdot(qb, kt)
        s_n = _bdot_nt(qb, kn)
        wb = wc
        tq = _mod_pow2(lax.broadcasted_iota(jnp.int32, (rows, wc), 0), t_new)
        ki = lax.broadcasted_iota(jnp.int32, (rows, wc), 1)
        dist = wb + tq - ki
        valid_c = (dist >= 0) & (dist <= win) & (_mod_pow2(dist, dil) == 0)
        tqn = _mod_pow2(lax.broadcasted_iota(jnp.int32, (rows, t_new), 0), t_new)
        jn = lax.broadcasted_iota(jnp.int32, (rows, t_new), 1)
        dn = tqn - jn
        valid_n = (dn >= 0) & (dn <= win) & (_mod_pow2(dn, dil) == 0)
        s_c = jnp.where(valid_c, s_c, -jnp.inf)
        s_n = jnp.where(valid_n, s_n, -jnp.inf)
        m = jnp.maximum(jnp.max(s_c, axis=-1, keepdims=True), jnp.max(s_n, axis=-1, keepdims=True))
        p_c = jnp.exp(s_c - m)
        p_n = jnp.exp(s_n - m)
        lsum = jnp.sum(p_c, axis=-1, keepdims=True) + jnp.sum(p_n, axis=-1, keepdims=True)
        o_full = (_bdot_nt(p_c, vt) + _bdot(p_n, vn)) / lsum
        lse_full = jnp.broadcast_to(m + jnp.log(lsum), (rows, GRP_W))
        o_full = jnp.where(head_mask, o_full, 0.0)
        lse_full = jnp.where(head_mask, lse_full, 0.0)
        o = o_full[0:t_new]
        lse = lse_full[0:t_new]
        for h in range(1, HPG):
            o = o + o_full[h * t_new:(h + 1) * t_new]
            lse = lse + lse_full[h * t_new:(h + 1) * t_new]
        os_.append(o)
        ls_.append(lse)
    att_ref[0] = _merge_groups(os_, ls_)


def _cache_rows_on_lanes(cache):
    ly, nb, wb = cache.shape[:3]
    return jnp.transpose(cache, (0, 1, 3, 4, 5, 2)).reshape(ly, nb, 2, GRP_W, wb)


def _attn_sample_call(q, k, v, caches_t, ei):
    nb, t_new, att_w = q.shape
    assert all(c.shape[-1] == w for c, (w, _) in zip(caches_t, DIL_GROUPS))
    qspec = pl.BlockSpec((1, t_new, att_w), lambda i: (i, 0, 0))
    cspecs = [pl.BlockSpec((1, 1) + c.shape[2:], lambda i: (ei, i, 0, 0, 0)) for c in caches_t]
    return pl.pallas_call(
        _attn_sample_kernel,
        grid=(nb,),
        in_specs=[qspec, qspec, qspec] + cspecs,
        out_specs=pl.BlockSpec((1, t_new, GRP_W), lambda i: (i, 0, 0)),
        out_shape=jax.ShapeDtypeStruct((nb, t_new, GRP_W), F32),
        compiler_params=_cparams(("parallel",)),
        name="attn_sample",
    )(q, k, v, *caches_t)


def _outproj_kernel(x_ref, g1_ref, yc_ref, att_ref, wa_ref, wb_ref, xo_ref):
    b0, b1, d = x_ref.shape
    rows = b0 * b1
    mix = _bdot(yc_ref[...].reshape(rows, -1), wa_ref[...]) + _bdot(att_ref[...].reshape(rows, -1), wb_ref[...])
    xo_ref[...] = x_ref[...] + g1_ref[...] * mix.reshape(b0, b1, d)


def _outproj_call(x, mod, yc, att, wa_bf, wb_bf, b0, b1):
    a, b, d = x.shape
    return pl.pallas_call(
        _outproj_kernel,
        grid=(a // b0, b // b1),
        in_specs=[_row_spec(b0, b1, d), _mod_spec(b0, d, 2), _row_spec(b0, b1, yc.shape[-1]),
                  _row_spec(b0, b1, att.shape[-1]), _weight_spec(wa_bf.shape), _weight_spec(wb_bf.shape)],
        out_specs=_row_spec(b0, b1, d),
        out_shape=jax.ShapeDtypeStruct((a, b, d), F32),
        compiler_params=_cparams(("parallel", "parallel")),
        name="out_proj",
    )(x, mod, yc, att, wa_bf, wb_bf)


def _ffn_kernel(x_ref, sh_ref, sc_ref, g2_ref, ng_ref, wg_ref, wu_ref, wd_ref, fg_ref, o_ref, *, fchunk, final):
    b0, b1, d = x_ref.shape
    rows = b0 * b1
    x = x_ref[...]
    hb = _norm_mod(x, ng_ref[...], sc_ref[...], sh_ref[...]).reshape(rows, d).astype(BF16)
    dff = wg_ref.shape[1]
    acc = None
    for f0 in range(0, dff, fchunk):
        g = jnp.dot(hb, wg_ref[:, f0:f0 + fchunk], preferred_element_type=F32)
        u = jnp.dot(hb, wu_ref[:, f0:f0 + fchunk], preferred_element_type=F32)
        part = jnp.dot((_silu(g) * u).astype(BF16), wd_ref[f0:f0 + fchunk, :], preferred_element_type=F32)
        acc = part if acc is None else acc + part
    xn = x + g2_ref[...] * acc.reshape(b0, b1, d)
    if final:
        xn = xn * lax.rsqrt(jnp.mean(xn * xn, axis=-1, keepdims=True) + EPS) * fg_ref[...]
    o_ref[...] = xn


def _ffn_call(x, mod, ng, wg_bf, wu_bf, wd_bf, final_g, b0, b1, final):
    a, b, d = x.shape
    dff = wg_bf.shape[1]
    fchunk = dff // 2 if (dff // 2) % LANES == 0 else dff
    return pl.pallas_call(
        functools.partial(_ffn_kernel, fchunk=fchunk, final=final),
        grid=(a // b0, b // b1),
        in_specs=[_row_spec(b0, b1, d), _mod_spec(b0, d, 3), _mod_spec(b0, d, 4), _mod_spec(b0, d, 5),
                  _const_spec((1, d)), _weight_spec(wg_bf.shape), _weight_spec(wu_bf.shape),
                  _weight_spec(wd_bf.shape), _const_spec((1, d))],
        out_specs=_row_spec(b0, b1, d),
        out_shape=jax.ShapeDtypeStruct((a, b, d), F32),
        compiler_params=_cparams(("parallel", "parallel")),
        name="ffn_final" if final else "ffn",
    )(x, mod, mod, mod, ng.reshape(1, d), wg_bf, wu_bf, wd_bf, final_g.reshape(1, d))


def _s5_param_kernel(lr_ref, li_ref, ldt_ref, br_ref, bi_ref, abr_ref, abi_ref, bbr_ref, bbi_ref):
    lr, li = lr_ref[0], li_ref[0]
    dt = jnp.exp(ldt_ref[0])
    mag = jnp.exp(lr * dt)
    ph = li * dt
    abr = mag * jnp.cos(ph)
    abi = mag * jnp.sin(ph)
    den = lr * lr + li * li
    nr = abr - 1.0
    fr = (nr * lr + abi * li) / den
    fi = (abi * lr - nr * li) / den
    abr_ref[0] = abr
    abi_ref[0] = abi
    for h in range(br_ref.shape[1]):
        br, bi = br_ref[0, h], bi_ref[0, h]
        bbr_ref[0, h] = fr * br - fi * bi
        bbi_ref[0, h] = fr * bi + fi * br


def _s5_param_call(lam_re, lam_im, log_dt, b_re, b_im):
    n_odd, g, p = lam_re.shape
    hh = b_re.shape[-1]
    bt_re = jnp.transpose(b_re, (0, 3, 1, 2))
    bt_im = jnp.transpose(b_im, (0, 3, 1, 2))
    gp = pl.BlockSpec((1, g, p), lambda l: (l, 0, 0))
    bs = pl.BlockSpec((1, hh, g, p), lambda l: (l, 0, 0, 0))
    return pl.pallas_call(
        _s5_param_kernel,
        grid=(n_odd,),
        in_specs=[gp, gp, pl.BlockSpec((1, g, 1), lambda l: (l, 0, 0)), bs, bs],
        out_specs=(gp, gp, bs, bs),
        out_shape=(jax.ShapeDtypeStruct((n_odd, g, p), F32),) * 2
        + (jax.ShapeDtypeStruct((n_odd, hh, g, p), F32),) * 2,
        compiler_params=_cparams(("arbitrary",)),
        name="s5_params",
    )(lam_re, lam_im, log_dt.reshape(n_odd, g, 1), bt_re, bt_im)


def _s5_block_weights(bbr, bbi, c_re, c_im):
    hh, g, p = bbr.shape
    gl = g // S5_JBLK
    eye = jnp.eye(gl, dtype=F32)

    def bblk(t):
        t = t.reshape(hh, S5_JBLK, gl, p).transpose(1, 2, 0, 3)
        t = t[:, :, :, None, :] * eye[None, :, None, :, None]
        return t.reshape(S5_JBLK, gl * hh, gl * p)

    def cblk(t):
        t = t.reshape(S5_JBLK, gl, hh, p).transpose(0, 1, 3, 2)
        t = t[:, :, :, None, :] * eye[None, :, None, :, None]
        return t.reshape(S5_JBLK, gl * p, gl * hh)

    bb = jnp.concatenate([bblk(bbr), bblk(bbi)], axis=-1).astype(BF16)
    cc = jnp.concatenate([cblk(c_re), cblk(-c_im)], axis=1).astype(BF16)
    return bb, cc


def _gelu_tanh(x):
    return 0.5 * x * (1.0 + jnp.tanh(math.sqrt(2.0 / math.pi) * (x + 0.044715 * (x * x * x))))


def _s5_kernel(x_ref, sh_ref, sc_ref, g1_ref, ng_ref, s0r_ref, s0i_ref, abr_ref, abi_ref, perm_ref, bb_ref, cc_ref,
               dsk_ref, wglu_ref, bglu_ref, xo_ref, sro_ref, sio_ref,
               st_r, st_i, bu_scr, h_scr, yp_scr, y_scr, *, lw, unroll):
    c = pl.program_id(1)
    nbk, tc, d = x_ref.shape
    rows = nbk * tc
    jw = d // S5_JBLK
    sw = bb_ref.shape[2] // 2
    w = lw * LANES
    sub = 8

    @pl.when(c == 0)
    def _():
        st_r[...] = s0r_ref[...]
        st_i[...] = s0i_ref[...]

    x = x_ref[...]
    u2 = _norm_mod(x, ng_ref[...], sc_ref[...], sh_ref[...]).reshape(rows, d)
    up = jnp.dot(perm_ref[...], u2.astype(BF16), preferred_element_type=F32).astype(BF16)
    for j in range(S5_JBLK):
        bu_scr[...] = jnp.dot(up[:, j * jw:(j + 1) * jw], bb_ref[j], preferred_element_type=F32)
        for l0 in range(0, sw, w):
            c_re, c_im = slice(l0, l0 + w), slice(sw + l0, sw + l0 + w)
            c_st = slice(j * sw + l0, j * sw + l0 + w)
            if nbk % sub == 0:
                ar = jnp.broadcast_to(abr_ref[:, c_st], (nbk, w))
                ai = jnp.broadcast_to(abi_ref[:, c_st], (nbk, w))

                def step(t, carry, c_re=c_re, c_im=c_im, ar=ar, ai=ai):
                    hr, hi = carry
                    rs = pl.ds(pl.multiple_of(t * nbk, nbk), nbk)
                    nhr = ar * hr - ai * hi + bu_scr[rs, c_re]
                    nhi = ar * hi + ai * hr + bu_scr[rs, c_im]
                    h_scr[rs, c_re] = nhr
                    h_scr[rs, c_im] = nhi
                    return nhr, nhi

                hr, hi = lax.fori_loop(0, tc, step, (st_r[:, c_st], st_i[:, c_st]), unroll=unroll)
                st_r[:, c_st] = hr
                st_i[:, c_st] = hi
            else:
                assert 2 * nbk == sub and tc % 2 == 0
                ar = jnp.broadcast_to(abr_ref[:, c_st], (sub, w))
                ai = jnp.broadcast_to(abi_ref[:, c_st], (sub, w))
                lower = lax.broadcasted_iota(jnp.int32, (sub, w), 0) < nbk

                def step2(i, carry, c_re=c_re, c_im=c_im, ar=ar, ai=ai, lower=lower):
                    sr, si = carry
                    rs = pl.ds(pl.multiple_of(i * sub, sub), sub)
                    xr, xi = bu_scr[rs, c_re], bu_scr[rs, c_im]
                    yr = ar * sr - ai * si + xr
                    yi = ar * si + ai * sr + xi
                    yrs, yis = pltpu.roll(yr, nbk, 0), pltpu.roll(yi, nbk, 0)
                    zr = ar * yrs - ai * yis + xr
                    zi = ar * yis + ai * yrs + xi
                    h_scr[rs, c_re] = jnp.where(lower, yr, zr)
                    h_scr[rs, c_im] = jnp.where(lower, yi, zi)
                    return (jnp.where(lower, pltpu.roll(zr, nbk, 0), zr),
                            jnp.where(lower, pltpu.roll(zi, nbk, 0), zi))

                h0r, h0i = st_r[:, c_st], st_i[:, c_st]
                init = (jnp.concatenate([h0r, h0r], axis=0), jnp.concatenate([h0i, h0i], axis=0))
                sr, si = lax.fori_loop(0, tc // 2, step2, init, unroll=unroll)
                st_r[:, c_st] = sr[0:nbk]
                st_i[:, c_st] = si[0:nbk]
        yj = jnp.dot(h_scr[...].astype(BF16), cc_ref[j], preferred_element_type=F32)
        for k in range(jw // LANES):
            yp_scr[j * (jw // LANES) + k] = yj[:, k * LANES:(k + 1) * LANES]

    nlc = d // LANES
    for k in range(nlc):
        if nbk <= tc:
            for n in range(nbk):
                y_scr[k, n * tc:(n + 1) * tc, :] = yp_scr[k, pl.ds(n, tc, stride=nbk), :]
        else:
            for t in range(tc):
                y_scr[k, pl.ds(t, nbk, stride=tc), :] = yp_scr[k, t * nbk:(t + 1) * nbk, :]
    y = jnp.concatenate([y_scr[k] for k in range(nlc)], axis=-1) + dsk_ref[...] * u2
    g = _bdot(_gelu_tanh(y), wglu_ref[...]) + bglu_ref[...]
    out = g[:, 0:d] * jax.nn.sigmoid(g[:, d:2 * d])
    xo_ref[...] = x + g1_ref[...] * out.reshape(nbk, tc, d)

    @pl.when(c == pl.num_programs(1) - 1)
    def _():
        sro_ref[...] = st_r[...]
        sio_ref[...] = st_i[...]


def _s5_call(x, mod, ng, s0r, s0i, abr, abi, bb, cc, d_skip, wglu_bf, b_glu, nbk, tc, lw, unroll):
    a, b, d = x.shape
    ns = s0r.shape[1]
    rows = nbk * tc
    dst = jnp.arange(rows)
    src = (dst % nbk) * tc + dst // nbk
    perm = (src[:, None] == jnp.arange(rows)[None, :]).astype(BF16)
    sspec = pl.BlockSpec((nbk, ns), lambda i, j: (i, 0))
    return pl.pallas_call(
        functools.partial(_s5_kernel, lw=lw, unroll=unroll),
        grid=(a // nbk, b // tc),
        in_specs=[_row_spec(nbk, tc, d), _mod_spec(nbk, d, 0), _mod_spec(nbk, d, 1), _mod_spec(nbk, d, 2),
                  _const_spec((1, d)), sspec, sspec, _const_spec((1, ns)), _const_spec((1, ns)),
                  _weight_spec(perm.shape), _weight_spec(bb.shape), _weight_spec(cc.shape), _const_spec((1, d)),
                  _weight_spec(wglu_bf.shape), _const_spec((1, 2 * d))],
        out_specs=(_row_spec(nbk, tc, d), sspec, sspec),
        out_shape=(jax.ShapeDtypeStruct((a, b, d), F32),
                   jax.ShapeDtypeStruct((a, ns), F32), jax.ShapeDtypeStruct((a, ns), F32)),
        scratch_shapes=[pltpu.VMEM((nbk, ns), F32), pltpu.VMEM((nbk, ns), F32),
                        pltpu.VMEM((rows, bb.shape[2]), F32), pltpu.VMEM((rows, bb.shape[2]), F32),
                        pltpu.VMEM((d // LANES, rows, LANES), F32), pltpu.VMEM((d // LANES, rows, LANES), F32)],
        compiler_params=_cparams(("parallel", "arbitrary")),
        name="s5_mixer",
    )(x, mod, mod, mod, ng.reshape(1, d), s0r, s0i, abr.reshape(1, ns), abi.reshape(1, ns), perm, bb, cc,
      d_skip.reshape(1, d), wglu_bf, b_glu.reshape(1, 2 * d))


def _run_trunk(x, mods, tabs, conv_st, kv_st, s5_st, W, b0, b1, s5_tiles):
    a, b, d = x.shape
    depth = len(mods)
    conv_ch = W['conv_w'].shape[-1]
    att_w = N_DIL * GRP_W
    new_conv, new_s5 = [], []
    new_kv = [[] for _ in range(N_DIL)]
    ei = oi = 0
    for layer in range(depth):
        mod = mods[layer]
        if layer % 2 == 0:
            u, q, k, v = _inproj_call(x, mod, W['norm_g'][layer, 0], W['w_in_bf'][ei], tabs, b0, b1, conv_ch, att_w)
            cw = (W['conv_w'][ei], W['conv_b'][ei], W['conv_ln_g'][ei], W['conv_ln_b'][ei])
            if conv_st is None:
                yc = _conv_prompt_call(u, *cw)
                new_conv.append(u[:, b - (CONV_W - 1):])
                att = _attn_prompt_call(q, k, v)
                for gi, (win, _) in enumerate(DIL_GROUPS):
                    keep = min(win, b)
                    kg = k[:, b - keep:, gi * GRP_W:(gi + 1) * GRP_W].reshape(a, keep, HPG, HEAD_DIM)
                    vg = v[:, b - keep:, gi * GRP_W:(gi + 1) * GRP_W].reshape(a, keep, HPG, HEAD_DIM)
                    new_kv[gi].append(jnp.stack([kg, vg], axis=2))
            else:
                full = jnp.concatenate([conv_st[ei], u], axis=1)
                yc = jnp.transpose(_conv_sample_call(jnp.transpose(full, (1, 0, 2)), *cw), (1, 0, 2))
                new_conv.append(full[:, -(CONV_W - 1):])
                att = _attn_sample_call(q, k, v, kv_st, ei)
                for gi in range(N_DIL):
                    kg = k[:, :, gi * GRP_W:(gi + 1) * GRP_W].reshape(a, b, HPG, HEAD_DIM)
                    vg = v[:, :, gi * GRP_W:(gi + 1) * GRP_W].reshape(a, b, HPG, HEAD_DIM)
                    new_kv[gi].append(jnp.stack([kg, vg], axis=2))
            x = _outproj_call(x, mod, yc, att, W['w_o_a_bf'][ei], W['w_o_b_bf'][ei], b0, b1)
            ei += 1
        else:
            ns = W['abr'].shape[1] * W['abr'].shape[2]
            if s5_st is None:
                s0r = jnp.zeros((a, ns), F32)
                s0i = jnp.zeros((a, ns), F32)
            else:
                s0r = s5_st[oi][..., 0].reshape(a, ns)
                s0i = s5_st[oi][..., 1].reshape(a, ns)
            nbk, tc, lw, unroll = s5_tiles
            x, sr, si = _s5_call(x, mod, W['norm_g'][layer, 0], s0r, s0i, W['abr'][oi], W['abi'][oi],
                                 W['bb'][oi], W['cc'][oi], W['s5_d'][oi], W['s5_w_glu_bf'][oi],
                                 W['s5_b_glu'][oi], nbk, tc, lw, unroll)
            g, p = W['abr'].shape[1:]
            new_s5.append(jnp.stack([sr.reshape(a, g, p), si.reshape(a, g, p)], axis=-1))
            oi += 1
        x = _ffn_call(x, mod, W['norm_g'][layer, 1], W['w_ff_gate_bf'][layer], W['w_ff_up_bf'][layer],
                      W['w_ff_down_bf'][layer], W['final_g'], b0, b1, final=(layer == depth - 1))
    return x, jnp.stack(new_conv), [jnp.stack(kv) for kv in new_kv], jnp.stack(new_s5)


def kernel(x_prompt, x_sample, cache_conv, cache_kv_g0, cache_kv_g1, cache_kv_g2, state_s5, c_prompt, c_sample, norm_g, final_g, w_ada, b_ada, w_in, conv_w, conv_b, conv_ln_g, conv_ln_b, w_o, s5_lam_re, s5_lam_im, s5_log_dt, s5_b_re, s5_b_im, s5_c_re, s5_c_im, s5_d, s5_w_glu, s5_b_glu, w_ff_gate, w_ff_up, w_ff_down):
    n, l, d = x_prompt.shape
    nb, t_new, _ = x_sample.shape
    depth = w_ada.shape[0]
    conv_ch = conv_w.shape[-1]

    m_all = n + nb
    m_pad = -(-m_all // 8) * 8
    c_all = jnp.concatenate([c_prompt, c_sample, jnp.zeros((m_pad - m_all, d), F32)], axis=0)
    mod_all = _ada_call(c_all, w_ada, b_ada)
    mods_p = [mod_all[i, :n].reshape(n, 1, 6 * d) for i in range(depth)]
    mods_s = [mod_all[i, n:m_all].reshape(nb, 1, 6 * d) for i in range(depth)]

    abr, abi, bbr, bbi = _s5_param_call(s5_lam_re, s5_lam_im, s5_log_dt, s5_b_re, s5_b_im)
    blocks = [_s5_block_weights(bbr[i], bbi[i], s5_c_re[i], s5_c_im[i]) for i in range(abr.shape[0])]

    W = dict(norm_g=norm_g, final_g=final_g, conv_w=conv_w, conv_b=conv_b, conv_ln_g=conv_ln_g,
             conv_ln_b=conv_ln_b, s5_d=s5_d, s5_b_glu=s5_b_glu, abr=abr, abi=abi,
             bb=[b[0] for b in blocks], cc=[b[1] for b in blocks],
             w_in_bf=w_in.astype(BF16), w_o_a_bf=w_o[:, :conv_ch].astype(BF16),
             w_o_b_bf=w_o[:, conv_ch:].astype(BF16), s5_w_glu_bf=s5_w_glu.astype(BF16),
             w_ff_gate_bf=w_ff_gate.astype(BF16), w_ff_up_bf=w_ff_up.astype(BF16),
             w_ff_down_bf=w_ff_down.astype(BF16))

    tabs_p = _rope_tables(jnp.arange(l))
    tabs_s = _rope_tables(PAST_LEN + jnp.arange(t_new))

    tm = min(512, l)
    y_p, conv_p, kvs_p, s5_p = _run_trunk(x_prompt, mods_p, tabs_p, None, None, None, W,
                                          1, tm, (n, 128, 4, 2))
    nbk = min(nb, tm // t_new)
    caches_t = [_cache_rows_on_lanes(c) for c in (cache_kv_g0, cache_kv_g1, cache_kv_g2)]
    y_s, conv_s, kvs_s, s5_s = _run_trunk(x_sample, mods_s, tabs_s, cache_conv, caches_t, state_s5, W,
                                          nbk, t_new, (nbk, t_new, 1, True))
    return (y_p, y_s, conv_p, kvs_p[0], kvs_p[1], kvs_p[2], s5_p,
            conv_s, kvs_s[0], kvs_s[1], kvs_s[2], s5_s)
```

```python
import functools
import math

import jax
import jax.numpy as jnp
from jax import lax
from jax.experimental import pallas as pl
from jax.experimental.pallas import tpu as pltpu

F32 = jnp.float32
BF16 = jnp.bfloat16

EPS = 1e-6
CONV_W = 31
HEAD_DIM = 64
HPG = 4
DIL_GROUPS = ((128, 1), (512, 4), (2048, 16))
N_DIL = len(DIL_GROUPS)
GRP_W = HPG * HEAD_DIM
ROT_DIM = HEAD_DIM // 4
ROPE_THETA = 500000.0
BAND_BLOCK = 128
S5_GROUP = 16
S5_STATE = 64
S5_JBLK = 4
PAST_LEN = 2048
LANES = 128
VMEM_LIMIT = 56 * 1024 * 1024


def _cparams(sem):
    return pltpu.CompilerParams(dimension_semantics=sem, vmem_limit_bytes=VMEM_LIMIT)


def _const_spec(shape):
    nd = len(shape)
    return pl.BlockSpec(shape, lambda *_: (0,) * nd)


def _weight_spec(shape, layer=None):
    nd = len(shape)
    if layer is None:
        return pl.BlockSpec(shape, lambda *_: (0,) * nd, pipeline_mode=pl.Buffered(1))
    return pl.BlockSpec((1,) + tuple(shape[1:]), lambda *_: (layer,) + (0,) * (nd - 1),
                        pipeline_mode=pl.Buffered(1))


def _row_spec(b0, b1, cols):
    return pl.BlockSpec((b0, b1, cols), lambda i, j: (i, j, 0))


def _mod_spec(mod, b0, d, k):
    _, layer, row0 = mod
    assert row0 % b0 == 0
    return pl.BlockSpec((1, b0, 1, d), lambda i, j: (layer, row0 // b0 + i, 0, k))


def _gain_spec(d, layer, which):
    return pl.BlockSpec((1, 1, 1, d), lambda *_: (layer, which, 0, 0))


def _norm_mod(x, g, sc, sh):
    y = x * lax.rsqrt(jnp.mean(x * x, axis=-1, keepdims=True) + EPS)
    return (y * g) * (1.0 + sc) + sh


def _silu(x):
    return x * jax.nn.sigmoid(x)


def _log2(n):
    assert n > 0 and n & (n - 1) == 0, n
    return n.bit_length() - 1


def _div_pow2(x, n):
    return x >> _log2(n)


def _mod_pow2(x, n):
    return x & ((1 << _log2(n)) - 1)


def _bdot(a, b):
    return jnp.dot(a.astype(BF16), b.astype(BF16), preferred_element_type=F32)


def _bdot_nt(a, b):
    return lax.dot_general(a.astype(BF16), b.astype(BF16), (((1,), (1,)), ((), ())),
                           preferred_element_type=F32)


def _ada_kernel(c_ref, w_ref, b_ref, o_ref):
    o_ref[0] = _bdot(_silu(c_ref[...]), w_ref[0]) + b_ref[0]


def _ada_call(c_all, w_ada, b_ada):
    depth, d, d6 = w_ada.shape
    m = c_all.shape[0]
    tn = 1536
    return pl.pallas_call(
        _ada_kernel,
        grid=(depth, d6 // tn),
        in_specs=[pl.BlockSpec((m, d), lambda l, j: (0, 0)),
                  pl.BlockSpec((1, d, tn), lambda l, j: (l, 0, j)),
                  pl.BlockSpec((1, 1, tn), lambda l, j: (l, 0, j))],
        out_specs=pl.BlockSpec((1, m, tn), lambda l, j: (l, 0, j)),
        out_shape=jax.ShapeDtypeStruct((depth, m, d6), F32),
        compiler_params=_cparams(("arbitrary", "arbitrary")),
        name="ada_mod",
    )(c_all, w_ada, b_ada.reshape(depth, 1, d6))


def _inproj_kernel(x_ref, sh_ref, sc_ref, ng_ref, w_ref, ta_ref, tb_ref, tc_ref,
                   u_ref, q_ref, k_ref, v_ref, *, conv_ch, att_w):
    b0, b1, d = x_ref.shape
    rows = b0 * b1
    h = _norm_mod(x_ref[...], ng_ref[0, 0], sc_ref[0], sh_ref[0])
    hb = h.reshape(rows, d).astype(BF16)
    a_val = jnp.dot(hb, w_ref[0, :, 0:conv_ch], preferred_element_type=F32)
    a_gate = jnp.dot(hb, w_ref[0, :, conv_ch:2 * conv_ch], preferred_element_type=F32)
    u_ref[...] = (a_val * jax.nn.sigmoid(a_gate)).reshape(b0, b1, conv_ch)

    def table(t_ref):
        return jnp.broadcast_to(t_ref[...], (b0, b1, LANES)).reshape(rows, LANES)

    ta, tb, tc = table(ta_ref), table(tb_ref), table(tc_ref)
    half = ROT_DIM // 2

    def rope(z, scale):
        outs = []
        for c in range(att_w // LANES):
            seg = z[:, c * LANES:(c + 1) * LANES]
            fwd = pltpu.roll(seg, LANES - half, 1)
            bwd = pltpu.roll(seg, half, 1)
            r = seg * ta + fwd * tb + bwd * tc
            outs.append(r * scale if scale != 1.0 else r)
        return jnp.concatenate(outs, axis=-1)

    c0 = 2 * conv_ch
    q = jnp.dot(hb, w_ref[0, :, c0:c0 + att_w], preferred_element_type=F32)
    q_ref[...] = rope(q, HEAD_DIM ** -0.5).reshape(b0, b1, att_w)
    k = jnp.dot(hb, w_ref[0, :, c0 + att_w:c0 + 2 * att_w], preferred_element_type=F32)
    k_ref[...] = rope(k, 1.0).reshape(b0, b1, att_w)
    v = jnp.dot(hb, w_ref[0, :, c0 + 2 * att_w:c0 + 3 * att_w], preferred_element_type=F32)
    v_ref[...] = v.reshape(b0, b1, att_w)


def _inproj_call(x, mod, gains, layer, w_in_bf, ei, tabs, b0, b1, conv_ch, att_w):
    a, b, d = x.shape
    ta, tb, tc = tabs
    tspec = pl.BlockSpec((1, b1, LANES), lambda i, j: (0, j, 0))
    out_shapes = (jax.ShapeDtypeStruct((a, b, conv_ch), F32),) + \
        tuple(jax.ShapeDtypeStruct((a, b, att_w), F32) for _ in range(3))
    out_specs = (_row_spec(b0, b1, conv_ch),) + tuple(_row_spec(b0, b1, att_w) for _ in range(3))
    return pl.pallas_call(
        functools.partial(_inproj_kernel, conv_ch=conv_ch, att_w=att_w),
        grid=(a // b0, b // b1),
        in_specs=[_row_spec(b0, b1, d), _mod_spec(mod, b0, d, 0), _mod_spec(mod, b0, d, 1),
                  _gain_spec(d, layer, 0), _weight_spec(w_in_bf.shape, ei), tspec, tspec, tspec],
        out_specs=out_specs,
        out_shape=out_shapes,
        compiler_params=_cparams(("parallel", "parallel")),
        name="in_proj",
    )(x, mod[0], mod[0], gains, w_in_bf, ta, tb, tc)


def _rope_tables(pos):
    half = ROT_DIM // 2
    inv = jnp.float32(ROPE_THETA) ** (-(2.0 / ROT_DIM) * jnp.arange(half, dtype=F32))
    ang = pos.astype(F32)[:, None] * inv[None, :]
    cos, sin = jnp.cos(ang), jnp.sin(ang)
    n = pos.shape[0]
    ones = jnp.ones((n, HEAD_DIM - ROT_DIM), F32)
    zeros = jnp.zeros((n, HEAD_DIM - ROT_DIM), F32)
    zh = jnp.zeros((n, half), F32)
    ta = jnp.concatenate([cos, cos, ones], axis=-1)
    tb = jnp.concatenate([-sin, zh, zeros], axis=-1)
    tc = jnp.concatenate([zh, sin, zeros], axis=-1)
    rep = LANES // HEAD_DIM
    return tuple(jnp.tile(t, (1, rep))[None] for t in (ta, tb, tc))


def _ln_silu(acc, g, b):
    mu = jnp.mean(acc, axis=-1, keepdims=True)
    xc = acc - mu
    y = xc * lax.rsqrt(jnp.mean(xc * xc, axis=-1, keepdims=True) + EPS)
    return _silu(y * g + b)


def _conv_prompt_kernel(cur_ref, prev_ref, w_ref, cb_ref, g_ref, b_ref, o_ref, buf, shf, *, halo, rb):
    j = pl.program_id(1)
    tt = cur_ref.shape[1]
    sub = shf.shape[0] + 1
    prev = prev_ref[0]
    buf[0:halo, :] = jnp.where(j == 0, jnp.zeros_like(prev), prev)
    buf[halo:halo + tt, :] = cur_ref[0]
    n8 = shf.shape[1]
    for s in range(1, sub):
        shf[s - 1] = buf[s:s + n8, :]
    off = halo - (CONV_W - 1)
    for r0 in range(0, tt, rb):
        acc = None
        for jt in range(CONV_W):
            s = (off + jt) % sub
            lo = r0 + off + jt - s
            rows = buf[lo:lo + rb, :] if s == 0 else shf[s - 1, lo:lo + rb, :]
            term = w_ref[jt:jt + 1, :] * rows
            acc = term if acc is None else acc + term
        o_ref[0, r0:r0 + rb, :] = _ln_silu(acc + cb_ref[...], g_ref[...], b_ref[...])


def _conv_prompt_call(u, conv_w, conv_b, ln_g, ln_b):
    n, l, c = u.shape
    tt, halo, rb, sub = min(512, l), 32, 32, 8
    assert l % tt == 0 and tt % halo == 0 and halo % sub == 0 and halo >= CONV_W - 1
    ratio = tt // halo
    return pl.pallas_call(
        functools.partial(_conv_prompt_kernel, halo=halo, rb=rb),
        grid=(n, l // tt),
        in_specs=[pl.BlockSpec((1, tt, c), lambda i, j: (i, j, 0)),
                  pl.BlockSpec((1, halo, c), lambda i, j: (i, jnp.maximum(j * ratio - 1, 0), 0)),
                  _const_spec((CONV_W, c)), _const_spec((1, c)), _const_spec((1, c)), _const_spec((1, c))],
        out_specs=pl.BlockSpec((1, tt, c), lambda i, j: (i, j, 0)),
        out_shape=jax.ShapeDtypeStruct((n, l, c), F32),
        scratch_shapes=[pltpu.VMEM((tt + halo, c), F32), pltpu.VMEM((sub - 1, tt + halo - sub, c), F32)],
        compiler_params=_cparams(("parallel", "arbitrary")),
        name="conv_prompt",
    )(u, u, conv_w, conv_b.reshape(1, c), ln_g.reshape(1, c), ln_b.reshape(1, c))


def _conv_sample_kernel(full_ref, w_ref, cb_ref, g_ref, b_ref, o_ref):
    t_out = o_ref.shape[0]
    for t in range(t_out):
        acc = w_ref[0:1, :] * full_ref[t]
        for jt in range(1, CONV_W):
            acc = acc + w_ref[jt:jt + 1, :] * full_ref[t + jt]
        o_ref[t] = _ln_silu(acc + cb_ref[...], g_ref[...], b_ref[...])


def _conv_sample_call(full_t, conv_w, conv_b, ln_g, ln_b):
    tf, nb, c = full_t.shape
    t_out = tf - (CONV_W - 1)
    nbk = 32
    return pl.pallas_call(
        _conv_sample_kernel,
        grid=(nb // nbk,),
        in_specs=[pl.BlockSpec((tf, nbk, c), lambda i: (0, i, 0)),
                  _const_spec((CONV_W, c)), _const_spec((1, c)), _const_spec((1, c)), _const_spec((1, c))],
        out_specs=pl.BlockSpec((t_out, nbk, c), lambda i: (0, i, 0)),
        out_shape=jax.ShapeDtypeStruct((t_out, nb, c), F32),
        compiler_params=_cparams(("parallel",)),
        name="conv_sample",
    )(full_t, conv_w, conv_b.reshape(1, c), ln_g.reshape(1, c), ln_b.reshape(1, c))


def _merge_groups(os_, ls_):
    m = functools.reduce(jnp.maximum, ls_)
    es = [jnp.exp(l - m) for l in ls_]
    num = functools.reduce(lambda a, b: a + b, [e * o for e, o in zip(es, os_)])
    return num / functools.reduce(lambda a, b: a + b, es)


def _attn_prompt_kernel(*refs, tb):
    in_refs, att_ref, scr = refs[:5 * N_DIL], refs[5 * N_DIL], refs[5 * N_DIL + 1:]
    kbufs, vbufs, (o_scr, l_scr) = scr[:N_DIL], scr[N_DIL:2 * N_DIL], scr[2 * N_DIL:]
    jb = pl.program_id(1)
    bb = BAND_BLOCK
    qi = lax.broadcasted_iota(jnp.int32, (bb, 2 * bb), 0) + bb
    ki = lax.broadcasted_iota(jnp.int32, (bb, 2 * bb), 1)
    dist = qi - ki
    band = (dist >= 0) & (dist <= bb)
    older = ki < bb
    for g, (win, dil) in enumerate(DIL_GROUPS):
        q_ref, kc_ref, kp_ref, vc_ref, vp_ref = in_refs[5 * g:5 * g + 5]
        kbuf, vbuf = kbufs[g], vbufs[g]
        pb = bb * dil
        kbuf[0:pb, :] = kp_ref[0]
        kbuf[pb:pb + tb, :] = kc_ref[0]
        vbuf[0:pb, :] = vp_ref[0]
        vbuf[pb:pb + tb, :] = vc_ref[0]

        def unit(u, carry, g=g, dil=dil, pb=pb, q_ref=q_ref, kbuf=kbuf, vbuf=vbuf):
            i = _div_pow2(u, dil)
            r = _mod_pow2(u, dil)
            if dil == 1:
                start = pl.multiple_of(i * pb, bb)
                qsel, ksel = pl.ds(start, bb), pl.ds(start, 2 * bb)
            else:
                start = i * pb + r
                qsel, ksel = pl.ds(start, bb, stride=dil), pl.ds(start, 2 * bb, stride=dil)
            qu = q_ref[0, qsel, :].astype(BF16)
            ku = kbuf[ksel, :].astype(BF16)
            vu = vbuf[ksel, :].astype(BF16)
            first = jnp.logical_and(jb == 0, i == 0)
            mask = jnp.logical_and(band, jnp.logical_not(jnp.logical_and(older, first)))
            o_parts, l_parts = [], []
            for h in range(LANES // HEAD_DIM):
                cs = slice(h * HEAD_DIM, (h + 1) * HEAD_DIM)
                s = lax.dot_general(qu[:, cs], ku[:, cs], (((1,), (1,)), ((), ())), preferred_element_type=F32)
                s = jnp.where(mask, s, -jnp.inf)
                m = jnp.max(s, axis=-1, keepdims=True)
                p = jnp.exp(s - m)
                lsum = jnp.sum(p, axis=-1, keepdims=True)
                o_parts.append(jnp.dot(p.astype(BF16), vu[:, cs], preferred_element_type=F32) / lsum)
                l_parts.append(jnp.broadcast_to(m + jnp.log(lsum), (bb, HEAD_DIM)))
            o_scr[g, qsel, :] = jnp.concatenate(o_parts, axis=-1)
            l_scr[g, qsel, :] = jnp.concatenate(l_parts, axis=-1)
            return carry

        lax.fori_loop(0, tb // bb, unit, 0, unroll=4)

    mr = 256
    for r0 in range(0, tb, mr):
        att_ref[0, r0:r0 + mr, :] = _merge_groups([o_scr[g, r0:r0 + mr, :] for g in range(N_DIL)],
                                                  [l_scr[g, r0:r0 + mr, :] for g in range(N_DIL)])


def _attn_prompt_call(q, k, v):
    n, l, att_w = q.shape
    bb = BAND_BLOCK
    tb = bb * max(d for _, d in DIL_GROUPS)
    assert l % tb == 0 and all(w // d == bb for w, d in DIL_GROUPS) and att_w == N_DIL * GRP_W
    hp = GRP_W // LANES
    in_specs, args, kv_scr = [], [], []
    for g, (_, dil) in enumerate(DIL_GROUPS):
        pb = bb * dil
        ratio = tb // pb
        cur = pl.BlockSpec((1, tb, LANES), lambda i, j, h, g=g: (i, j, g * hp + h))
        prev = pl.BlockSpec((1, pb, LANES),
                            lambda i, j, h, g=g, ratio=ratio: (i, jnp.maximum(j * ratio - 1, 0), g * hp + h))
        in_specs += [cur, cur, prev, cur, prev]
        args += [q, k, k, v, v]
        kv_scr.append(pltpu.VMEM((pb + tb, LANES), F32))
    return pl.pallas_call(
        functools.partial(_attn_prompt_kernel, tb=tb),
        grid=(n, l // tb, hp),
        in_specs=in_specs,
        out_specs=pl.BlockSpec((1, tb, LANES), lambda i, j, h: (i, j, h)),
        out_shape=jax.ShapeDtypeStruct((n, l, GRP_W), F32),
        scratch_shapes=kv_scr + kv_scr + [pltpu.VMEM((N_DIL, tb, LANES), F32)] * 2,
        compiler_params=_cparams(("parallel", "arbitrary", "arbitrary")),
        name="attn_prompt",
    )(*args)


def _attn_sample_kernel(q_ref, kn_ref, vn_ref, c0_ref, c1_ref, c2_ref, att_ref):
    t_new = q_ref.shape[1]
    rows = HPG * t_new
    ri = lax.broadcasted_iota(jnp.int32, (rows, GRP_W), 0)
    ci = lax.broadcasted_iota(jnp.int32, (rows, GRP_W), 1)
    head_mask = _div_pow2(ci, HEAD_DIM) == _div_pow2(ri, t_new)
    os_, ls_ = [], []
    for gi, ((win, dil), c_ref) in enumerate(zip(DIL_GROUPS, (c0_ref, c1_ref, c2_ref))):
        cs = slice(gi * GRP_W, (gi + 1) * GRP_W)
        kt, vt = c_ref[0, 0, 0], c_ref[0, 0, 1]
        wc = kt.shape[1]
        qg = q_ref[0, :, cs]
        qb = jnp.where(head_mask, jnp.concatenate([qg] * HPG, axis=0), 0.0)
        kn, vn = kn_ref[0, :, cs], vn_ref[0, :, cs]
        s_c = _bdot(qb, kt)
        s_n = _bdot_nt(qb, kn)
        wb = wc
        tq = _mod_pow2(lax.broadcasted_iota(jnp.int32, (rows, wc), 0), t_new)
        ki = lax.broadcasted_iota(jnp.int32, (rows, wc), 1)
        dist = wb + tq - ki
        valid_c = (dist >= 0) & (dist <= win) & (_mod_pow2(dist, dil) == 0)
        tqn = _mod_pow2(lax.broadcasted_iota(jnp.int32, (rows, t_new), 0), t_new)
        jn = lax.broadcasted_iota(jnp.int32, (rows, t_new), 1)
        dn = tqn - jn
        valid_n = (dn >= 0) & (dn <= win) & (_mod_pow2(dn, dil) == 0)
        s_c = jnp.where(valid_c, s_c, -jnp.inf)
        s_n = jnp.where(valid_n, s_n, -jnp.inf)
        m = jnp.maximum(jnp.max(s_c, axis=-1, keepdims=True), jnp.max(s_n, axis=-1, keepdims=True))
        p_c = jnp.exp(s_c - m)
        p_n = jnp.exp(s_n - m)
        lsum = jnp.sum(p_c, axis=-1, keepdims=True) + jnp.sum(p_n, axis=-1, keepdims=True)
        o_full = (_bdot_nt(p_c, vt) + _bdot(p_n, vn)) / lsum
        lse_full = jnp.broadcast_to(m + jnp.log(lsum), (rows, GRP_W))
        o_full = jnp.where(head_mask, o_full, 0.0)
        lse_full = jnp.where(head_mask, lse_full, 0.0)
        o = o_full[0:t_new]
        lse = lse_full[0:t_new]
        for h in range(1, HPG):
            o = o + o_full[h * t_new:(h + 1) * t_new]
            lse = lse + lse_full[h * t_new:(h + 1) * t_new]
        os_.append(o)
        ls_.append(lse)
    att_ref[0] = _merge_groups(os_, ls_)


def _cache_rows_on_lanes(cache):
    ly, nb, wb = cache.shape[:3]
    return jnp.transpose(cache, (0, 1, 3, 4, 5, 2)).reshape(ly, nb, 2, GRP_W, wb)


def _attn_sample_call(q, k, v, caches_t, ei):
    nb, t_new, att_w = q.shape
    assert all(c.shape[-1] == w for c, (w, _) in zip(caches_t, DIL_GROUPS))
    qspec = pl.BlockSpec((1, t_new, att_w), lambda i: (i, 0, 0))
    cspecs = [pl.BlockSpec((1, 1) + c.shape[2:], lambda i: (ei, i, 0, 0, 0)) for c in caches_t]
    return pl.pallas_call(
        _attn_sample_kernel,
        grid=(nb,),
        in_specs=[qspec, qspec, qspec] + cspecs,
        out_specs=pl.BlockSpec((1, t_new, GRP_W), lambda i: (i, 0, 0)),
        out_shape=jax.ShapeDtypeStruct((nb, t_new, GRP_W), F32),
        compiler_params=_cparams(("parallel",)),
        name="attn_sample",
    )(q, k, v, *caches_t)


def _outproj_kernel(x_ref, g1_ref, yc_ref, att_ref, wa_ref, wb_ref, xo_ref):
    b0, b1, d = x_ref.shape
    rows = b0 * b1
    mix = _bdot(yc_ref[...].reshape(rows, -1), wa_ref[0]) + _bdot(att_ref[...].reshape(rows, -1), wb_ref[0])
    xo_ref[...] = x_ref[...] + g1_ref[0] * mix.reshape(b0, b1, d)


def _outproj_call(x, mod, yc, att, wa_bf, wb_bf, ei, b0, b1):
    a, b, d = x.shape
    return pl.pallas_call(
        _outproj_kernel,
        grid=(a // b0, b // b1),
        in_specs=[_row_spec(b0, b1, d), _mod_spec(mod, b0, d, 2), _row_spec(b0, b1, yc.shape[-1]),
                  _row_spec(b0, b1, att.shape[-1]), _weight_spec(wa_bf.shape, ei), _weight_spec(wb_bf.shape, ei)],
        out_specs=_row_spec(b0, b1, d),
        out_shape=jax.ShapeDtypeStruct((a, b, d), F32),
        compiler_params=_cparams(("parallel", "parallel")),
        name="out_proj",
    )(x, mod[0], yc, att, wa_bf, wb_bf)


def _ffn_kernel(x_ref, sh_ref, sc_ref, g2_ref, ng_ref, wg_ref, wu_ref, wd_ref, fg_ref, o_ref, *, fchunk, final):
    b0, b1, d = x_ref.shape
    rows = b0 * b1
    x = x_ref[...]
    hb = _norm_mod(x, ng_ref[0, 0], sc_ref[0], sh_ref[0]).reshape(rows, d).astype(BF16)
    dff = wg_ref.shape[2]
    acc = None
    for f0 in range(0, dff, fchunk):
        g = jnp.dot(hb, wg_ref[0, :, f0:f0 + fchunk], preferred_element_type=F32)
        u = jnp.dot(hb, wu_ref[0, :, f0:f0 + fchunk], preferred_element_type=F32)
        part = jnp.dot((_silu(g) * u).astype(BF16), wd_ref[0, f0:f0 + fchunk, :], preferred_element_type=F32)
        acc = part if acc is None else acc + part
    xn = x + g2_ref[0] * acc.reshape(b0, b1, d)
    if final:
        xn = xn * lax.rsqrt(jnp.mean(xn * xn, axis=-1, keepdims=True) + EPS) * fg_ref[...]
    o_ref[...] = xn


def _ffn_call(x, mod, gains, layer, wg_bf, wu_bf, wd_bf, final_g, b0, b1, final):
    a, b, d = x.shape
    dff = wg_bf.shape[2]
    fchunk = dff // 2 if (dff // 2) % LANES == 0 else dff
    return pl.pallas_call(
        functools.partial(_ffn_kernel, fchunk=fchunk, final=final),
        grid=(a // b0, b // b1),
        in_specs=[_row_spec(b0, b1, d), _mod_spec(mod, b0, d, 3), _mod_spec(mod, b0, d, 4), _mod_spec(mod, b0, d, 5),
                  _gain_spec(d, layer, 1), _weight_spec(wg_bf.shape, layer), _weight_spec(wu_bf.shape, layer),
                  _weight_spec(wd_bf.shape, layer), _const_spec((1, d))],
        out_specs=_row_spec(b0, b1, d),
        out_shape=jax.ShapeDtypeStruct((a, b, d), F32),
        compiler_params=_cparams(("parallel", "parallel")),
        name="ffn_final" if final else "ffn",
    )(x, mod[0], mod[0], mod[0], gains, wg_bf, wu_bf, wd_bf, final_g.reshape(1, d))


def _s5_param_kernel(lr_ref, li_ref, ldt_ref, br_ref, bi_ref, abr_ref, abi_ref, bbr_ref, bbi_ref):
    lr, li = lr_ref[0], li_ref[0]
    dt = jnp.exp(ldt_ref[0])
    mag = jnp.exp(lr * dt)
    ph = li * dt
    abr = mag * jnp.cos(ph)
    abi = mag * jnp.sin(ph)
    den = lr * lr + li * li
    nr = abr - 1.0
    fr = (nr * lr + abi * li) / den
    fi = (abi * lr - nr * li) / den
    abr_ref[0] = abr
    abi_ref[0] = abi
    for h in range(br_ref.shape[1]):
        br, bi = br_ref[0, h], bi_ref[0, h]
        bbr_ref[0, h] = fr * br - fi * bi
        bbi_ref[0, h] = fr * bi + fi * br


def _s5_param_call(lam_re, lam_im, log_dt, b_re, b_im):
    n_odd, g, p = lam_re.shape
    hh = b_re.shape[-1]
    bt_re = jnp.transpose(b_re, (0, 3, 1, 2))
    bt_im = jnp.transpose(b_im, (0, 3, 1, 2))
    gp = pl.BlockSpec((1, g, p), lambda l: (l, 0, 0))
    bs = pl.BlockSpec((1, hh, g, p), lambda l: (l, 0, 0, 0))
    return pl.pallas_call(
        _s5_param_kernel,
        grid=(n_odd,),
        in_specs=[gp, gp, pl.BlockSpec((1, g, 1), lambda l: (l, 0, 0)), bs, bs],
        out_specs=(gp, gp, bs, bs),
        out_shape=(jax.ShapeDtypeStruct((n_odd, g, p), F32),) * 2
        + (jax.ShapeDtypeStruct((n_odd, hh, g, p), F32),) * 2,
        compiler_params=_cparams(("arbitrary",)),
        name="s5_params",
    )(lam_re, lam_im, log_dt.reshape(n_odd, g, 1), bt_re, bt_im)


def _s5_block_weights(bbr, bbi, c_re, c_im):
    hh, g, p = bbr.shape
    gl = g // S5_JBLK
    eye = jnp.eye(gl, dtype=F32)

    def bblk(t):
        t = t.reshape(hh, S5_JBLK, gl, p).transpose(1, 2, 0, 3)
        t = t[:, :, :, None, :] * eye[None, :, None, :, None]
        return t.reshape(S5_JBLK, gl * hh, gl * p)

    def cblk(t):
        t = t.reshape(S5_JBLK, gl, hh, p).transpose(0, 1, 3, 2)
        t = t[:, :, :, None, :] * eye[None, :, None, :, None]
        return t.reshape(S5_JBLK, gl * p, gl * hh)

    bb = jnp.concatenate([bblk(bbr), bblk(bbi)], axis=-1).astype(BF16)
    cc = jnp.concatenate([cblk(c_re), cblk(-c_im)], axis=1).astype(BF16)
    return bb, cc


def _gelu_tanh(x):
    return 0.5 * x * (1.0 + jnp.tanh(math.sqrt(2.0 / math.pi) * (x + 0.044715 * (x * x * x))))


def _s5_kernel(x_ref, sh_ref, sc_ref, g1_ref, ng_ref, s0r_ref, s0i_ref, abr_ref, abi_ref, perm_ref, bb_ref, cc_ref,
               dsk_ref, wglu_ref, bglu_ref, xo_ref, sro_ref, sio_ref,
               st_r, st_i, bu_scr, h_scr, yp_scr, y_scr, *, lw, unroll):
    c = pl.program_id(1)
    nbk, tc, d = x_ref.shape
    rows = nbk * tc
    jw = d // S5_JBLK
    sw = bb_ref.shape[2] // 2
    w = lw * LANES
    sub = 8

    @pl.when(c == 0)
    def _():
        st_r[...] = s0r_ref[...]
        st_i[...] = s0i_ref[...]

    x = x_ref[...]
    u2 = _norm_mod(x, ng_ref[0, 0], sc_ref[0], sh_ref[0]).reshape(rows, d)
    up = jnp.dot(perm_ref[...], u2.astype(BF16), preferred_element_type=F32).astype(BF16)
    def project_in(j):
        bu_scr[j % 2] = jnp.dot(up[:, j * jw:(j + 1) * jw], bb_ref[j], preferred_element_type=F32)

    project_in(0)
    for j in range(S5_JBLK):
        if j + 1 < S5_JBLK:
            project_in(j + 1)
        bu, hs = bu_scr.at[j % 2], h_scr.at[j % 2]
        for l0 in range(0, sw, w):
            c_re, c_im = slice(l0, l0 + w), slice(sw + l0, sw + l0 + w)
            c_st = slice(j * sw + l0, j * sw + l0 + w)
            if nbk % sub == 0:
                ar = jnp.broadcast_to(abr_ref[:, c_st], (nbk, w))
                ai = jnp.broadcast_to(abi_ref[:, c_st], (nbk, w))

                def step(t, carry, c_re=c_re, c_im=c_im, ar=ar, ai=ai):
                    hr, hi = carry
                    rs = pl.ds(pl.multiple_of(t * nbk, nbk), nbk)
                    nhr = ar * hr - ai * hi + bu[rs, c_re]
                    nhi = ar * hi + ai * hr + bu[rs, c_im]
                    hs[rs, c_re] = nhr
                    hs[rs, c_im] = nhi
                    return nhr, nhi

                hr, hi = lax.fori_loop(0, tc, step, (st_r[:, c_st], st_i[:, c_st]), unroll=unroll)
                st_r[:, c_st] = hr
                st_i[:, c_st] = hi
            else:
                assert 2 * nbk == sub and tc % 2 == 0
                ar = jnp.broadcast_to(abr_ref[:, c_st], (sub, w))
                ai = jnp.broadcast_to(abi_ref[:, c_st], (sub, w))
                lower = lax.broadcasted_iota(jnp.int32, (sub, w), 0) < nbk

                def step2(i, carry, c_re=c_re, c_im=c_im, ar=ar, ai=ai, lower=lower):
                    sr, si = carry
                    rs = pl.ds(pl.multiple_of(i * sub, sub), sub)
                    xr, xi = bu[rs, c_re], bu[rs, c_im]
                    yr = ar * sr - ai * si + xr
                    yi = ar * si + ai * sr + xi
                    yrs, yis = pltpu.roll(yr, nbk, 0), pltpu.roll(yi, nbk, 0)
                    zr = ar * yrs - ai * yis + xr
                    zi = ar * yis + ai * yrs + xi
                    hs[rs, c_re] = jnp.where(lower, yr, zr)
                    hs[rs, c_im] = jnp.where(lower, yi, zi)
                    return (jnp.where(lower, pltpu.roll(zr, nbk, 0), zr),
                            jnp.where(lower, pltpu.roll(zi, nbk, 0), zi))

                h0r, h0i = st_r[:, c_st], st_i[:, c_st]
                init = (jnp.concatenate([h0r, h0r], axis=0), jnp.concatenate([h0i, h0i], axis=0))
                sr, si = lax.fori_loop(0, tc // 2, step2, init, unroll=unroll)
                st_r[:, c_st] = sr[0:nbk]
                st_i[:, c_st] = si[0:nbk]
        yj = jnp.dot(hs[...].astype(BF16), cc_ref[j], preferred_element_type=F32)
        for k in range(jw // LANES):
            yp_scr[j * (jw // LANES) + k] = yj[:, k * LANES:(k + 1) * LANES]

    nlc = d // LANES
    for k in range(nlc):
        if nbk <= tc:
            for n in range(nbk):
                y_scr[k, n * tc:(n + 1) * tc, :] = yp_scr[k, pl.ds(n, tc, stride=nbk), :]
        else:
            for t in range(tc):
                y_scr[k, pl.ds(t, nbk, stride=tc), :] = yp_scr[k, t * nbk:(t + 1) * nbk, :]
    y = jnp.concatenate([y_scr[k] for k in range(nlc)], axis=-1) + dsk_ref[...] * u2
    g = _bdot(_gelu_tanh(y), wglu_ref[0]) + bglu_ref[...]
    out = g[:, 0:d] * jax.nn.sigmoid(g[:, d:2 * d])
    xo_ref[...] = x + g1_ref[0] * out.reshape(nbk, tc, d)

    @pl.when(c == pl.num_programs(1) - 1)
    def _():
        sro_ref[...] = st_r[...]
        sio_ref[...] = st_i[...]


def _s5_call(x, mod, gains, layer, s0r, s0i, abr, abi, bb, cc, d_skip, wglu_bf, oi, b_glu, nbk, tc, lw, unroll):
    a, b, d = x.shape
    ns = s0r.shape[1]
    rows = nbk * tc
    dst = jnp.arange(rows)
    src = (dst % nbk) * tc + dst // nbk
    perm = (src[:, None] == jnp.arange(rows)[None, :]).astype(BF16)
    sspec = pl.BlockSpec((nbk, ns), lambda i, j: (i, 0))
    return pl.pallas_call(
        functools.partial(_s5_kernel, lw=lw, unroll=unroll),
        grid=(a // nbk, b // tc),
        in_specs=[_row_spec(nbk, tc, d), _mod_spec(mod, nbk, d, 0), _mod_spec(mod, nbk, d, 1),
                  _mod_spec(mod, nbk, d, 2), _gain_spec(d, layer, 0), sspec, sspec,
                  _const_spec((1, ns)), _const_spec((1, ns)),
                  _weight_spec(perm.shape), _weight_spec(bb.shape), _weight_spec(cc.shape), _const_spec((1, d)),
                  _weight_spec(wglu_bf.shape, oi), _const_spec((1, 2 * d))],
        out_specs=(_row_spec(nbk, tc, d), sspec, sspec),
        out_shape=(jax.ShapeDtypeStruct((a, b, d), F32),
                   jax.ShapeDtypeStruct((a, ns), F32), jax.ShapeDtypeStruct((a, ns), F32)),
        scratch_shapes=[pltpu.VMEM((nbk, ns), F32), pltpu.VMEM((nbk, ns), F32),
                        pltpu.VMEM((2, rows, bb.shape[2]), F32), pltpu.VMEM((2, rows, bb.shape[2]), F32),
                        pltpu.VMEM((d // LANES, rows, LANES), F32), pltpu.VMEM((d // LANES, rows, LANES), F32)],
        compiler_params=_cparams(("parallel", "arbitrary")),
        name="s5_mixer",
    )(x, mod[0], mod[0], mod[0], gains, s0r, s0i, abr.reshape(1, ns), abi.reshape(1, ns), perm, bb, cc,
      d_skip.reshape(1, d), wglu_bf, b_glu.reshape(1, 2 * d))


def _run_trunk(x, mods, tabs, conv_st, kv_st, s5_st, W, b0, b1, s5_tiles):
    a, b, d = x.shape
    depth = mods[0].shape[0]
    conv_ch = W['conv_w'].shape[-1]
    att_w = N_DIL * GRP_W
    new_conv, new_s5 = [], []
    new_kv = [[] for _ in range(N_DIL)]
    ei = oi = 0
    for layer in range(depth):
        mod = (mods[0], layer, mods[1])
        if layer % 2 == 0:
            u, q, k, v = _inproj_call(x, mod, W['gains'], layer, W['w_in_bf'], ei, tabs, b0, b1, conv_ch, att_w)
            cw = (W['conv_w'][ei], W['conv_b'][ei], W['conv_ln_g'][ei], W['conv_ln_b'][ei])
            if conv_st is None:
                yc = _conv_prompt_call(u, *cw)
                new_conv.append(u[:, b - (CONV_W - 1):])
                att = _attn_prompt_call(q, k, v)
                for gi, (win, _) in enumerate(DIL_GROUPS):
                    keep = min(win, b)
                    kg = k[:, b - keep:, gi * GRP_W:(gi + 1) * GRP_W].reshape(a, keep, HPG, HEAD_DIM)
                    vg = v[:, b - keep:, gi * GRP_W:(gi + 1) * GRP_W].reshape(a, keep, HPG, HEAD_DIM)
                    new_kv[gi].append(jnp.stack([kg, vg], axis=2))
            else:
                full = jnp.concatenate([conv_st[ei], u], axis=1)
                yc = jnp.transpose(_conv_sample_call(jnp.transpose(full, (1, 0, 2)), *cw), (1, 0, 2))
                new_conv.append(full[:, -(CONV_W - 1):])
                att = _attn_sample_call(q, k, v, kv_st, ei)
                for gi in range(N_DIL):
                    kg = k[:, :, gi * GRP_W:(gi + 1) * GRP_W].reshape(a, b, HPG, HEAD_DIM)
                    vg = v[:, :, gi * GRP_W:(gi + 1) * GRP_W].reshape(a, b, HPG, HEAD_DIM)
                    new_kv[gi].append(jnp.stack([kg, vg], axis=2))
            x = _outproj_call(x, mod, yc, att, W['w_o_a_bf'], W['w_o_b_bf'], ei, b0, b1)
            ei += 1
        else:
            ns = W['abr'].shape[1] * W['abr'].shape[2]
            if s5_st is None:
                s0r = jnp.zeros((a, ns), F32)
                s0i = jnp.zeros((a, ns), F32)
            else:
                s0r = s5_st[oi][..., 0].reshape(a, ns)
                s0i = s5_st[oi][..., 1].reshape(a, ns)
            nbk, tc, lw, unroll = s5_tiles
            x, sr, si = _s5_call(x, mod, W['gains'], layer, s0r, s0i, W['abr'][oi], W['abi'][oi],
                                 W['bb'][oi], W['cc'][oi], W['s5_d'][oi], W['s5_w_glu_bf'], oi,
                                 W['s5_b_glu'][oi], nbk, tc, lw, unroll)
            g, p = W['abr'].shape[1:]
            new_s5.append(jnp.stack([sr.reshape(a, g, p), si.reshape(a, g, p)], axis=-1))
            oi += 1
        x = _ffn_call(x, mod, W['gains'], layer, W['w_ff_gate_bf'], W['w_ff_up_bf'],
                      W['w_ff_down_bf'], W['final_g'], b0, b1, final=(layer == depth - 1))
    return x, jnp.stack(new_conv), [jnp.stack(kv) for kv in new_kv], jnp.stack(new_s5)


def kernel(x_prompt, x_sample, cache_conv, cache_kv_g0, cache_kv_g1, cache_kv_g2, state_s5, c_prompt, c_sample, norm_g, final_g, w_ada, b_ada, w_in, conv_w, conv_b, conv_ln_g, conv_ln_b, w_o, s5_lam_re, s5_lam_im, s5_log_dt, s5_b_re, s5_b_im, s5_c_re, s5_c_im, s5_d, s5_w_glu, s5_b_glu, w_ff_gate, w_ff_up, w_ff_down):
    n, l, d = x_prompt.shape
    nb, t_new, _ = x_sample.shape
    depth = w_ada.shape[0]
    conv_ch = conv_w.shape[-1]

    m_all = n + nb
    m_pad = -(-m_all // 8) * 8
    c_all = jnp.concatenate([c_sample, c_prompt, jnp.zeros((m_pad - m_all, d), F32)], axis=0)
    mod4 = _ada_call(c_all, w_ada, b_ada).reshape(depth, m_pad, 1, 6 * d)

    abr, abi, bbr, bbi = _s5_param_call(s5_lam_re, s5_lam_im, s5_log_dt, s5_b_re, s5_b_im)
    blocks = [_s5_block_weights(bbr[i], bbi[i], s5_c_re[i], s5_c_im[i]) for i in range(abr.shape[0])]

    W = dict(gains=norm_g.reshape(depth, 2, 1, d), final_g=final_g, conv_w=conv_w, conv_b=conv_b, conv_ln_g=conv_ln_g,
             conv_ln_b=conv_ln_b, s5_d=s5_d, s5_b_glu=s5_b_glu, abr=abr, abi=abi,
             bb=[b[0] for b in blocks], cc=[b[1] for b in blocks],
             w_in_bf=w_in.astype(BF16), w_o_a_bf=w_o[:, :conv_ch].astype(BF16),
             w_o_b_bf=w_o[:, conv_ch:].astype(BF16), s5_w_glu_bf=s5_w_glu.astype(BF16),
             w_ff_gate_bf=w_ff_gate.astype(BF16), w_ff_up_bf=w_ff_up.astype(BF16),
             w_ff_down_bf=w_ff_down.astype(BF16))

    tabs_p = _rope_tables(jnp.arange(l))
    tabs_s = _rope_tables(PAST_LEN + jnp.arange(t_new))

    tm = min(512, l)
    y_p, conv_p, kvs_p, s5_p = _run_trunk(x_prompt, (mod4, nb), tabs_p, None, None, None, W,
                                          1, tm, (n, 128, 4, True))
    nbk = min(nb, tm // t_new)
    caches_t = [_cache_rows_on_lanes(c) for c in (cache_kv_g0, cache_kv_g1, cache_kv_g2)]
    y_s, conv_s, kvs_s, s5_s = _run_trunk(x_sample, (mod4, 0), tabs_s, cache_conv, caches_t, state_s5, W,
                                          nbk, t_new, (nbk, t_new, 1, True))
    return (y_p, y_s, conv_p, kvs_p[0], kvs_p[1], kvs_p[2], s5_p,
            conv_s, kvs_s[0], kvs_s[1], kvs_s[2], s5_s)
```

```python
import functools
import math

import jax
import jax.numpy as jnp
from jax import lax
from jax.experimental import pallas as pl
from jax.experimental.pallas import tpu as pltpu

F32 = jnp.float32
BF16 = jnp.bfloat16

EPS = 1e-6
CONV_W = 31
HEAD_DIM = 64
HPG = 4
DIL_GROUPS = ((128, 1), (512, 4), (2048, 16))
N_DIL = len(DIL_GROUPS)
GRP_W = HPG * HEAD_DIM
ROT_DIM = HEAD_DIM // 4
ROPE_THETA = 500000.0
BAND_BLOCK = 128
S5_GROUP = 16
S5_STATE = 64
S5_JBLK = 4
PAST_LEN = 2048
LANES = 128
MXU_TILE = 256
VMEM_LIMIT = 56 * 1024 * 1024


def _cparams(sem):
    return pltpu.CompilerParams(dimension_semantics=sem, vmem_limit_bytes=VMEM_LIMIT)


def _const_spec(shape):
    nd = len(shape)
    return pl.BlockSpec(shape, lambda *_: (0,) * nd)


def _weight_spec(shape, layer=None):
    nd = len(shape)
    if layer is None:
        return pl.BlockSpec(shape, lambda *_: (0,) * nd, pipeline_mode=pl.Buffered(1))
    return pl.BlockSpec((1,) + tuple(shape[1:]), lambda *_: (layer,) + (0,) * (nd - 1),
                        pipeline_mode=pl.Buffered(1))


def _row_spec(b0, b1, cols):
    return pl.BlockSpec((b0, b1, cols), lambda i, j: (i, j, 0))


def _mod_spec(mod, b0, d, k):
    _, layer, row0 = mod
    assert row0 % b0 == 0
    return pl.BlockSpec((1, b0, 1, d), lambda i, j: (layer, row0 // b0 + i, 0, k))


def _gain_spec(d, layer, which):
    return pl.BlockSpec((1, 1, 1, d), lambda *_: (layer, which, 0, 0))


def _norm_mod(x, g, sc, sh):
    y = x * lax.rsqrt(jnp.mean(x * x, axis=-1, keepdims=True) + EPS)
    return (y * g) * (1.0 + sc) + sh


def _silu(x):
    return x * jax.nn.sigmoid(x)


def _log2(n):
    assert n > 0 and n & (n - 1) == 0, n
    return n.bit_length() - 1


def _div_pow2(x, n):
    return x >> _log2(n)


def _mod_pow2(x, n):
    return x & ((1 << _log2(n)) - 1)


def _bdot(a, b):
    return jnp.dot(a.astype(BF16), b.astype(BF16), preferred_element_type=F32)


def _bdot_nt(a, b):
    return lax.dot_general(a.astype(BF16), b.astype(BF16), (((1,), (1,)), ((), ())),
                           preferred_element_type=F32)


def _ada_kernel(c_ref, w_ref, b_ref, o_ref):
    o_ref[0] = _bdot(_silu(c_ref[...]), w_ref[0]) + b_ref[0]


def _ada_call(c_all, w_ada, b_ada):
    depth, d, d6 = w_ada.shape
    m = c_all.shape[0]
    tn = 1536
    return pl.pallas_call(
        _ada_kernel,
        grid=(depth, d6 // tn),
        in_specs=[pl.BlockSpec((m, d), lambda l, j: (0, 0)),
                  pl.BlockSpec((1, d, tn), lambda l, j: (l, 0, j)),
                  pl.BlockSpec((1, 1, tn), lambda l, j: (l, 0, j))],
        out_specs=pl.BlockSpec((1, m, tn), lambda l, j: (l, 0, j)),
        out_shape=jax.ShapeDtypeStruct((depth, m, d6), F32),
        compiler_params=_cparams(("arbitrary", "arbitrary")),
        name="ada_mod",
    )(c_all, w_ada, b_ada.reshape(depth, 1, d6))


def _inproj_kernel(x_ref, sh_ref, sc_ref, ng_ref, w_ref, ta_ref, tb_ref, tc_ref,
                   u_ref, q_ref, k_ref, v_ref, *, conv_ch, att_w):
    b0, b1, d = x_ref.shape
    rows = b0 * b1
    h = _norm_mod(x_ref[...], ng_ref[0, 0], sc_ref[0], sh_ref[0])
    hb = h.reshape(rows, d).astype(BF16)
    a_val = jnp.dot(hb, w_ref[0, :, 0:conv_ch], preferred_element_type=F32)
    a_gate = jnp.dot(hb, w_ref[0, :, conv_ch:2 * conv_ch], preferred_element_type=F32)
    u_ref[...] = (a_val * jax.nn.sigmoid(a_gate)).reshape(b0, b1, conv_ch)

    def table(t_ref):
        return jnp.broadcast_to(t_ref[...], (b0, b1, LANES)).reshape(rows, LANES)

    ta, tb, tc = table(ta_ref), table(tb_ref), table(tc_ref)
    half = ROT_DIM // 2

    def rope(z, scale):
        outs = []
        for c in range(att_w // LANES):
            seg = z[:, c * LANES:(c + 1) * LANES]
            fwd = pltpu.roll(seg, LANES - half, 1)
            bwd = pltpu.roll(seg, half, 1)
            r = seg * ta + fwd * tb + bwd * tc
            outs.append(r * scale if scale != 1.0 else r)
        return jnp.concatenate(outs, axis=-1)

    c0 = 2 * conv_ch
    q = jnp.dot(hb, w_ref[0, :, c0:c0 + att_w], preferred_element_type=F32)
    q_ref[...] = rope(q, HEAD_DIM ** -0.5).reshape(b0, b1, att_w)
    k = jnp.dot(hb, w_ref[0, :, c0 + att_w:c0 + 2 * att_w], preferred_element_type=F32)
    k_ref[...] = rope(k, 1.0).reshape(b0, b1, att_w)
    v = jnp.dot(hb, w_ref[0, :, c0 + 2 * att_w:c0 + 3 * att_w], preferred_element_type=F32)
    v_ref[...] = v.reshape(b0, b1, att_w)


def _inproj_call(x, mod, gains, layer, w_in_bf, ei, tabs, b0, b1, conv_ch, att_w):
    a, b, d = x.shape
    ta, tb, tc = tabs
    tspec = pl.BlockSpec((1, b1, LANES), lambda i, j: (0, j, 0))
    out_shapes = (jax.ShapeDtypeStruct((a, b, conv_ch), F32),) + \
        tuple(jax.ShapeDtypeStruct((a, b, att_w), F32) for _ in range(3))
    out_specs = (_row_spec(b0, b1, conv_ch),) + tuple(_row_spec(b0, b1, att_w) for _ in range(3))
    return pl.pallas_call(
        functools.partial(_inproj_kernel, conv_ch=conv_ch, att_w=att_w),
        grid=(a // b0, b // b1),
        in_specs=[_row_spec(b0, b1, d), _mod_spec(mod, b0, d, 0), _mod_spec(mod, b0, d, 1),
                  _gain_spec(d, layer, 0), _weight_spec(w_in_bf.shape, ei), tspec, tspec, tspec],
        out_specs=out_specs,
        out_shape=out_shapes,
        compiler_params=_cparams(("parallel", "parallel")),
        name="in_proj",
    )(x, mod[0], mod[0], gains, w_in_bf, ta, tb, tc)


def _rope_tables(pos):
    half = ROT_DIM // 2
    inv = jnp.float32(ROPE_THETA) ** (-(2.0 / ROT_DIM) * jnp.arange(half, dtype=F32))
    ang = pos.astype(F32)[:, None] * inv[None, :]
    cos, sin = jnp.cos(ang), jnp.sin(ang)
    n = pos.shape[0]
    ones = jnp.ones((n, HEAD_DIM - ROT_DIM), F32)
    zeros = jnp.zeros((n, HEAD_DIM - ROT_DIM), F32)
    zh = jnp.zeros((n, half), F32)
    ta = jnp.concatenate([cos, cos, ones], axis=-1)
    tb = jnp.concatenate([-sin, zh, zeros], axis=-1)
    tc = jnp.concatenate([zh, sin, zeros], axis=-1)
    rep = LANES // HEAD_DIM
    return tuple(jnp.tile(t, (1, rep))[None] for t in (ta, tb, tc))


def _ln_silu(acc, g, b):
    mu = jnp.mean(acc, axis=-1, keepdims=True)
    xc = acc - mu
    y = xc * lax.rsqrt(jnp.mean(xc * xc, axis=-1, keepdims=True) + EPS)
    return _silu(y * g + b)


def _conv_prompt_kernel(cur_ref, prev_ref, w_ref, cb_ref, g_ref, b_ref, o_ref, buf, shf, *, halo, rb):
    j = pl.program_id(1)
    tt = cur_ref.shape[1]
    sub = shf.shape[0] + 1
    prev = prev_ref[0]
    buf[0:halo, :] = jnp.where(j == 0, jnp.zeros_like(prev), prev)
    buf[halo:halo + tt, :] = cur_ref[0]
    n8 = shf.shape[1]
    for s in range(1, sub):
        shf[s - 1] = buf[s:s + n8, :]
    off = halo - (CONV_W - 1)
    for r0 in range(0, tt, rb):
        acc = None
        for jt in range(CONV_W):
            s = (off + jt) % sub
            lo = r0 + off + jt - s
            rows = buf[lo:lo + rb, :] if s == 0 else shf[s - 1, lo:lo + rb, :]
            term = w_ref[jt:jt + 1, :] * rows
            acc = term if acc is None else acc + term
        o_ref[0, r0:r0 + rb, :] = _ln_silu(acc + cb_ref[...], g_ref[...], b_ref[...])


def _conv_prompt_call(u, conv_w, conv_b, ln_g, ln_b):
    n, l, c = u.shape
    tt, halo, rb, sub = min(512, l), 32, 32, 8
    assert l % tt == 0 and tt % halo == 0 and halo % sub == 0 and halo >= CONV_W - 1
    ratio = tt // halo
    return pl.pallas_call(
        functools.partial(_conv_prompt_kernel, halo=halo, rb=rb),
        grid=(n, l // tt),
        in_specs=[pl.BlockSpec((1, tt, c), lambda i, j: (i, j, 0)),
                  pl.BlockSpec((1, halo, c), lambda i, j: (i, jnp.maximum(j * ratio - 1, 0), 0)),
                  _const_spec((CONV_W, c)), _const_spec((1, c)), _const_spec((1, c)), _const_spec((1, c))],
        out_specs=pl.BlockSpec((1, tt, c), lambda i, j: (i, j, 0)),
        out_shape=jax.ShapeDtypeStruct((n, l, c), F32),
        scratch_shapes=[pltpu.VMEM((tt + halo, c), F32), pltpu.VMEM((sub - 1, tt + halo - sub, c), F32)],
        compiler_params=_cparams(("parallel", "arbitrary")),
        name="conv_prompt",
    )(u, u, conv_w, conv_b.reshape(1, c), ln_g.reshape(1, c), ln_b.reshape(1, c))


def _conv_sample_kernel(full_ref, w_ref, cb_ref, g_ref, b_ref, o_ref):
    t_out = o_ref.shape[0]
    for t in range(t_out):
        acc = w_ref[0:1, :] * full_ref[t]
        for jt in range(1, CONV_W):
            acc = acc + w_ref[jt:jt + 1, :] * full_ref[t + jt]
        o_ref[t] = _ln_silu(acc + cb_ref[...], g_ref[...], b_ref[...])


def _conv_sample_call(full_t, conv_w, conv_b, ln_g, ln_b):
    tf, nb, c = full_t.shape
    t_out = tf - (CONV_W - 1)
    nbk = 32
    return pl.pallas_call(
        _conv_sample_kernel,
        grid=(nb // nbk,),
        in_specs=[pl.BlockSpec((tf, nbk, c), lambda i: (0, i, 0)),
                  _const_spec((CONV_W, c)), _const_spec((1, c)), _const_spec((1, c)), _const_spec((1, c))],
        out_specs=pl.BlockSpec((t_out, nbk, c), lambda i: (0, i, 0)),
        out_shape=jax.ShapeDtypeStruct((t_out, nb, c), F32),
        compiler_params=_cparams(("parallel",)),
        name="conv_sample",
    )(full_t, conv_w, conv_b.reshape(1, c), ln_g.reshape(1, c), ln_b.reshape(1, c))


def _merge_groups(os_, ls_):
    m = functools.reduce(jnp.maximum, ls_)
    es = [jnp.exp(l - m) for l in ls_]
    num = functools.reduce(lambda a, b: a + b, [e * o for e, o in zip(es, os_)])
    return num / functools.reduce(lambda a, b: a + b, es)


def _attn_prompt_kernel(*refs, tb):
    in_refs, att_ref, scr = refs[:5 * N_DIL], refs[5 * N_DIL], refs[5 * N_DIL + 1:]
    kbufs, vbufs, (o_scr, l_scr) = scr[:N_DIL], scr[N_DIL:2 * N_DIL], scr[2 * N_DIL:]
    jb = pl.program_id(1)
    bb = BAND_BLOCK
    qi = lax.broadcasted_iota(jnp.int32, (bb, 2 * bb), 0) + bb
    ki = lax.broadcasted_iota(jnp.int32, (bb, 2 * bb), 1)
    dist = qi - ki
    band = (dist >= 0) & (dist <= bb)
    older = ki < bb
    for g, (win, dil) in enumerate(DIL_GROUPS):
        q_ref, kc_ref, kp_ref, vc_ref, vp_ref = in_refs[5 * g:5 * g + 5]
        kbuf, vbuf = kbufs[g], vbufs[g]
        pb = bb * dil
        kbuf[0:pb, :] = kp_ref[0]
        kbuf[pb:pb + tb, :] = kc_ref[0]
        vbuf[0:pb, :] = vp_ref[0]
        vbuf[pb:pb + tb, :] = vc_ref[0]

        def unit(u, carry, g=g, dil=dil, pb=pb, q_ref=q_ref, kbuf=kbuf, vbuf=vbuf):
            i = _div_pow2(u, dil)
            r = _mod_pow2(u, dil)
            if dil == 1:
                start = pl.multiple_of(i * pb, bb)
                qsel, ksel = pl.ds(start, bb), pl.ds(start, 2 * bb)
            else:
                start = i * pb + r
                qsel, ksel = pl.ds(start, bb, stride=dil), pl.ds(start, 2 * bb, stride=dil)
            qu = q_ref[0, qsel, :].astype(BF16)
            ku = kbuf[ksel, :].astype(BF16)
            vu = vbuf[ksel, :].astype(BF16)
            first = jnp.logical_and(jb == 0, i == 0)
            mask = jnp.logical_and(band, jnp.logical_not(jnp.logical_and(older, first)))
            o_parts, l_parts = [], []
            for h in range(LANES // HEAD_DIM):
                cs = slice(h * HEAD_DIM, (h + 1) * HEAD_DIM)
                s = lax.dot_general(qu[:, cs], ku[:, cs], (((1,), (1,)), ((), ())), preferred_element_type=F32)
                s = jnp.where(mask, s, -jnp.inf)
                m = jnp.max(s, axis=-1, keepdims=True)
                p = jnp.exp(s - m)
                lsum = jnp.sum(p, axis=-1, keepdims=True)
                o_parts.append(jnp.dot(p.astype(BF16), vu[:, cs], preferred_element_type=F32) / lsum)
                l_parts.append(jnp.broadcast_to(m + jnp.log(lsum), (bb, HEAD_DIM)))
            o_scr[g, qsel, :] = jnp.concatenate(o_parts, axis=-1)
            l_scr[g, qsel, :] = jnp.concatenate(l_parts, axis=-1)
            return carry

        lax.fori_loop(0, tb // bb, unit, 0, unroll=4)

    mr = 256
    for r0 in range(0, tb, mr):
        att_ref[0, r0:r0 + mr, :] = _merge_groups([o_scr[g, r0:r0 + mr, :] for g in range(N_DIL)],
                                                  [l_scr[g, r0:r0 + mr, :] for g in range(N_DIL)])


def _attn_prompt_call(q, k, v):
    n, l, att_w = q.shape
    bb = BAND_BLOCK
    tb = bb * max(d for _, d in DIL_GROUPS)
    assert l % tb == 0 and all(w // d == bb for w, d in DIL_GROUPS) and att_w == N_DIL * GRP_W
    hp = GRP_W // LANES
    in_specs, args, kv_scr = [], [], []
    for g, (_, dil) in enumerate(DIL_GROUPS):
        pb = bb * dil
        ratio = tb // pb
        cur = pl.BlockSpec((1, tb, LANES), lambda i, j, h, g=g: (i, j, g * hp + h))
        prev = pl.BlockSpec((1, pb, LANES),
                            lambda i, j, h, g=g, ratio=ratio: (i, jnp.maximum(j * ratio - 1, 0), g * hp + h))
        in_specs += [cur, cur, prev, cur, prev]
        args += [q, k, k, v, v]
        kv_scr.append(pltpu.VMEM((pb + tb, LANES), F32))
    return pl.pallas_call(
        functools.partial(_attn_prompt_kernel, tb=tb),
        grid=(n, l // tb, hp),
        in_specs=in_specs,
        out_specs=pl.BlockSpec((1, tb, LANES), lambda i, j, h: (i, j, h)),
        out_shape=jax.ShapeDtypeStruct((n, l, GRP_W), F32),
        scratch_shapes=kv_scr + kv_scr + [pltpu.VMEM((N_DIL, tb, LANES), F32)] * 2,
        compiler_params=_cparams(("parallel", "arbitrary", "arbitrary")),
        name="attn_prompt",
    )(*args)


def _attn_sample_kernel(q_ref, kn_ref, vn_ref, c0_ref, c1_ref, c2_ref, att_ref):
    t_new = q_ref.shape[1]
    rows = HPG * t_new
    ri = lax.broadcasted_iota(jnp.int32, (rows, GRP_W), 0)
    ci = lax.broadcasted_iota(jnp.int32, (rows, GRP_W), 1)
    head_mask = _div_pow2(ci, HEAD_DIM) == _div_pow2(ri, t_new)
    os_, ls_ = [], []
    for gi, ((win, dil), c_ref) in enumerate(zip(DIL_GROUPS, (c0_ref, c1_ref, c2_ref))):
        cs = slice(gi * GRP_W, (gi + 1) * GRP_W)
        kt, vt = c_ref[0, 0, 0], c_ref[0, 0, 1]
        wc = kt.shape[1]
        qg = q_ref[0, :, cs]
        qb = jnp.where(head_mask, jnp.concatenate([qg] * HPG, axis=0), 0.0)
        kn, vn = kn_ref[0, :, cs], vn_ref[0, :, cs]
        s_c = _bdot(qb, kt)
        s_n = _bdot_nt(qb, kn)
        wb = wc
        tq = _mod_pow2(lax.broadcasted_iota(jnp.int32, (rows, wc), 0), t_new)
        ki = lax.broadcasted_iota(jnp.int32, (rows, wc), 1)
        dist = wb + tq - ki
        valid_c = (dist >= 0) & (dist <= win) & (_mod_pow2(dist, dil) == 0)
        tqn = _mod_pow2(lax.broadcasted_iota(jnp.int32, (rows, t_new), 0), t_new)
        jn = lax.broadcasted_iota(jnp.int32, (rows, t_new), 1)
        dn = tqn - jn
        valid_n = (dn >= 0) & (dn <= win) & (_mod_pow2(dn, dil) == 0)
        s_c = jnp.where(valid_c, s_c, -jnp.inf)
        s_n = jnp.where(valid_n, s_n, -jnp.inf)
        m = jnp.maximum(jnp.max(s_c, axis=-1, keepdims=True), jnp.max(s_n, axis=-1, keepdims=True))
        p_c = jnp.exp(s_c - m)
        p_n = jnp.exp(s_n - m)
        lsum = jnp.sum(p_c, axis=-1, keepdims=True) + jnp.sum(p_n, axis=-1, keepdims=True)
        o_full = (_bdot_nt(p_c, vt) + _bdot(p_n, vn)) / lsum
        lse_full = jnp.broadcast_to(m + jnp.log(lsum), (rows, GRP_W))
        o_full = jnp.where(head_mask, o_full, 0.0)
        lse_full = jnp.where(head_mask, lse_full, 0.0)
        o = o_full[0:t_new]
        lse = lse_full[0:t_new]
        for h in range(1, HPG):
            o = o + o_full[h * t_new:(h + 1) * t_new]
            lse = lse + lse_full[h * t_new:(h + 1) * t_new]
        os_.append(o)
        ls_.append(lse)
    att_ref[0] = _merge_groups(os_, ls_)


def _cache_rows_on_lanes(cache):
    ly, nb, wb = cache.shape[:3]
    return jnp.transpose(cache, (0, 1, 3, 4, 5, 2)).reshape(ly, nb, 2, GRP_W, wb)


def _attn_sample_call(q, k, v, caches_t, ei):
    nb, t_new, att_w = q.shape
    assert all(c.shape[-1] == w for c, (w, _) in zip(caches_t, DIL_GROUPS))
    qspec = pl.BlockSpec((1, t_new, att_w), lambda i: (i, 0, 0))
    cspecs = [pl.BlockSpec((1, 1) + c.shape[2:], lambda i: (ei, i, 0, 0, 0)) for c in caches_t]
    return pl.pallas_call(
        _attn_sample_kernel,
        grid=(nb,),
        in_specs=[qspec, qspec, qspec] + cspecs,
        out_specs=pl.BlockSpec((1, t_new, GRP_W), lambda i: (i, 0, 0)),
        out_shape=jax.ShapeDtypeStruct((nb, t_new, GRP_W), F32),
        compiler_params=_cparams(("parallel",)),
        name="attn_sample",
    )(q, k, v, *caches_t)


def _outproj_kernel(x_ref, g1_ref, yc_ref, att_ref, wa_ref, wb_ref, xo_ref):
    b0, b1, d = x_ref.shape
    rows = b0 * b1
    mix = _bdot(yc_ref[...].reshape(rows, -1), wa_ref[0]) + _bdot(att_ref[...].reshape(rows, -1), wb_ref[0])
    xo_ref[...] = x_ref[...] + g1_ref[0] * mix.reshape(b0, b1, d)


def _outproj_call(x, mod, yc, att, wa_bf, wb_bf, ei, b0, b1):
    a, b, d = x.shape
    return pl.pallas_call(
        _outproj_kernel,
        grid=(a // b0, b // b1),
        in_specs=[_row_spec(b0, b1, d), _mod_spec(mod, b0, d, 2), _row_spec(b0, b1, yc.shape[-1]),
                  _row_spec(b0, b1, att.shape[-1]), _weight_spec(wa_bf.shape, ei), _weight_spec(wb_bf.shape, ei)],
        out_specs=_row_spec(b0, b1, d),
        out_shape=jax.ShapeDtypeStruct((a, b, d), F32),
        compiler_params=_cparams(("parallel", "parallel")),
        name="out_proj",
    )(x, mod[0], yc, att, wa_bf, wb_bf)


def _ffn_kernel(x_ref, sh_ref, sc_ref, g2_ref, ng_ref, wg_ref, wu_ref, wd_ref, fg_ref, o_ref, *, fchunks, final):
    b0, b1, d = x_ref.shape
    rows = b0 * b1
    x = x_ref[...]
    hb = _norm_mod(x, ng_ref[0, 0], sc_ref[0], sh_ref[0]).reshape(rows, d).astype(BF16)
    dff = wg_ref.shape[2]
    acc = None
    for f0, fchunk in fchunks:
        g = jnp.dot(hb, wg_ref[0, :, f0:f0 + fchunk], preferred_element_type=F32)
        u = jnp.dot(hb, wu_ref[0, :, f0:f0 + fchunk], preferred_element_type=F32)
        part = jnp.dot((_silu(g) * u).astype(BF16), wd_ref[0, f0:f0 + fchunk, :], preferred_element_type=F32)
        acc = part if acc is None else acc + part
    xn = x + g2_ref[0] * acc.reshape(b0, b1, d)
    if final:
        xn = xn * lax.rsqrt(jnp.mean(xn * xn, axis=-1, keepdims=True) + EPS) * fg_ref[...]
    o_ref[...] = xn


def _ffn_call(x, mod, gains, layer, wg_bf, wu_bf, wd_bf, final_g, b0, b1, final):
    a, b, d = x.shape
    dff = wg_bf.shape[2]
    tiles = -(-dff // MXU_TILE)
    f_split = min(dff, -(-tiles // 2) * MXU_TILE)
    fchunks = tuple((f0, f1 - f0) for f0, f1 in ((0, f_split), (f_split, dff)) if f1 > f0)
    return pl.pallas_call(
        functools.partial(_ffn_kernel, fchunks=fchunks, final=final),
        grid=(a // b0, b // b1),
        in_specs=[_row_spec(b0, b1, d), _mod_spec(mod, b0, d, 3), _mod_spec(mod, b0, d, 4), _mod_spec(mod, b0, d, 5),
                  _gain_spec(d, layer, 1), _weight_spec(wg_bf.shape, layer), _weight_spec(wu_bf.shape, layer),
                  _weight_spec(wd_bf.shape, layer), _const_spec((1, d))],
        out_specs=_row_spec(b0, b1, d),
        out_shape=jax.ShapeDtypeStruct((a, b, d), F32),
        compiler_params=_cparams(("parallel", "parallel")),
        name="ffn_final" if final else "ffn",
    )(x, mod[0], mod[0], mod[0], gains, wg_bf, wu_bf, wd_bf, final_g.reshape(1, d))


def _s5_param_kernel(lr_ref, li_ref, ldt_ref, br_ref, bi_ref, abr_ref, abi_ref, bbr_ref, bbi_ref,
                     sbr_ref, sbi_ref):
    lr, li = lr_ref[0], li_ref[0]
    dt = jnp.exp(ldt_ref[0])
    mag = jnp.exp(lr * dt)
    ph = li * dt
    abr = mag * jnp.cos(ph)
    abi = mag * jnp.sin(ph)
    den = lr * lr + li * li
    nr = abr - 1.0
    fr = (nr * lr + abi * li) / den
    fi = (abi * lr - nr * li) / den
    abr_ref[0] = abr
    abi_ref[0] = abi
    for h in range(br_ref.shape[1]):
        br, bi = br_ref[0, h], bi_ref[0, h]
        bbr = fr * br - fi * bi
        bbi = fr * bi + fi * br
        bbr_ref[0, h] = bbr
        bbi_ref[0, h] = bbi
        sbr_ref[0, h] = abr * bbr - abi * bbi
        sbi_ref[0, h] = abr * bbi + abi * bbr


def _s5_param_call(lam_re, lam_im, log_dt, b_re, b_im):
    n_odd, g, p = lam_re.shape
    hh = b_re.shape[-1]
    bt_re = jnp.transpose(b_re, (0, 3, 1, 2))
    bt_im = jnp.transpose(b_im, (0, 3, 1, 2))
    gp = pl.BlockSpec((1, g, p), lambda l: (l, 0, 0))
    bs = pl.BlockSpec((1, hh, g, p), lambda l: (l, 0, 0, 0))
    return pl.pallas_call(
        _s5_param_kernel,
        grid=(n_odd,),
        in_specs=[gp, gp, pl.BlockSpec((1, g, 1), lambda l: (l, 0, 0)), bs, bs],
        out_specs=(gp, gp, bs, bs, bs, bs),
        out_shape=(jax.ShapeDtypeStruct((n_odd, g, p), F32),) * 2
        + (jax.ShapeDtypeStruct((n_odd, hh, g, p), F32),) * 4,
        compiler_params=_cparams(("arbitrary",)),
        name="s5_params",
    )(lam_re, lam_im, log_dt.reshape(n_odd, g, 1), bt_re, bt_im)


def _s5_block_weights(bbr, bbi, sbr, sbi, c_re, c_im):
    hh, g, p = bbr.shape
    gl = g // S5_JBLK
    eye = jnp.eye(gl, dtype=F32)

    def bblk(t):
        t = t.reshape(hh, S5_JBLK, gl, p).transpose(1, 2, 0, 3)
        t = t[:, :, :, None, :] * eye[None, :, None, :, None]
        return t.reshape(S5_JBLK, gl * hh, gl * p)

    def cblk(t):
        t = t.reshape(S5_JBLK, gl, hh, p).transpose(0, 1, 3, 2)
        t = t[:, :, :, None, :] * eye[None, :, None, :, None]
        return t.reshape(S5_JBLK, gl * p, gl * hh)

    bb = jnp.concatenate([bblk(bbr), bblk(bbi)], axis=-1).astype(BF16)
    sb = jnp.concatenate([bblk(sbr), bblk(sbi)], axis=-1).astype(BF16)
    cc = jnp.concatenate([cblk(c_re), cblk(-c_im)], axis=1).astype(BF16)
    return bb, jnp.concatenate([bb, sb], axis=1), cc


def _gelu_tanh(x):
    return 0.5 * x * (1.0 + jnp.tanh(math.sqrt(2.0 / math.pi) * (x + 0.044715 * (x * x * x))))


def _s5_kernel(x_ref, sh_ref, sc_ref, g1_ref, ng_ref, s0r_ref, s0i_ref, abr_ref, abi_ref, perm_ref, bb_ref, cc_ref,
               dsk_ref, wglu_ref, bglu_ref, xo_ref, sro_ref, sio_ref,
               st_r, st_i, bu_scr, h_scr, yp_scr, y_scr, *, lw, unroll):
    c = pl.program_id(1)
    nbk, tc, d = x_ref.shape
    rows = nbk * tc
    jw = d // S5_JBLK
    sw = bb_ref.shape[2] // 2
    w = lw * LANES
    sub = 8

    two_step = nbk % sub != 0
    if two_step:
        assert 2 * nbk == sub and tc % 2 == 0 and bb_ref.shape[1] == 2 * jw

    @pl.when(c == 0)
    def _():
        if two_step:
            st_r[0:nbk, :] = jnp.zeros((nbk, st_r.shape[1]), F32)
            st_i[0:nbk, :] = jnp.zeros((nbk, st_i.shape[1]), F32)
            st_r[nbk:sub, :] = s0r_ref[...]
            st_i[nbk:sub, :] = s0i_ref[...]
        else:
            st_r[...] = s0r_ref[...]
            st_i[...] = s0i_ref[...]

    x = x_ref[...]
    u2 = _norm_mod(x, ng_ref[0, 0], sc_ref[0], sh_ref[0]).reshape(rows, d)
    ub = u2.astype(BF16)
    up = jnp.dot(perm_ref[0], ub, preferred_element_type=F32).astype(BF16)
    if two_step:
        ups = jnp.dot(perm_ref[1], ub, preferred_element_type=F32).astype(BF16)

    def project_in(j):
        lhs = up[:, j * jw:(j + 1) * jw]
        if two_step:
            lhs = jnp.concatenate([lhs, ups[:, j * jw:(j + 1) * jw]], axis=1)
        bu_scr[j % 2] = jnp.dot(lhs, bb_ref[j], preferred_element_type=F32)

    project_in(0)
    for j in range(S5_JBLK):
        if j + 1 < S5_JBLK:
            project_in(j + 1)
        bu, hs = bu_scr.at[j % 2], h_scr.at[j % 2]
        for l0 in range(0, sw, w):
            c_re, c_im = slice(l0, l0 + w), slice(sw + l0, sw + l0 + w)
            c_st = slice(j * sw + l0, j * sw + l0 + w)
            if nbk % sub == 0:
                ar = jnp.broadcast_to(abr_ref[:, c_st], (nbk, w))
                ai = jnp.broadcast_to(abi_ref[:, c_st], (nbk, w))

                def step(t, carry, c_re=c_re, c_im=c_im, ar=ar, ai=ai):
                    hr, hi = carry
                    rs = pl.ds(pl.multiple_of(t * nbk, nbk), nbk)
                    nhr = ar * hr - ai * hi + bu[rs, c_re]
                    nhi = ar * hi + ai * hr + bu[rs, c_im]
                    hs[rs, c_re] = nhr
                    hs[rs, c_im] = nhi
                    return nhr, nhi

                hr, hi = lax.fori_loop(0, tc, step, (st_r[:, c_st], st_i[:, c_st]), unroll=unroll)
                st_r[:, c_st] = hr
                st_i[:, c_st] = hi
            else:
                ar = jnp.broadcast_to(abr_ref[:, c_st], (sub, w))
                ai = jnp.broadcast_to(abi_ref[:, c_st], (sub, w))
                a2r, a2i = ar * ar - ai * ai, 2.0 * (ar * ai)
                lower = lax.broadcasted_iota(jnp.int32, (sub, w), 0) < nbk
                pr, pi = st_r[:, c_st], st_i[:, c_st]
                xr, xi = bu[0:sub, c_re], bu[0:sub, c_im]
                qr, qi = pltpu.roll(pr, nbk, 0), pltpu.roll(pi, nbk, 0)
                hr = jnp.where(lower, ar * qr - ai * qi, a2r * pr - a2i * pi) + xr
                hi = jnp.where(lower, ar * qi + ai * qr, a2r * pi + a2i * pr) + xi
                hs[0:sub, c_re] = hr
                hs[0:sub, c_im] = hi

                def step2(i, carry, c_re=c_re, c_im=c_im, a2r=a2r, a2i=a2i):
                    sr, si = carry
                    rs = pl.ds(pl.multiple_of(i * sub, sub), sub)
                    nr = a2r * sr - a2i * si + bu[rs, c_re]
                    ni = a2r * si + a2i * sr + bu[rs, c_im]
                    hs[rs, c_re] = nr
                    hs[rs, c_im] = ni
                    return nr, ni

                sr, si = lax.fori_loop(1, tc // 2, step2, (hr, hi), unroll=unroll)
                st_r[:, c_st] = sr
                st_i[:, c_st] = si
        yj = jnp.dot(hs[...].astype(BF16), cc_ref[j], preferred_element_type=F32)
        for k in range(jw // LANES):
            yp_scr[j * (jw // LANES) + k] = yj[:, k * LANES:(k + 1) * LANES]

    nlc = d // LANES
    for k in range(nlc):
        if nbk <= tc:
            for n in range(nbk):
                y_scr[k, n * tc:(n + 1) * tc, :] = yp_scr[k, pl.ds(n, tc, stride=nbk), :]
        else:
            for t in range(tc):
                y_scr[k, pl.ds(t, nbk, stride=tc), :] = yp_scr[k, t * nbk:(t + 1) * nbk, :]
    y = jnp.concatenate([y_scr[k] for k in range(nlc)], axis=-1) + dsk_ref[...] * u2
    g = _bdot(_gelu_tanh(y), wglu_ref[0]) + bglu_ref[...]
    out = g[:, 0:d] * jax.nn.sigmoid(g[:, d:2 * d])
    xo_ref[...] = x + g1_ref[0] * out.reshape(nbk, tc, d)

    @pl.when(c == pl.num_programs(1) - 1)
    def _():
        if two_step:
            sro_ref[...] = st_r[nbk:sub, :]
            sio_ref[...] = st_i[nbk:sub, :]
        else:
            sro_ref[...] = st_r[...]
            sio_ref[...] = st_i[...]


def _s5_call(x, mod, gains, layer, s0r, s0i, abr, abi, bb, cc, d_skip, wglu_bf, oi, b_glu, nbk, tc, lw, unroll):
    a, b, d = x.shape
    ns = s0r.shape[1]
    rows = nbk * tc
    dst = jnp.arange(rows)
    src = (dst % nbk) * tc + dst // nbk
    col = jnp.arange(rows)[None, :]
    perm = jnp.stack([src[:, None] == col,
                      jnp.logical_and(src[:, None] - 1 == col, (dst // nbk > 0)[:, None])]).astype(BF16)
    st_rows = nbk if nbk % 8 == 0 else 2 * nbk
    sspec = pl.BlockSpec((nbk, ns), lambda i, j: (i, 0))
    return pl.pallas_call(
        functools.partial(_s5_kernel, lw=lw, unroll=unroll),
        grid=(a // nbk, b // tc),
        in_specs=[_row_spec(nbk, tc, d), _mod_spec(mod, nbk, d, 0), _mod_spec(mod, nbk, d, 1),
                  _mod_spec(mod, nbk, d, 2), _gain_spec(d, layer, 0), sspec, sspec,
                  _const_spec((1, ns)), _const_spec((1, ns)),
                  _weight_spec(perm.shape), _weight_spec(bb.shape), _weight_spec(cc.shape), _const_spec((1, d)),
                  _weight_spec(wglu_bf.shape, oi), _const_spec((1, 2 * d))],
        out_specs=(_row_spec(nbk, tc, d), sspec, sspec),
        out_shape=(jax.ShapeDtypeStruct((a, b, d), F32),
                   jax.ShapeDtypeStruct((a, ns), F32), jax.ShapeDtypeStruct((a, ns), F32)),
        scratch_shapes=[pltpu.VMEM((st_rows, ns), F32), pltpu.VMEM((st_rows, ns), F32),
                        pltpu.VMEM((2, rows, bb.shape[2]), F32), pltpu.VMEM((2, rows, bb.shape[2]), F32),
                        pltpu.VMEM((d // LANES, rows, LANES), F32), pltpu.VMEM((d // LANES, rows, LANES), F32)],
        compiler_params=_cparams(("parallel", "arbitrary")),
        name="s5_mixer",
    )(x, mod[0], mod[0], mod[0], gains, s0r, s0i, abr.reshape(1, ns), abi.reshape(1, ns), perm, bb, cc,
      d_skip.reshape(1, d), wglu_bf, b_glu.reshape(1, 2 * d))


def _run_trunk(x, mods, tabs, conv_st, kv_st, s5_st, W, b0, b1, s5_tiles):
    a, b, d = x.shape
    depth = mods[0].shape[0]
    conv_ch = W['conv_w'].shape[-1]
    att_w = N_DIL * GRP_W
    new_conv, new_s5 = [], []
    new_kv = [[] for _ in range(N_DIL)]
    ei = oi = 0
    for layer in range(depth):
        mod = (mods[0], layer, mods[1])
        if layer % 2 == 0:
            u, q, k, v = _inproj_call(x, mod, W['gains'], layer, W['w_in_bf'], ei, tabs, b0, b1, conv_ch, att_w)
            cw = (W['conv_w'][ei], W['conv_b'][ei], W['conv_ln_g'][ei], W['conv_ln_b'][ei])
            if conv_st is None:
                yc = _conv_prompt_call(u, *cw)
                new_conv.append(u[:, b - (CONV_W - 1):])
                att = _attn_prompt_call(q, k, v)
                for gi, (win, _) in enumerate(DIL_GROUPS):
                    keep = min(win, b)
                    kg = k[:, b - keep:, gi * GRP_W:(gi + 1) * GRP_W].reshape(a, keep, HPG, HEAD_DIM)
                    vg = v[:, b - keep:, gi * GRP_W:(gi + 1) * GRP_W].reshape(a, keep, HPG, HEAD_DIM)
                    new_kv[gi].append(jnp.stack([kg, vg], axis=2))
            else:
                full = jnp.concatenate([conv_st[ei], u], axis=1)
                yc = jnp.transpose(_conv_sample_call(jnp.transpose(full, (1, 0, 2)), *cw), (1, 0, 2))
                new_conv.append(full[:, -(CONV_W - 1):])
                att = _attn_sample_call(q, k, v, kv_st, ei)
                for gi in range(N_DIL):
                    kg = k[:, :, gi * GRP_W:(gi + 1) * GRP_W].reshape(a, b, HPG, HEAD_DIM)
                    vg = v[:, :, gi * GRP_W:(gi + 1) * GRP_W].reshape(a, b, HPG, HEAD_DIM)
                    new_kv[gi].append(jnp.stack([kg, vg], axis=2))
            x = _outproj_call(x, mod, yc, att, W['w_o_a_bf'], W['w_o_b_bf'], ei, b0, b1)
            ei += 1
        else:
            ns = W['abr'].shape[1] * W['abr'].shape[2]
            if s5_st is None:
                s0r = jnp.zeros((a, ns), F32)
                s0i = jnp.zeros((a, ns), F32)
            else:
                s0r = s5_st[oi][..., 0].reshape(a, ns)
                s0i = s5_st[oi][..., 1].reshape(a, ns)
            nbk, tc, lw, unroll = s5_tiles
            bb = W['bb'][oi] if nbk % 8 == 0 else W['bb2'][oi]
            x, sr, si = _s5_call(x, mod, W['gains'], layer, s0r, s0i, W['abr'][oi], W['abi'][oi],
                                 bb, W['cc'][oi], W['s5_d'][oi], W['s5_w_glu_bf'], oi,
                                 W['s5_b_glu'][oi], nbk, tc, lw, unroll)
            g, p = W['abr'].shape[1:]
            new_s5.append(jnp.stack([sr.reshape(a, g, p), si.reshape(a, g, p)], axis=-1))
            oi += 1
        x = _ffn_call(x, mod, W['gains'], layer, W['w_ff_gate_bf'], W['w_ff_up_bf'],
                      W['w_ff_down_bf'], W['final_g'], b0, b1, final=(layer == depth - 1))
    return x, jnp.stack(new_conv), [jnp.stack(kv) for kv in new_kv], jnp.stack(new_s5)


def kernel(x_prompt, x_sample, cache_conv, cache_kv_g0, cache_kv_g1, cache_kv_g2, state_s5, c_prompt, c_sample, norm_g, final_g, w_ada, b_ada, w_in, conv_w, conv_b, conv_ln_g, conv_ln_b, w_o, s5_lam_re, s5_lam_im, s5_log_dt, s5_b_re, s5_b_im, s5_c_re, s5_c_im, s5_d, s5_w_glu, s5_b_glu, w_ff_gate, w_ff_up, w_ff_down):
    n, l, d = x_prompt.shape
    nb, t_new, _ = x_sample.shape
    depth = w_ada.shape[0]
    conv_ch = conv_w.shape[-1]

    m_all = n + nb
    m_pad = -(-m_all // 8) * 8
    c_all = jnp.concatenate([c_sample, c_prompt, jnp.zeros((m_pad - m_all, d), F32)], axis=0)
    mod4 = _ada_call(c_all, w_ada, b_ada).reshape(depth, m_pad, 1, 6 * d)

    abr, abi, bbr, bbi, sbr, sbi = _s5_param_call(s5_lam_re, s5_lam_im, s5_log_dt, s5_b_re, s5_b_im)
    blocks = [_s5_block_weights(bbr[i], bbi[i], sbr[i], sbi[i], s5_c_re[i], s5_c_im[i])
              for i in range(abr.shape[0])]

    W = dict(gains=norm_g.reshape(depth, 2, 1, d), final_g=final_g, conv_w=conv_w, conv_b=conv_b, conv_ln_g=conv_ln_g,
             conv_ln_b=conv_ln_b, s5_d=s5_d, s5_b_glu=s5_b_glu, abr=abr, abi=abi,
             bb=[b[0] for b in blocks], bb2=[b[1] for b in blocks], cc=[b[2] for b in blocks],
             w_in_bf=w_in.astype(BF16), w_o_a_bf=w_o[:, :conv_ch].astype(BF16),
             w_o_b_bf=w_o[:, conv_ch:].astype(BF16), s5_w_glu_bf=s5_w_glu.astype(BF16),
             w_ff_gate_bf=w_ff_gate.astype(BF16), w_ff_up_bf=w_ff_up.astype(BF16),
             w_ff_down_bf=w_ff_down.astype(BF16))

    tabs_p = _rope_tables(jnp.arange(l))
    tabs_s = _rope_tables(PAST_LEN + jnp.arange(t_new))

    tm = min(512, l)
    y_p, conv_p, kvs_p, s5_p = _run_trunk(x_prompt, (mod4, nb), tabs_p, None, None, None, W,
                                          1, tm, (n, 128, 4, True))
    nbk = min(nb, tm // t_new)
    caches_t = [_cache_rows_on_lanes(c) for c in (cache_kv_g0, cache_kv_g1, cache_kv_g2)]
    y_s, conv_s, kvs_s, s5_s = _run_trunk(x_sample, (mod4, 0), tabs_s, cache_conv, caches_t, state_s5, W,
                                          nbk, t_new, (nbk, t_new, 1, True))
    return (y_p, y_s, conv_p, kvs_p[0], kvs_p[1], kvs_p[2], s5_p,
            conv_s, kvs_s[0], kvs_s[1], kvs_s[2], s5_s)
```

```python
import functools
import math

import jax
import jax.numpy as jnp
from jax import lax
from jax.experimental import pallas as pl
from jax.experimental.pallas import tpu as pltpu

F32 = jnp.float32
BF16 = jnp.bfloat16

EPS = 1e-6
CONV_W = 31
HEAD_DIM = 64
HPG = 4
DIL_GROUPS = ((128, 1), (512, 4), (2048, 16))
N_DIL = len(DIL_GROUPS)
GRP_W = HPG * HEAD_DIM
ROT_DIM = HEAD_DIM // 4
ROPE_THETA = 500000.0
BAND_BLOCK = 128
S5_GROUP = 16
S5_STATE = 64
S5_JBLK = 4
PAST_LEN = 2048
LANES = 128
MXU_TILE = 256
SUBLANES = 8
CONV_HALO = 32
VMEM_LIMIT = 56 * 1024 * 1024


def _cparams(sem):
    return pltpu.CompilerParams(dimension_semantics=sem, vmem_limit_bytes=VMEM_LIMIT)


def _const_spec(shape):
    nd = len(shape)
    return pl.BlockSpec(shape, lambda *_: (0,) * nd)


def _weight_spec(shape, layer=None):
    nd = len(shape)
    if layer is None:
        return pl.BlockSpec(shape, lambda *_: (0,) * nd, pipeline_mode=pl.Buffered(1))
    return pl.BlockSpec((1,) + tuple(shape[1:]), lambda *_: (layer,) + (0,) * (nd - 1),
                        pipeline_mode=pl.Buffered(1))


def _row_spec(b0, b1, cols):
    return pl.BlockSpec((b0, b1, cols), lambda i, j: (i, j, 0))


def _mod_spec(mod, b0, d, k):
    _, layer, row0 = mod
    assert row0 % b0 == 0
    return pl.BlockSpec((1, b0, 1, d), lambda i, j: (layer, row0 // b0 + i, 0, k))


def _gain_spec(d, layer, which):
    return pl.BlockSpec((1, 1, 1, d), lambda *_: (layer, which, 0, 0))


def _norm_mod(x, g, sc, sh):
    y = x * lax.rsqrt(jnp.mean(x * x, axis=-1, keepdims=True) + EPS)
    return (y * g) * (1.0 + sc) + sh


def _silu(x):
    return x * jax.nn.sigmoid(x)


def _log2(n):
    assert n > 0 and n & (n - 1) == 0, n
    return n.bit_length() - 1


def _div_pow2(x, n):
    return x >> _log2(n)


def _mod_pow2(x, n):
    return x & ((1 << _log2(n)) - 1)


def _bdot(a, b):
    return jnp.dot(a.astype(BF16), b.astype(BF16), preferred_element_type=F32)


def _bdot_nt(a, b):
    return lax.dot_general(a.astype(BF16), b.astype(BF16), (((1,), (1,)), ((), ())),
                           preferred_element_type=F32)


def _ada_kernel(c_ref, w_ref, b_ref, o_ref):
    o_ref[0] = _bdot(_silu(c_ref[...]), w_ref[0]) + b_ref[0]


def _ada_call(c_all, w_ada, b_ada):
    depth, d, d6 = w_ada.shape
    m = c_all.shape[0]
    tn = 1536
    return pl.pallas_call(
        _ada_kernel,
        grid=(depth, d6 // tn),
        in_specs=[pl.BlockSpec((m, d), lambda l, j: (0, 0)),
                  pl.BlockSpec((1, d, tn), lambda l, j: (l, 0, j)),
                  pl.BlockSpec((1, 1, tn), lambda l, j: (l, 0, j))],
        out_specs=pl.BlockSpec((1, m, tn), lambda l, j: (l, 0, j)),
        out_shape=jax.ShapeDtypeStruct((depth, m, d6), F32),
        compiler_params=_cparams(("arbitrary", "arbitrary")),
        name="ada_mod",
    )(c_all, w_ada, b_ada.reshape(depth, 1, d6))


def _inproj_kernel(*refs, conv_ch, att_w, fuse_conv):
    x_ref, sh_ref, sc_ref, ng_ref, w_ref, ta_ref, tb_ref, tc_ref = refs[:8]
    if fuse_conv:
        cw_ref, cb_ref, lg_ref, lb_ref, yc_ref, tail_ref, q_ref, k_ref, v_ref, buf, shf, wtile = refs[8:]
    else:
        u_ref, q_ref, k_ref, v_ref = refs[8:]
    b0, b1, d = x_ref.shape
    rows = b0 * b1
    h = _norm_mod(x_ref[...], ng_ref[0, 0], sc_ref[0], sh_ref[0])
    hb = h.reshape(rows, d).astype(BF16)
    a_val = jnp.dot(hb, w_ref[0, :, 0:conv_ch], preferred_element_type=F32)
    a_gate = jnp.dot(hb, w_ref[0, :, conv_ch:2 * conv_ch], preferred_element_type=F32)
    u = a_val * jax.nn.sigmoid(a_gate)
    conv_chunks = []
    if fuse_conv:
        halo = CONV_HALO

        @pl.when(pl.program_id(1) == 0)
        def _():
            buf[0:halo, :] = jnp.zeros((halo, conv_ch), F32)

        buf[halo:halo + rows, :] = u
        conv_chunks = _conv_rows(buf, shf, wtile, cw_ref, cb_ref, lg_ref, lb_ref, yc_ref, rows, halo)
    else:
        u_ref[...] = u.reshape(b0, b1, conv_ch)

    def table(t_ref):
        return jnp.broadcast_to(t_ref[...], (b0, b1, LANES)).reshape(rows, LANES)

    ta, tb, tc = table(ta_ref), table(tb_ref), table(tc_ref)
    half = ROT_DIM // 2

    def rope(seg, scale):
        fwd = pltpu.roll(seg, LANES - half, 1)
        bwd = pltpu.roll(seg, half, 1)
        r = seg * ta + fwd * tb + bwd * tc
        return r * scale if scale != 1.0 else r

    blocks = [(o_ref, c, scale) for o_ref, scale in ((q_ref, HEAD_DIM ** -0.5), (k_ref, 1.0), (v_ref, None))
              for c in range(0, att_w, MXU_TILE)]
    c0 = 2 * conv_ch
    done = 0
    for bi, (o_ref, c, scale) in enumerate(blocks):
        wcol = c0 + bi * MXU_TILE
        z = jnp.dot(hb, w_ref[0, :, wcol:wcol + MXU_TILE], preferred_element_type=F32)
        if scale is not None:
            z = jnp.concatenate([rope(z[:, l:l + LANES], scale) for l in range(0, MXU_TILE, LANES)], axis=-1)
        o_ref[:, :, c:c + MXU_TILE] = z.reshape(b0, b1, MXU_TILE)
        upto = (bi + 1) * len(conv_chunks) // len(blocks)
        for chunk in conv_chunks[done:upto]:
            chunk()
        done = upto
    if fuse_conv:
        tail = buf[rows:rows + halo, :]
        tail_ref[0] = tail
        buf[0:halo, :] = tail


def _inproj_call(x, mod, gains, layer, w_in_bf, ei, tabs, b0, b1, conv_ch, att_w, conv=None):
    a, b, d = x.shape
    ta, tb, tc = tabs
    tspec = pl.BlockSpec((1, b1, LANES), lambda i, j: (0, j, 0))
    qkv_shapes = tuple(jax.ShapeDtypeStruct((a, b, att_w), F32) for _ in range(3))
    qkv_specs = tuple(_row_spec(b0, b1, att_w) for _ in range(3))
    in_specs = [_row_spec(b0, b1, d), _mod_spec(mod, b0, d, 0), _mod_spec(mod, b0, d, 1),
                _gain_spec(d, layer, 0), _weight_spec(w_in_bf.shape, ei), tspec, tspec, tspec]
    args = [x, mod[0], mod[0], gains, w_in_bf, ta, tb, tc]
    if conv is None:
        out_shapes = (jax.ShapeDtypeStruct((a, b, conv_ch), F32),) + qkv_shapes
        out_specs = (_row_spec(b0, b1, conv_ch),) + qkv_specs
        scratch, sem = [], ("parallel", "parallel")
    else:
        assert b0 == 1 and b1 % CONV_HALO == 0
        conv_w, conv_b, ln_g, ln_b = conv
        in_specs += [_const_spec((CONV_W, conv_ch))] + [_const_spec((1, conv_ch))] * 3
        args += [conv_w, conv_b.reshape(1, conv_ch), ln_g.reshape(1, conv_ch), ln_b.reshape(1, conv_ch)]
        out_shapes = (jax.ShapeDtypeStruct((a, b, conv_ch), F32),
                      jax.ShapeDtypeStruct((a, CONV_HALO, conv_ch), F32)) + qkv_shapes
        out_specs = (_row_spec(b0, b1, conv_ch),
                     pl.BlockSpec((1, CONV_HALO, conv_ch), lambda i, j: (i, 0, 0))) + qkv_specs
        scratch = [pltpu.VMEM((b1 + CONV_HALO, conv_ch), F32),
                   pltpu.VMEM((SUBLANES - 1, b1 + CONV_HALO - SUBLANES, conv_ch), F32),
                   pltpu.VMEM((CONV_W, SUBLANES, conv_ch), F32)]
        sem = ("parallel", "arbitrary")
    return pl.pallas_call(
        functools.partial(_inproj_kernel, conv_ch=conv_ch, att_w=att_w, fuse_conv=conv is not None),
        grid=(a // b0, b // b1),
        in_specs=in_specs,
        out_specs=out_specs,
        out_shape=out_shapes,
        scratch_shapes=scratch,
        compiler_params=_cparams(sem),
        name="in_proj_conv" if conv is not None else "in_proj",
    )(*args)


def _rope_tables(pos):
    half = ROT_DIM // 2
    inv = jnp.float32(ROPE_THETA) ** (-(2.0 / ROT_DIM) * jnp.arange(half, dtype=F32))
    ang = pos.astype(F32)[:, None] * inv[None, :]
    cos, sin = jnp.cos(ang), jnp.sin(ang)
    n = pos.shape[0]
    ones = jnp.ones((n, HEAD_DIM - ROT_DIM), F32)
    zeros = jnp.zeros((n, HEAD_DIM - ROT_DIM), F32)
    zh = jnp.zeros((n, half), F32)
    ta = jnp.concatenate([cos, cos, ones], axis=-1)
    tb = jnp.concatenate([-sin, zh, zeros], axis=-1)
    tc = jnp.concatenate([zh, sin, zeros], axis=-1)
    rep = LANES // HEAD_DIM
    return tuple(jnp.tile(t, (1, rep))[None] for t in (ta, tb, tc))


def _ln_silu(acc, g, b):
    mu = jnp.mean(acc, axis=-1, keepdims=True)
    xc = acc - mu
    y = xc * lax.rsqrt(jnp.mean(xc * xc, axis=-1, keepdims=True) + EPS)
    return _silu(y * g + b)


def _conv_rows(buf, shf, wtile, w_ref, cb_ref, g_ref, b_ref, o_ref, tt, halo):
    sub = shf.shape[0] + 1
    n8 = shf.shape[1]
    for s in range(1, sub):
        shf[s - 1] = buf[s:s + n8, :]
    off = halo - (CONV_W - 1)
    rb = 32
    c = buf.shape[1]
    for jt in range(CONV_W):
        wtile[jt] = jnp.broadcast_to(w_ref[jt:jt + 1, :], (sub, c))

    def chunk(r0):
        acc = None
        for jt in range(CONV_W):
            s = (off + jt) % sub
            lo = r0 + off + jt - s
            rows = buf[lo:lo + rb, :] if s == 0 else shf[s - 1, lo:lo + rb, :]
            term = rows.reshape(rb // sub, sub, c) * wtile[jt][None]
            acc = term if acc is None else acc + term
        acc = acc.reshape(rb, c)
        o_ref[0, r0:r0 + rb, :] = _ln_silu(acc + cb_ref[...], g_ref[...], b_ref[...])

    return [functools.partial(chunk, r0) for r0 in range(0, tt, rb)]


def _conv_sample_kernel(full_ref, w_ref, cb_ref, g_ref, b_ref, o_ref):
    t_out = o_ref.shape[0]
    for t in range(t_out):
        acc = w_ref[0:1, :] * full_ref[t]
        for jt in range(1, CONV_W):
            acc = acc + w_ref[jt:jt + 1, :] * full_ref[t + jt]
        o_ref[t] = _ln_silu(acc + cb_ref[...], g_ref[...], b_ref[...])


def _conv_sample_call(full_t, conv_w, conv_b, ln_g, ln_b):
    tf, nb, c = full_t.shape
    t_out = tf - (CONV_W - 1)
    nbk = 32
    return pl.pallas_call(
        _conv_sample_kernel,
        grid=(nb // nbk,),
        in_specs=[pl.BlockSpec((tf, nbk, c), lambda i: (0, i, 0)),
                  _const_spec((CONV_W, c)), _const_spec((1, c)), _const_spec((1, c)), _const_spec((1, c))],
        out_specs=pl.BlockSpec((t_out, nbk, c), lambda i: (0, i, 0)),
        out_shape=jax.ShapeDtypeStruct((t_out, nb, c), F32),
        compiler_params=_cparams(("parallel",)),
        name="conv_sample",
    )(full_t, conv_w, conv_b.reshape(1, c), ln_g.reshape(1, c), ln_b.reshape(1, c))


def _merge_groups(os_, ls_):
    m = functools.reduce(jnp.maximum, ls_)
    es = [jnp.exp(l - m) for l in ls_]
    num = functools.reduce(lambda a, b: a + b, [e * o for e, o in zip(es, os_)])
    return num / functools.reduce(lambda a, b: a + b, es)


def _attn_prompt_kernel(*refs, tb):
    in_refs, att_ref, scr = refs[:5 * N_DIL], refs[5 * N_DIL], refs[5 * N_DIL + 1:]
    kbufs, vbufs, (o_scr, l_scr) = scr[:N_DIL], scr[N_DIL:2 * N_DIL], scr[2 * N_DIL:]
    jb = pl.program_id(1)
    bb = BAND_BLOCK
    qi = lax.broadcasted_iota(jnp.int32, (bb, 2 * bb), 0) + bb
    ki = lax.broadcasted_iota(jnp.int32, (bb, 2 * bb), 1)
    dist = qi - ki
    band = (dist >= 0) & (dist <= bb)
    older = ki < bb
    for g, (win, dil) in enumerate(DIL_GROUPS):
        q_ref, kc_ref, kp_ref, vc_ref, vp_ref = in_refs[5 * g:5 * g + 5]
        kbuf, vbuf = kbufs[g], vbufs[g]
        pb = bb * dil
        kbuf[0:pb, :] = kp_ref[0]
        kbuf[pb:pb + tb, :] = kc_ref[0]
        vbuf[0:pb, :] = vp_ref[0]
        vbuf[pb:pb + tb, :] = vc_ref[0]

        def unit(u, carry, g=g, dil=dil, pb=pb, q_ref=q_ref, kbuf=kbuf, vbuf=vbuf):
            i = _div_pow2(u, dil)
            r = _mod_pow2(u, dil)
            if dil == 1:
                start = pl.multiple_of(i * pb, bb)
                qsel, ksel = pl.ds(start, bb), pl.ds(start, 2 * bb)
            else:
                start = i * pb + r
                qsel, ksel = pl.ds(start, bb, stride=dil), pl.ds(start, 2 * bb, stride=dil)
            qu = q_ref[0, qsel, :].astype(BF16)
            ku = kbuf[ksel, :].astype(BF16)
            vu = vbuf[ksel, :].astype(BF16)
            first = jnp.logical_and(jb == 0, i == 0)
            mask = jnp.logical_and(band, jnp.logical_not(jnp.logical_and(older, first)))
            o_parts, l_parts = [], []
            for h in range(LANES // HEAD_DIM):
                cs = slice(h * HEAD_DIM, (h + 1) * HEAD_DIM)
                s = lax.dot_general(qu[:, cs], ku[:, cs], (((1,), (1,)), ((), ())), preferred_element_type=F32)
                s = jnp.where(mask, s, -jnp.inf)
                m = jnp.max(s, axis=-1, keepdims=True)
                p = jnp.exp(s - m)
                lsum = jnp.sum(p, axis=-1, keepdims=True)
                o_parts.append(jnp.dot(p.astype(BF16), vu[:, cs], preferred_element_type=F32) / lsum)
                l_parts.append(jnp.broadcast_to(m + jnp.log(lsum), (bb, HEAD_DIM)))
            o_scr[g, qsel, :] = jnp.concatenate(o_parts, axis=-1)
            l_scr[g, qsel, :] = jnp.concatenate(l_parts, axis=-1)
            return carry

        lax.fori_loop(0, tb // bb, unit, 0, unroll=4)

    mr = 256
    for r0 in range(0, tb, mr):
        att_ref[0, r0:r0 + mr, :] = _merge_groups([o_scr[g, r0:r0 + mr, :] for g in range(N_DIL)],
                                                  [l_scr[g, r0:r0 + mr, :] for g in range(N_DIL)])


def _attn_prompt_call(q, k, v):
    n, l, att_w = q.shape
    bb = BAND_BLOCK
    tb = bb * max(d for _, d in DIL_GROUPS)
    assert l % tb == 0 and all(w // d == bb for w, d in DIL_GROUPS) and att_w == N_DIL * GRP_W
    hp = GRP_W // LANES
    in_specs, args, kv_scr = [], [], []
    for g, (_, dil) in enumerate(DIL_GROUPS):
        pb = bb * dil
        ratio = tb // pb
        cur = pl.BlockSpec((1, tb, LANES), lambda i, j, h, g=g: (i, j, g * hp + h))
        prev = pl.BlockSpec((1, pb, LANES),
                            lambda i, j, h, g=g, ratio=ratio: (i, jnp.maximum(j * ratio - 1, 0), g * hp + h))
        in_specs += [cur, cur, prev, cur, prev]
        args += [q, k, k, v, v]
        kv_scr.append(pltpu.VMEM((pb + tb, LANES), F32))
    return pl.pallas_call(
        functools.partial(_attn_prompt_kernel, tb=tb),
        grid=(n, l // tb, hp),
        in_specs=in_specs,
        out_specs=pl.BlockSpec((1, tb, LANES), lambda i, j, h: (i, j, h)),
        out_shape=jax.ShapeDtypeStruct((n, l, GRP_W), F32),
        scratch_shapes=kv_scr + kv_scr + [pltpu.VMEM((N_DIL, tb, LANES), F32)] * 2,
        compiler_params=_cparams(("parallel", "arbitrary", "arbitrary")),
        name="attn_prompt",
    )(*args)


def _attn_sample_kernel(q_ref, kn_ref, vn_ref, c0_ref, c1_ref, c2_ref, att_ref):
    t_new = q_ref.shape[1]
    rows = HPG * t_new
    ri = lax.broadcasted_iota(jnp.int32, (rows, GRP_W), 0)
    ci = lax.broadcasted_iota(jnp.int32, (rows, GRP_W), 1)
    head_mask = _div_pow2(ci, HEAD_DIM) == _div_pow2(ri, t_new)
    c_refs = (c0_ref, c1_ref, c2_ref)
    valid = []
    for (win, dil), c_ref in zip(DIL_GROUPS, c_refs):
        wb = c_ref.shape[-1]
        tq = _mod_pow2(lax.broadcasted_iota(jnp.int32, (rows, wb), 0), t_new)
        ki = lax.broadcasted_iota(jnp.int32, (rows, wb), 1)
        dist = wb + tq - ki
        tqn = _mod_pow2(lax.broadcasted_iota(jnp.int32, (rows, t_new), 0), t_new)
        jn = lax.broadcasted_iota(jnp.int32, (rows, t_new), 1)
        dn = tqn - jn
        valid.append(((dist >= 0) & (dist <= win) & (_mod_pow2(dist, dil) == 0),
                      (dn >= 0) & (dn <= win) & (_mod_pow2(dn, dil) == 0)))
    nns = range(q_ref.shape[0])
    os_, ls_ = [[] for _ in nns], [[] for _ in nns]
    for gi, c_ref in enumerate(c_refs):
        cs = slice(gi * GRP_W, (gi + 1) * GRP_W)
        valid_c, valid_n = valid[gi]
        qbs = [jnp.where(head_mask, jnp.concatenate([q_ref[nn, :, cs]] * HPG, axis=0), 0.0) for nn in nns]
        s_cs = [jnp.where(valid_c, _bdot(qbs[nn], c_ref[0, nn, 0]), -jnp.inf) for nn in nns]
        s_ns = [jnp.where(valid_n, _bdot_nt(qbs[nn], kn_ref[nn, :, cs]), -jnp.inf) for nn in nns]
        ms = [jnp.maximum(jnp.max(s_cs[nn], axis=-1, keepdims=True), jnp.max(s_ns[nn], axis=-1, keepdims=True))
              for nn in nns]
        p_cs = [jnp.exp(s_cs[nn] - ms[nn]) for nn in nns]
        p_ns = [jnp.exp(s_ns[nn] - ms[nn]) for nn in nns]
        lsums = [jnp.sum(p_cs[nn], axis=-1, keepdims=True) + jnp.sum(p_ns[nn], axis=-1, keepdims=True)
                 for nn in nns]
        o_fulls = [(_bdot_nt(p_cs[nn], c_ref[0, nn, 1]) + _bdot(p_ns[nn], vn_ref[nn, :, cs])) / lsums[nn]
                   for nn in nns]
        for nn in nns:
            o_full = jnp.where(head_mask, o_fulls[nn], 0.0)
            lse_full = jnp.where(head_mask, jnp.broadcast_to(ms[nn] + jnp.log(lsums[nn]), (rows, GRP_W)), 0.0)
            o = o_full[0:t_new]
            lse = lse_full[0:t_new]
            for h in range(1, HPG):
                o = o + o_full[h * t_new:(h + 1) * t_new]
                lse = lse + lse_full[h * t_new:(h + 1) * t_new]
            os_[nn].append(o)
            ls_[nn].append(lse)
    for nn in nns:
        att_ref[nn] = _merge_groups(os_[nn], ls_[nn])


def _cache_rows_on_lanes(cache):
    ly, nb, wb = cache.shape[:3]
    return jnp.transpose(cache, (0, 1, 3, 4, 5, 2)).reshape(ly, nb, 2, GRP_W, wb)


def _attn_sample_call(q, k, v, caches_t, ei):
    nb, t_new, att_w = q.shape
    assert all(c.shape[-1] == w for c, (w, _) in zip(caches_t, DIL_GROUPS))
    n_blk = 2 if nb % 2 == 0 else 1
    qspec = pl.BlockSpec((n_blk, t_new, att_w), lambda i: (i, 0, 0))
    cspecs = [pl.BlockSpec((1, n_blk) + c.shape[2:], lambda i: (ei, i, 0, 0, 0)) for c in caches_t]
    return pl.pallas_call(
        _attn_sample_kernel,
        grid=(nb // n_blk,),
        in_specs=[qspec, qspec, qspec] + cspecs,
        out_specs=pl.BlockSpec((n_blk, t_new, GRP_W), lambda i: (i, 0, 0)),
        out_shape=jax.ShapeDtypeStruct((nb, t_new, GRP_W), F32),
        compiler_params=_cparams(("parallel",)),
        name="attn_sample",
    )(q, k, v, *caches_t)


def _ffn_kernel(*refs, fchunks, final, mix):
    x_ref, sh_ref, sc_ref, g2_ref, ng_ref, wg_ref, wu_ref, wd_ref, fg_ref = refs[:9]
    o_ref = refs[-1]
    b0, b1, d = x_ref.shape
    rows = b0 * b1
    x = x_ref[...]
    if mix:
        g1_ref, yc_ref, att_ref, wa_ref, wb_ref = refs[9:14]
        mixed = _bdot(yc_ref[...].reshape(rows, -1), wa_ref[0]) + _bdot(att_ref[...].reshape(rows, -1), wb_ref[0])
        x = x + g1_ref[0] * mixed.reshape(b0, b1, d)
    hb = _norm_mod(x, ng_ref[0, 0], sc_ref[0], sh_ref[0]).reshape(rows, d).astype(BF16)
    dff = wg_ref.shape[2]
    acc = None
    for f0, fchunk in fchunks:
        g = jnp.dot(hb, wg_ref[0, :, f0:f0 + fchunk], preferred_element_type=F32)
        u = jnp.dot(hb, wu_ref[0, :, f0:f0 + fchunk], preferred_element_type=F32)
        part = jnp.dot((_silu(g) * u).astype(BF16), wd_ref[0, f0:f0 + fchunk, :], preferred_element_type=F32)
        acc = part if acc is None else acc + part
    xn = x + g2_ref[0] * acc.reshape(b0, b1, d)
    if final:
        xn = xn * lax.rsqrt(jnp.mean(xn * xn, axis=-1, keepdims=True) + EPS) * fg_ref[...]
    o_ref[...] = xn


def _ffn_call(x, mod, gains, layer, wg_bf, wu_bf, wd_bf, final_g, b0, b1, final, mix=None):
    a, b, d = x.shape
    dff = wg_bf.shape[2]
    in_specs = [_row_spec(b0, b1, d), _mod_spec(mod, b0, d, 3), _mod_spec(mod, b0, d, 4), _mod_spec(mod, b0, d, 5),
                _gain_spec(d, layer, 1), _weight_spec(wg_bf.shape, layer), _weight_spec(wu_bf.shape, layer),
                _weight_spec(wd_bf.shape, layer), _const_spec((1, d))]
    args = [x, mod[0], mod[0], mod[0], gains, wg_bf, wu_bf, wd_bf, final_g.reshape(1, d)]
    if mix is not None:
        yc, att, wa_bf, wb_bf, ei = mix
        in_specs += [_mod_spec(mod, b0, d, 2), _row_spec(b0, b1, yc.shape[-1]), _row_spec(b0, b1, att.shape[-1]),
                     _weight_spec(wa_bf.shape, ei), _weight_spec(wb_bf.shape, ei)]
        args += [mod[0], yc, att, wa_bf, wb_bf]
    tiles = -(-dff // MXU_TILE)
    f_split = min(dff, -(-tiles // 2) * MXU_TILE)
    fchunks = tuple((f0, f1 - f0) for f0, f1 in ((0, f_split), (f_split, dff)) if f1 > f0)
    return pl.pallas_call(
        functools.partial(_ffn_kernel, fchunks=fchunks, final=final, mix=mix is not None),
        grid=(a // b0, b // b1),
        in_specs=in_specs,
        out_specs=_row_spec(b0, b1, d),
        out_shape=jax.ShapeDtypeStruct((a, b, d), F32),
        compiler_params=_cparams(("parallel", "parallel")),
        name=("mix_" if mix is not None else "") + ("ffn_final" if final else "ffn"),
    )(*args)


def _s5_param_kernel(lr_ref, li_ref, ldt_ref, br_ref, bi_ref, abr_ref, abi_ref, bbr_ref, bbi_ref,
                     sbr_ref, sbi_ref):
    lr, li = lr_ref[0], li_ref[0]
    dt = jnp.exp(ldt_ref[0])
    mag = jnp.exp(lr * dt)
    ph = li * dt
    abr = mag * jnp.cos(ph)
    abi = mag * jnp.sin(ph)
    den = lr * lr + li * li
    nr = abr - 1.0
    fr = (nr * lr + abi * li) / den
    fi = (abi * lr - nr * li) / den
    abr_ref[0] = abr
    abi_ref[0] = abi
    for h in range(br_ref.shape[1]):
        br, bi = br_ref[0, h], bi_ref[0, h]
        bbr = fr * br - fi * bi
        bbi = fr * bi + fi * br
        bbr_ref[0, h] = bbr
        bbi_ref[0, h] = bbi
        sbr_ref[0, h] = abr * bbr - abi * bbi
        sbi_ref[0, h] = abr * bbi + abi * bbr


def _s5_param_call(lam_re, lam_im, log_dt, b_re, b_im):
    n_odd, g, p = lam_re.shape
    hh = b_re.shape[-1]
    bt_re = jnp.transpose(b_re, (0, 3, 1, 2))
    bt_im = jnp.transpose(b_im, (0, 3, 1, 2))
    gp = pl.BlockSpec((1, g, p), lambda l: (l, 0, 0))
    bs = pl.BlockSpec((1, hh, g, p), lambda l: (l, 0, 0, 0))
    return pl.pallas_call(
        _s5_param_kernel,
        grid=(n_odd,),
        in_specs=[gp, gp, pl.BlockSpec((1, g, 1), lambda l: (l, 0, 0)), bs, bs],
        out_specs=(gp, gp, bs, bs, bs, bs),
        out_shape=(jax.ShapeDtypeStruct((n_odd, g, p), F32),) * 2
        + (jax.ShapeDtypeStruct((n_odd, hh, g, p), F32),) * 4,
        compiler_params=_cparams(("arbitrary",)),
        name="s5_params",
    )(lam_re, lam_im, log_dt.reshape(n_odd, g, 1), bt_re, bt_im)


def _s5_block_weights(bbr, bbi, sbr, sbi, c_re, c_im):
    hh, g, p = bbr.shape
    gl = g // S5_JBLK
    eye = jnp.eye(gl, dtype=F32)

    def bblk(t):
        t = t.reshape(hh, S5_JBLK, gl, p).transpose(1, 2, 0, 3)
        t = t[:, :, :, None, :] * eye[None, :, None, :, None]
        return t.reshape(S5_JBLK, gl * hh, gl * p)

    def cblk(t):
        t = t.reshape(S5_JBLK, gl, hh, p).transpose(0, 1, 3, 2)
        t = t[:, :, :, None, :] * eye[None, :, None, :, None]
        return t.reshape(S5_JBLK, gl * p, gl * hh)

    bb = jnp.concatenate([bblk(bbr), bblk(bbi)], axis=-1).astype(BF16)
    sb = jnp.concatenate([bblk(sbr), bblk(sbi)], axis=-1).astype(BF16)
    cc = jnp.concatenate([cblk(c_re), cblk(-c_im)], axis=1).astype(BF16)
    return bb, jnp.concatenate([bb, sb], axis=1), cc


def _gelu_tanh(x):
    return 0.5 * x * (1.0 + jnp.tanh(math.sqrt(2.0 / math.pi) * (x + 0.044715 * (x * x * x))))


def _s5_kernel(x_ref, sh_ref, sc_ref, g1_ref, ng_ref, s0r_ref, s0i_ref, abr_ref, abi_ref, perm_ref, bb_ref, cc_ref,
               dsk_ref, wglu_ref, bglu_ref, xo_ref, sro_ref, sio_ref,
               st_r, st_i, bu_scr, h_scr, yp_scr, y_scr, *, lw, unroll):
    c = pl.program_id(1)
    nbk, tc, d = x_ref.shape
    rows = nbk * tc
    jw = d // S5_JBLK
    sw = bb_ref.shape[2] // 2
    w = lw * LANES
    sub = 8

    two_step = nbk % sub != 0
    if two_step:
        assert 2 * nbk == sub and tc % 2 == 0 and bb_ref.shape[1] == 2 * jw

    @pl.when(c == 0)
    def _():
        if two_step:
            st_r[0:nbk, :] = jnp.zeros((nbk, st_r.shape[1]), F32)
            st_i[0:nbk, :] = jnp.zeros((nbk, st_i.shape[1]), F32)
            st_r[nbk:sub, :] = s0r_ref[...]
            st_i[nbk:sub, :] = s0i_ref[...]
        else:
            st_r[...] = s0r_ref[...]
            st_i[...] = s0i_ref[...]

    x = x_ref[...]
    u2 = _norm_mod(x, ng_ref[0, 0], sc_ref[0], sh_ref[0]).reshape(rows, d)
    ub = u2.astype(BF16)
    up = jnp.dot(perm_ref[0], ub, preferred_element_type=F32).astype(BF16)
    if two_step:
        ups = jnp.dot(perm_ref[1], ub, preferred_element_type=F32).astype(BF16)

    def project_in(j):
        lhs = up[:, j * jw:(j + 1) * jw]
        if two_step:
            lhs = jnp.concatenate([lhs, ups[:, j * jw:(j + 1) * jw]], axis=1)
        bu_scr[j % 2] = jnp.dot(lhs, bb_ref[j], preferred_element_type=F32)

    project_in(0)
    for j in range(S5_JBLK):
        if j + 1 < S5_JBLK:
            project_in(j + 1)
        bu, hs = bu_scr.at[j % 2], h_scr.at[j % 2]
        for l0 in range(0, sw, w):
            c_re, c_im = slice(l0, l0 + w), slice(sw + l0, sw + l0 + w)
            c_st = slice(j * sw + l0, j * sw + l0 + w)
            if nbk % sub == 0:
                ar = jnp.broadcast_to(abr_ref[:, c_st], (nbk, w))
                ai = jnp.broadcast_to(abi_ref[:, c_st], (nbk, w))

                def step(t, carry, c_re=c_re, c_im=c_im, ar=ar, ai=ai):
                    hr, hi = carry
                    rs = pl.ds(pl.multiple_of(t * nbk, nbk), nbk)
                    nhr = ar * hr - ai * hi + bu[rs, c_re]
                    nhi = ar * hi + ai * hr + bu[rs, c_im]
                    hs[rs, c_re] = nhr
                    hs[rs, c_im] = nhi
                    return nhr, nhi

                hr, hi = lax.fori_loop(0, tc, step, (st_r[:, c_st], st_i[:, c_st]), unroll=unroll)
                st_r[:, c_st] = hr
                st_i[:, c_st] = hi
            else:
                ar = jnp.broadcast_to(abr_ref[:, c_st], (sub, w))
                ai = jnp.broadcast_to(abi_ref[:, c_st], (sub, w))
                a2r, a2i = ar * ar - ai * ai, 2.0 * (ar * ai)
                lower = lax.broadcasted_iota(jnp.int32, (sub, w), 0) < nbk
                pr, pi = st_r[:, c_st], st_i[:, c_st]
                xr, xi = bu[0:sub, c_re], bu[0:sub, c_im]
                qr, qi = pltpu.roll(pr, nbk, 0), pltpu.roll(pi, nbk, 0)
                hr = jnp.where(lower, ar * qr - ai * qi, a2r * pr - a2i * pi) + xr
                hi = jnp.where(lower, ar * qi + ai * qr, a2r * pi + a2i * pr) + xi
                hs[0:sub, c_re] = hr
                hs[0:sub, c_im] = hi

                def step2(i, carry, c_re=c_re, c_im=c_im, a2r=a2r, a2i=a2i):
                    sr, si = carry
                    rs = pl.ds(pl.multiple_of(i * sub, sub), sub)
                    nr = a2r * sr - a2i * si + bu[rs, c_re]
                    ni = a2r * si + a2i * sr + bu[rs, c_im]
                    hs[rs, c_re] = nr
                    hs[rs, c_im] = ni
                    return nr, ni

                sr, si = lax.fori_loop(1, tc // 2, step2, (hr, hi), unroll=unroll)
                st_r[:, c_st] = sr
                st_i[:, c_st] = si
        yj = jnp.dot(hs[...].astype(BF16), cc_ref[j], preferred_element_type=F32)
        for k in range(jw // LANES):
            yp_scr[j * (jw // LANES) + k] = yj[:, k * LANES:(k + 1) * LANES]

    nlc = d // LANES
    for k in range(nlc):
        if nbk <= tc:
            for n in range(nbk):
                y_scr[k, n * tc:(n + 1) * tc, :] = yp_scr[k, pl.ds(n, tc, stride=nbk), :]
        else:
            for t in range(tc):
                y_scr[k, pl.ds(t, nbk, stride=tc), :] = yp_scr[k, t * nbk:(t + 1) * nbk, :]
    y = jnp.concatenate([y_scr[k] for k in range(nlc)], axis=-1) + dsk_ref[...] * u2
    g = _bdot(_gelu_tanh(y), wglu_ref[0]) + bglu_ref[...]
    out = g[:, 0:d] * jax.nn.sigmoid(g[:, d:2 * d])
    xo_ref[...] = x + g1_ref[0] * out.reshape(nbk, tc, d)

    @pl.when(c == pl.num_programs(1) - 1)
    def _():
        if two_step:
            sro_ref[...] = st_r[nbk:sub, :]
            sio_ref[...] = st_i[nbk:sub, :]
        else:
            sro_ref[...] = st_r[...]
            sio_ref[...] = st_i[...]


def _s5_call(x, mod, gains, layer, s0r, s0i, abr, abi, bb, cc, d_skip, wglu_bf, oi, b_glu, nbk, tc, lw, unroll):
    a, b, d = x.shape
    ns = s0r.shape[1]
    rows = nbk * tc
    dst = jnp.arange(rows)
    src = (dst % nbk) * tc + dst // nbk
    col = jnp.arange(rows)[None, :]
    perm = jnp.stack([src[:, None] == col,
                      jnp.logical_and(src[:, None] - 1 == col, (dst // nbk > 0)[:, None])]).astype(BF16)
    st_rows = nbk if nbk % 8 == 0 else 2 * nbk
    sspec = pl.BlockSpec((nbk, ns), lambda i, j: (i, 0))
    return pl.pallas_call(
        functools.partial(_s5_kernel, lw=lw, unroll=unroll),
        grid=(a // nbk, b // tc),
        in_specs=[_row_spec(nbk, tc, d), _mod_spec(mod, nbk, d, 0), _mod_spec(mod, nbk, d, 1),
                  _mod_spec(mod, nbk, d, 2), _gain_spec(d, layer, 0), sspec, sspec,
                  _const_spec((1, ns)), _const_spec((1, ns)),
                  _weight_spec(perm.shape), _weight_spec(bb.shape), _weight_spec(cc.shape), _const_spec((1, d)),
                  _weight_spec(wglu_bf.shape, oi), _const_spec((1, 2 * d))],
        out_specs=(_row_spec(nbk, tc, d), sspec, sspec),
        out_shape=(jax.ShapeDtypeStruct((a, b, d), F32),
                   jax.ShapeDtypeStruct((a, ns), F32), jax.ShapeDtypeStruct((a, ns), F32)),
        scratch_shapes=[pltpu.VMEM((st_rows, ns), F32), pltpu.VMEM((st_rows, ns), F32),
                        pltpu.VMEM((2, rows, bb.shape[2]), F32), pltpu.VMEM((2, rows, bb.shape[2]), F32),
                        pltpu.VMEM((d // LANES, rows, LANES), F32), pltpu.VMEM((d // LANES, rows, LANES), F32)],
        compiler_params=_cparams(("parallel", "arbitrary")),
        name="s5_mixer",
    )(x, mod[0], mod[0], mod[0], gains, s0r, s0i, abr.reshape(1, ns), abi.reshape(1, ns), perm, bb, cc,
      d_skip.reshape(1, d), wglu_bf, b_glu.reshape(1, 2 * d))


def _run_trunk(x, mods, tabs, conv_st, kv_st, s5_st, W, b0, b1, s5_tiles):
    a, b, d = x.shape
    depth = mods[0].shape[0]
    conv_ch = W['conv_w'].shape[-1]
    att_w = N_DIL * GRP_W
    new_conv, new_s5 = [], []
    new_kv = [[] for _ in range(N_DIL)]
    ei = oi = 0
    for layer in range(depth):
        mod = (mods[0], layer, mods[1])
        mix = None
        if layer % 2 == 0:
            cw = (W['conv_w'][ei], W['conv_b'][ei], W['conv_ln_g'][ei], W['conv_ln_b'][ei])
            if conv_st is None:
                yc, u_tail, q, k, v = _inproj_call(x, mod, W['gains'], layer, W['w_in_bf'], ei, tabs, b0, b1,
                                                   conv_ch, att_w, conv=cw)
                new_conv.append(u_tail[:, CONV_HALO - (CONV_W - 1):])
                att = _attn_prompt_call(q, k, v)
                for gi, (win, _) in enumerate(DIL_GROUPS):
                    keep = min(win, b)
                    kg = k[:, b - keep:, gi * GRP_W:(gi + 1) * GRP_W].reshape(a, keep, HPG, HEAD_DIM)
                    vg = v[:, b - keep:, gi * GRP_W:(gi + 1) * GRP_W].reshape(a, keep, HPG, HEAD_DIM)
                    new_kv[gi].append(jnp.stack([kg, vg], axis=2))
            else:
                u, q, k, v = _inproj_call(x, mod, W['gains'], layer, W['w_in_bf'], ei, tabs, b0, b1, conv_ch, att_w)
                full = jnp.concatenate([conv_st[ei], u], axis=1)
                yc = jnp.transpose(_conv_sample_call(jnp.transpose(full, (1, 0, 2)), *cw), (1, 0, 2))
                new_conv.append(full[:, -(CONV_W - 1):])
                att = _attn_sample_call(q, k, v, kv_st, ei)
                for gi in range(N_DIL):
                    kg = k[:, :, gi * GRP_W:(gi + 1) * GRP_W].reshape(a, b, HPG, HEAD_DIM)
                    vg = v[:, :, gi * GRP_W:(gi + 1) * GRP_W].reshape(a, b, HPG, HEAD_DIM)
                    new_kv[gi].append(jnp.stack([kg, vg], axis=2))
            mix = (yc, att, W['w_o_a_bf'], W['w_o_b_bf'], ei)
            ei += 1
        else:
            ns = W['abr'].shape[1] * W['abr'].shape[2]
            if s5_st is None:
                s0r = jnp.zeros((a, ns), F32)
                s0i = jnp.zeros((a, ns), F32)
            else:
                s0r = s5_st[oi][..., 0].reshape(a, ns)
                s0i = s5_st[oi][..., 1].reshape(a, ns)
            nbk, tc, lw, unroll = s5_tiles
            bb = W['bb'][oi] if nbk % 8 == 0 else W['bb2'][oi]
            x, sr, si = _s5_call(x, mod, W['gains'], layer, s0r, s0i, W['abr'][oi], W['abi'][oi],
                                 bb, W['cc'][oi], W['s5_d'][oi], W['s5_w_glu_bf'], oi,
                                 W['s5_b_glu'][oi], nbk, tc, lw, unroll)
            g, p = W['abr'].shape[1:]
            new_s5.append(jnp.stack([sr.reshape(a, g, p), si.reshape(a, g, p)], axis=-1))
            oi += 1
        x = _ffn_call(x, mod, W['gains'], layer, W['w_ff_gate_bf'], W['w_ff_up_bf'],
                      W['w_ff_down_bf'], W['final_g'], b0, b1, final=(layer == depth - 1), mix=mix)
    return x, jnp.stack(new_conv), [jnp.stack(kv) for kv in new_kv], jnp.stack(new_s5)


def kernel(x_prompt, x_sample, cache_conv, cache_kv_g0, cache_kv_g1, cache_kv_g2, state_s5, c_prompt, c_sample, norm_g, final_g, w_ada, b_ada, w_in, conv_w, conv_b, conv_ln_g, conv_ln_b, w_o, s5_lam_re, s5_lam_im, s5_log_dt, s5_b_re, s5_b_im, s5_c_re, s5_c_im, s5_d, s5_w_glu, s5_b_glu, w_ff_gate, w_ff_up, w_ff_down):
    n, l, d = x_prompt.shape
    nb, t_new, _ = x_sample.shape
    depth = w_ada.shape[0]
    conv_ch = conv_w.shape[-1]

    m_all = n + nb
    m_pad = -(-m_all // 8) * 8
    c_all = jnp.concatenate([c_sample, c_prompt, jnp.zeros((m_pad - m_all, d), F32)], axis=0)
    mod4 = _ada_call(c_all, w_ada, b_ada).reshape(depth, m_pad, 1, 6 * d)

    abr, abi, bbr, bbi, sbr, sbi = _s5_param_call(s5_lam_re, s5_lam_im, s5_log_dt, s5_b_re, s5_b_im)
    blocks = [_s5_block_weights(bbr[i], bbi[i], sbr[i], sbi[i], s5_c_re[i], s5_c_im[i])
              for i in range(abr.shape[0])]

    W = dict(gains=norm_g.reshape(depth, 2, 1, d), final_g=final_g, conv_w=conv_w, conv_b=conv_b, conv_ln_g=conv_ln_g,
             conv_ln_b=conv_ln_b, s5_d=s5_d, s5_b_glu=s5_b_glu, abr=abr, abi=abi,
             bb=[b[0] for b in blocks], bb2=[b[1] for b in blocks], cc=[b[2] for b in blocks],
             w_in_bf=w_in.astype(BF16), w_o_a_bf=w_o[:, :conv_ch].astype(BF16),
             w_o_b_bf=w_o[:, conv_ch:].astype(BF16), s5_w_glu_bf=s5_w_glu.astype(BF16),
             w_ff_gate_bf=w_ff_gate.astype(BF16), w_ff_up_bf=w_ff_up.astype(BF16),
             w_ff_down_bf=w_ff_down.astype(BF16))

    tabs_p = _rope_tables(jnp.arange(l))
    tabs_s = _rope_tables(PAST_LEN + jnp.arange(t_new))

    tm = min(512, l)
    y_p, conv_p, kvs_p, s5_p = _run_trunk(x_prompt, (mod4, nb), tabs_p, None, None, None, W,
                                          1, tm, (n, 128, 4, True))
    nbk = min(nb, tm // t_new)
    caches_t = [_cache_rows_on_lanes(c) for c in (cache_kv_g0, cache_kv_g1, cache_kv_g2)]
    y_s, conv_s, kvs_s, s5_s = _run_trunk(x_sample, (mod4, 0), tabs_s, cache_conv, caches_t, state_s5, W,
                                          nbk, t_new, (nbk, t_new, 1, True))
    return (y_p, y_s, conv_p, kvs_p[0], kvs_p[1], kvs_p[2], s5_p,
            conv_s, kvs_s[0], kvs_s[1], kvs_s[2], s5_s)
```

```python
import functools
import math

import jax
import jax.numpy as jnp
from jax import lax
from jax.experimental import pallas as pl
from jax.experimental.pallas import tpu as pltpu

F32 = jnp.float32
BF16 = jnp.bfloat16

EPS = 1e-6
CONV_W = 31
HEAD_DIM = 64
HPG = 4
DIL_GROUPS = ((128, 1), (512, 4), (2048, 16))
N_DIL = len(DIL_GROUPS)
GRP_W = HPG * HEAD_DIM
ROT_DIM = HEAD_DIM // 4
ROPE_THETA = 500000.0
BAND_BLOCK = 128
S5_GROUP = 16
S5_STATE = 64
S5_JBLK = 4
PAST_LEN = 2048
LANES = 128
MXU_TILE = 256
SUBLANES = 8
ATTN_UNITS = 4
CONV_HALO = 32
VMEM_LIMIT = 56 * 1024 * 1024


def _cparams(sem):
    return pltpu.CompilerParams(dimension_semantics=sem, vmem_limit_bytes=VMEM_LIMIT)


def _const_spec(shape):
    nd = len(shape)
    return pl.BlockSpec(shape, lambda *_: (0,) * nd)


def _weight_spec(shape, layer=None):
    nd = len(shape)
    if layer is None:
        return pl.BlockSpec(shape, lambda *_: (0,) * nd, pipeline_mode=pl.Buffered(1))
    return pl.BlockSpec((1,) + tuple(shape[1:]), lambda *_: (layer,) + (0,) * (nd - 1),
                        pipeline_mode=pl.Buffered(1))


def _row_spec(b0, b1, cols):
    return pl.BlockSpec((b0, b1, cols), lambda i, j: (i, j, 0))


def _mod_spec(mod, b0, d, k):
    _, layer, row0 = mod
    assert row0 % b0 == 0
    return pl.BlockSpec((1, b0, 1, d), lambda i, j: (layer, row0 // b0 + i, 0, k))


def _gain_spec(d, layer, which):
    return pl.BlockSpec((1, 1, 1, d), lambda *_: (layer, which, 0, 0))


def _norm_mod(x, g, sc, sh):
    y = x * lax.rsqrt(jnp.mean(x * x, axis=-1, keepdims=True) + EPS)
    return (y * g) * (1.0 + sc) + sh


def _silu(x):
    return x * jax.nn.sigmoid(x)


def _log2(n):
    assert n > 0 and n & (n - 1) == 0, n
    return n.bit_length() - 1


def _div_pow2(x, n):
    return x >> _log2(n)


def _mod_pow2(x, n):
    return x & ((1 << _log2(n)) - 1)


def _bdot(a, b):
    return jnp.dot(a.astype(BF16), b.astype(BF16), preferred_element_type=F32)


def _bdot_nt(a, b):
    return lax.dot_general(a.astype(BF16), b.astype(BF16), (((1,), (1,)), ((), ())),
                           preferred_element_type=F32)


def _ada_kernel(c_ref, w_ref, b_ref, o_ref):
    o_ref[0] = _bdot(_silu(c_ref[...]), w_ref[0]) + b_ref[0]


def _ada_call(c_all, w_ada, b_ada):
    depth, d, d6 = w_ada.shape
    m = c_all.shape[0]
    tn = 1536
    return pl.pallas_call(
        _ada_kernel,
        grid=(depth, d6 // tn),
        in_specs=[pl.BlockSpec((m, d), lambda l, j: (0, 0)),
                  pl.BlockSpec((1, d, tn), lambda l, j: (l, 0, j)),
                  pl.BlockSpec((1, 1, tn), lambda l, j: (l, 0, j))],
        out_specs=pl.BlockSpec((1, m, tn), lambda l, j: (l, 0, j)),
        out_shape=jax.ShapeDtypeStruct((depth, m, d6), F32),
        compiler_params=_cparams(("arbitrary", "arbitrary")),
        name="ada_mod",
    )(c_all, w_ada, b_ada.reshape(depth, 1, d6))


def _inproj_kernel(*refs, conv_ch, att_w, fuse_conv):
    x_ref, sh_ref, sc_ref, ng_ref, w_ref, ta_ref, tb_ref, tc_ref = refs[:8]
    if fuse_conv:
        cw_ref, cb_ref, lg_ref, lb_ref, yc_ref, tail_ref, q_ref, k_ref, v_ref, buf, shf, wtile = refs[8:]
    else:
        u_ref, q_ref, k_ref, v_ref = refs[8:]
    b0, b1, d = x_ref.shape
    rows = b0 * b1
    h = _norm_mod(x_ref[...], ng_ref[0, 0], sc_ref[0], sh_ref[0])
    hb = h.reshape(rows, d).astype(BF16)
    a_val = jnp.dot(hb, w_ref[0, :, 0:conv_ch], preferred_element_type=F32)
    a_gate = jnp.dot(hb, w_ref[0, :, conv_ch:2 * conv_ch], preferred_element_type=F32)
    u = a_val * jax.nn.sigmoid(a_gate)
    conv_chunks = []
    if fuse_conv:
        halo = CONV_HALO

        @pl.when(pl.program_id(1) == 0)
        def _():
            buf[0:halo, :] = jnp.zeros((halo, conv_ch), F32)

        buf[halo:halo + rows, :] = u
        conv_chunks = _conv_rows(buf, shf, wtile, cw_ref, cb_ref, lg_ref, lb_ref, yc_ref, rows, halo)
    else:
        u_ref[...] = u.reshape(b0, b1, conv_ch)

    def table(t_ref):
        return jnp.broadcast_to(t_ref[...], (b0, b1, LANES)).reshape(rows, LANES)

    ta, tb, tc = table(ta_ref), table(tb_ref), table(tc_ref)
    half = ROT_DIM // 2

    def rope(seg, scale):
        fwd = pltpu.roll(seg, LANES - half, 1)
        bwd = pltpu.roll(seg, half, 1)
        r = seg * ta + fwd * tb + bwd * tc
        return r * scale if scale != 1.0 else r

    blocks = [(o_ref, c, scale) for o_ref, scale in ((q_ref, HEAD_DIM ** -0.5), (k_ref, 1.0), (v_ref, None))
              for c in range(0, att_w, MXU_TILE)]
    c0 = 2 * conv_ch
    done = 0
    for bi, (o_ref, c, scale) in enumerate(blocks):
        wcol = c0 + bi * MXU_TILE
        z = jnp.dot(hb, w_ref[0, :, wcol:wcol + MXU_TILE], preferred_element_type=F32)
        if scale is not None:
            z = jnp.concatenate([rope(z[:, l:l + LANES], scale) for l in range(0, MXU_TILE, LANES)], axis=-1)
        o_ref[:, :, c:c + MXU_TILE] = z.reshape(b0, b1, MXU_TILE)
        upto = (bi + 1) * len(conv_chunks) // len(blocks)
        for chunk in conv_chunks[done:upto]:
            chunk()
        done = upto
    if fuse_conv:
        tail = buf[rows:rows + halo, :]
        tail_ref[0] = tail
        buf[0:halo, :] = tail


def _inproj_call(x, mod, gains, layer, w_in_bf, ei, tabs, b0, b1, conv_ch, att_w, conv=None):
    a, b, d = x.shape
    ta, tb, tc = tabs
    tspec = pl.BlockSpec((1, b1, LANES), lambda i, j: (0, j, 0))
    qkv_shapes = tuple(jax.ShapeDtypeStruct((a, b, att_w), F32) for _ in range(3))
    qkv_specs = tuple(_row_spec(b0, b1, att_w) for _ in range(3))
    in_specs = [_row_spec(b0, b1, d), _mod_spec(mod, b0, d, 0), _mod_spec(mod, b0, d, 1),
                _gain_spec(d, layer, 0), _weight_spec(w_in_bf.shape, ei), tspec, tspec, tspec]
    args = [x, mod[0], mod[0], gains, w_in_bf, ta, tb, tc]
    if conv is None:
        out_shapes = (jax.ShapeDtypeStruct((a, b, conv_ch), F32),) + qkv_shapes
        out_specs = (_row_spec(b0, b1, conv_ch),) + qkv_specs
        scratch, sem = [], ("parallel", "parallel")
    else:
        assert b0 == 1 and b1 % CONV_HALO == 0
        conv_w, conv_b, ln_g, ln_b = conv
        in_specs += [_const_spec((CONV_W, conv_ch))] + [_const_spec((1, conv_ch))] * 3
        args += [conv_w, conv_b.reshape(1, conv_ch), ln_g.reshape(1, conv_ch), ln_b.reshape(1, conv_ch)]
        out_shapes = (jax.ShapeDtypeStruct((a, b, conv_ch), F32),
                      jax.ShapeDtypeStruct((a, CONV_HALO, conv_ch), F32)) + qkv_shapes
        out_specs = (_row_spec(b0, b1, conv_ch),
                     pl.BlockSpec((1, CONV_HALO, conv_ch), lambda i, j: (i, 0, 0))) + qkv_specs
        scratch = [pltpu.VMEM((b1 + CONV_HALO, conv_ch), F32),
                   pltpu.VMEM((SUBLANES - 1, b1 + CONV_HALO - SUBLANES, conv_ch), F32),
                   pltpu.VMEM((CONV_W, SUBLANES, conv_ch), F32)]
        sem = ("parallel", "arbitrary")
    return pl.pallas_call(
        functools.partial(_inproj_kernel, conv_ch=conv_ch, att_w=att_w, fuse_conv=conv is not None),
        grid=(a // b0, b // b1),
        in_specs=in_specs,
        out_specs=out_specs,
        out_shape=out_shapes,
        scratch_shapes=scratch,
        compiler_params=_cparams(sem),
        name="in_proj_conv" if conv is not None else "in_proj",
    )(*args)


def _rope_tables(pos):
    half = ROT_DIM // 2
    inv = jnp.float32(ROPE_THETA) ** (-(2.0 / ROT_DIM) * jnp.arange(half, dtype=F32))
    ang = pos.astype(F32)[:, None] * inv[None, :]
    cos, sin = jnp.cos(ang), jnp.sin(ang)
    n = pos.shape[0]
    ones = jnp.ones((n, HEAD_DIM - ROT_DIM), F32)
    zeros = jnp.zeros((n, HEAD_DIM - ROT_DIM), F32)
    zh = jnp.zeros((n, half), F32)
    ta = jnp.concatenate([cos, cos, ones], axis=-1)
    tb = jnp.concatenate([-sin, zh, zeros], axis=-1)
    tc = jnp.concatenate([zh, sin, zeros], axis=-1)
    rep = LANES // HEAD_DIM
    return tuple(jnp.tile(t, (1, rep))[None] for t in (ta, tb, tc))


def _ln_silu(acc, g, b):
    mu = jnp.mean(acc, axis=-1, keepdims=True)
    xc = acc - mu
    y = xc * lax.rsqrt(jnp.mean(xc * xc, axis=-1, keepdims=True) + EPS)
    return _silu(y * g + b)


def _conv_rows(buf, shf, wtile, w_ref, cb_ref, g_ref, b_ref, o_ref, tt, halo):
    sub = shf.shape[0] + 1
    n8 = shf.shape[1]
    for s in range(1, sub):
        shf[s - 1] = buf[s:s + n8, :]
    off = halo - (CONV_W - 1)
    rb = 32
    c = buf.shape[1]
    for jt in range(CONV_W):
        wtile[jt] = jnp.broadcast_to(w_ref[jt:jt + 1, :], (sub, c))

    def chunk(r0):
        acc = None
        for jt in range(CONV_W):
            s = (off + jt) % sub
            lo = r0 + off + jt - s
            rows = buf[lo:lo + rb, :] if s == 0 else shf[s - 1, lo:lo + rb, :]
            term = rows.reshape(rb // sub, sub, c) * wtile[jt][None]
            acc = term if acc is None else acc + term
        acc = acc.reshape(rb, c)
        o_ref[0, r0:r0 + rb, :] = _ln_silu(acc + cb_ref[...], g_ref[...], b_ref[...])

    return [functools.partial(chunk, r0) for r0 in range(0, tt, rb)]


def _conv_sample_kernel(full_ref, w_ref, cb_ref, g_ref, b_ref, o_ref):
    t_out = o_ref.shape[0]
    for t in range(t_out):
        acc = w_ref[0:1, :] * full_ref[t]
        for jt in range(1, CONV_W):
            acc = acc + w_ref[jt:jt + 1, :] * full_ref[t + jt]
        o_ref[t] = _ln_silu(acc + cb_ref[...], g_ref[...], b_ref[...])


def _conv_sample_call(full_t, conv_w, conv_b, ln_g, ln_b):
    tf, nb, c = full_t.shape
    t_out = tf - (CONV_W - 1)
    nbk = 32
    return pl.pallas_call(
        _conv_sample_kernel,
        grid=(nb // nbk,),
        in_specs=[pl.BlockSpec((tf, nbk, c), lambda i: (0, i, 0)),
                  _const_spec((CONV_W, c)), _const_spec((1, c)), _const_spec((1, c)), _const_spec((1, c))],
        out_specs=pl.BlockSpec((t_out, nbk, c), lambda i: (0, i, 0)),
        out_shape=jax.ShapeDtypeStruct((t_out, nb, c), F32),
        compiler_params=_cparams(("parallel",)),
        name="conv_sample",
    )(full_t, conv_w, conv_b.reshape(1, c), ln_g.reshape(1, c), ln_b.reshape(1, c))


def _merge_groups(os_, ls_):
    m = functools.reduce(jnp.maximum, ls_)
    es = [jnp.exp(l - m) for l in ls_]
    num = functools.reduce(lambda a, b: a + b, [e * o for e, o in zip(es, os_)])
    return num / functools.reduce(lambda a, b: a + b, es)


def _attn_prompt_kernel(*refs, tb):
    in_refs, att_ref, scr = refs[:5 * N_DIL], refs[5 * N_DIL], refs[5 * N_DIL + 1:]
    kbufs, vbufs, (o_scr, l_scr) = scr[:N_DIL], scr[N_DIL:2 * N_DIL], scr[2 * N_DIL:]
    jb = pl.program_id(1)
    bb = BAND_BLOCK
    qi = lax.broadcasted_iota(jnp.int32, (bb, 2 * bb), 0) + bb
    ki = lax.broadcasted_iota(jnp.int32, (bb, 2 * bb), 1)
    dist = qi - ki
    band = (dist >= 0) & (dist <= bb)
    older = ki < bb
    assert LANES == 2 * HEAD_DIM
    head0 = lax.broadcasted_iota(jnp.int32, (bb, LANES), 1) < HEAD_DIM
    ones_kv = jnp.ones((2 * bb, LANES), BF16)
    for g, (win, dil) in enumerate(DIL_GROUPS):
        q_ref, kc_ref, kp_ref, vc_ref, vp_ref = in_refs[5 * g:5 * g + 5]
        kbuf, vbuf = kbufs[g], vbufs[g]
        pb = bb * dil
        kbuf[0:pb, :] = kp_ref[0]
        kbuf[pb:pb + tb, :] = kc_ref[0]
        vbuf[0:pb, :] = vp_ref[0]
        vbuf[pb:pb + tb, :] = vc_ref[0]

        def units(ug, carry, g=g, dil=dil, pb=pb, q_ref=q_ref, kbuf=kbuf, vbuf=vbuf):
            qsels, qus, kus, vus, masks = [], [], [], [], []
            for kk in range(ATTN_UNITS):
                u = ug * ATTN_UNITS + kk
                i = _div_pow2(u, dil)
                r = _mod_pow2(u, dil)
                if dil == 1:
                    start = pl.multiple_of(i * pb, bb)
                    qsel, ksel = pl.ds(start, bb), pl.ds(start, 2 * bb)
                else:
                    start = i * pb + r
                    qsel, ksel = pl.ds(start, bb, stride=dil), pl.ds(start, 2 * bb, stride=dil)
                qsels.append(qsel)
                qus.append(q_ref[0, qsel, :].astype(BF16))
                kus.append(kbuf[ksel, :].astype(BF16))
                vus.append(vbuf[ksel, :].astype(BF16))
                first = jnp.logical_and(jb == 0, i == 0)
                masks.append(jnp.logical_and(band, jnp.logical_not(jnp.logical_and(older, first))))
            chains = [(kk, slice(h * HEAD_DIM, (h + 1) * HEAD_DIM))
                      for kk in range(ATTN_UNITS) for h in range(LANES // HEAD_DIM)]
            nt = (((1,), (1,)), ((), ()))
            ss = [jnp.where(masks[kk], lax.dot_general(qus[kk][:, cs], kus[kk][:, cs], nt,
                                                       preferred_element_type=F32), -jnp.inf) for kk, cs in chains]
            ms = [jnp.max(s, axis=-1, keepdims=True) for s in ss]
            ps = [jnp.exp(s - m).astype(BF16) for s, m in zip(ss, ms)]
            outs = [jnp.dot(p, vus[kk], preferred_element_type=F32) for p, (kk, _) in zip(ps, chains)]
            sums = [jnp.dot(p, ones_kv, preferred_element_type=F32) for p in ps]
            for kk in range(ATTN_UNITS):
                o_cat = jnp.where(head0, outs[2 * kk], outs[2 * kk + 1])
                l_cat = jnp.where(head0, sums[2 * kk], sums[2 * kk + 1])
                m_cat = jnp.where(head0, ms[2 * kk], ms[2 * kk + 1])
                o_scr[g, qsels[kk], :] = o_cat / l_cat
                l_scr[g, qsels[kk], :] = m_cat + jnp.log(l_cat)
            return carry

        lax.fori_loop(0, tb // bb // ATTN_UNITS, units, 0)

    mr = 256
    for r0 in range(0, tb, mr):
        att_ref[0, r0:r0 + mr, :] = _merge_groups([o_scr[g, r0:r0 + mr, :] for g in range(N_DIL)],
                                                  [l_scr[g, r0:r0 + mr, :] for g in range(N_DIL)])


def _attn_prompt_call(q, k, v):
    n, l, att_w = q.shape
    bb = BAND_BLOCK
    tb = bb * max(d for _, d in DIL_GROUPS)
    assert l % tb == 0 and all(w // d == bb for w, d in DIL_GROUPS) and att_w == N_DIL * GRP_W
    hp = GRP_W // LANES
    in_specs, args, kv_scr = [], [], []
    for g, (_, dil) in enumerate(DIL_GROUPS):
        pb = bb * dil
        ratio = tb // pb
        cur = pl.BlockSpec((1, tb, LANES), lambda i, j, h, g=g: (i, j, g * hp + h))
        prev = pl.BlockSpec((1, pb, LANES),
                            lambda i, j, h, g=g, ratio=ratio: (i, jnp.maximum(j * ratio - 1, 0), g * hp + h))
        in_specs += [cur, cur, prev, cur, prev]
        args += [q, k, k, v, v]
        kv_scr.append(pltpu.VMEM((pb + tb, LANES), F32))
    return pl.pallas_call(
        functools.partial(_attn_prompt_kernel, tb=tb),
        grid=(n, l // tb, hp),
        in_specs=in_specs,
        out_specs=pl.BlockSpec((1, tb, LANES), lambda i, j, h: (i, j, h)),
        out_shape=jax.ShapeDtypeStruct((n, l, GRP_W), F32),
        scratch_shapes=kv_scr + kv_scr + [pltpu.VMEM((N_DIL, tb, LANES), F32)] * 2,
        compiler_params=_cparams(("parallel", "arbitrary", "arbitrary")),
        name="attn_prompt",
    )(*args)


def _attn_sample_kernel(q_ref, kn_ref, vn_ref, c0_ref, c1_ref, c2_ref, att_ref):
    t_new = q_ref.shape[1]
    rows = HPG * t_new
    ri = lax.broadcasted_iota(jnp.int32, (rows, GRP_W), 0)
    ci = lax.broadcasted_iota(jnp.int32, (rows, GRP_W), 1)
    head_mask = _div_pow2(ci, HEAD_DIM) == _div_pow2(ri, t_new)
    c_refs = (c0_ref, c1_ref, c2_ref)
    valid = []
    for (win, dil), c_ref in zip(DIL_GROUPS, c_refs):
        wb = c_ref.shape[-1]
        tq = _mod_pow2(lax.broadcasted_iota(jnp.int32, (rows, wb), 0), t_new)
        ki = lax.broadcasted_iota(jnp.int32, (rows, wb), 1)
        dist = wb + tq - ki
        tqn = _mod_pow2(lax.broadcasted_iota(jnp.int32, (rows, t_new), 0), t_new)
        jn = lax.broadcasted_iota(jnp.int32, (rows, t_new), 1)
        dn = tqn - jn
        valid.append(((dist >= 0) & (dist <= win) & (_mod_pow2(dist, dil) == 0),
                      (dn >= 0) & (dn <= win) & (_mod_pow2(dn, dil) == 0)))
    nns = range(q_ref.shape[0])
    os_, ls_ = [[] for _ in nns], [[] for _ in nns]
    for gi, c_ref in enumerate(c_refs):
        cs = slice(gi * GRP_W, (gi + 1) * GRP_W)
        valid_c, valid_n = valid[gi]
        qbs = [jnp.where(head_mask, jnp.concatenate([q_ref[nn, :, cs]] * HPG, axis=0), 0.0) for nn in nns]
        s_cs = [jnp.where(valid_c, _bdot(qbs[nn], c_ref[0, nn, 0]), -jnp.inf) for nn in nns]
        s_ns = [jnp.where(valid_n, _bdot_nt(qbs[nn], kn_ref[nn, :, cs]), -jnp.inf) for nn in nns]
        ms = [jnp.maximum(jnp.max(s_cs[nn], axis=-1, keepdims=True), jnp.max(s_ns[nn], axis=-1, keepdims=True))
              for nn in nns]
        p_cs = [jnp.exp(s_cs[nn] - ms[nn]) for nn in nns]
        p_ns = [jnp.exp(s_ns[nn] - ms[nn]) for nn in nns]
        lsums = [jnp.sum(p_cs[nn], axis=-1, keepdims=True) + jnp.sum(p_ns[nn], axis=-1, keepdims=True)
                 for nn in nns]
        o_fulls = [(_bdot_nt(p_cs[nn], c_ref[0, nn, 1]) + _bdot(p_ns[nn], vn_ref[nn, :, cs])) / lsums[nn]
                   for nn in nns]
        for nn in nns:
            o_full = jnp.where(head_mask, o_fulls[nn], 0.0)
            lse_full = jnp.where(head_mask, jnp.broadcast_to(ms[nn] + jnp.log(lsums[nn]), (rows, GRP_W)), 0.0)
            o = o_full[0:t_new]
            lse = lse_full[0:t_new]
            for h in range(1, HPG):
                o = o + o_full[h * t_new:(h + 1) * t_new]
                lse = lse + lse_full[h * t_new:(h + 1) * t_new]
            os_[nn].append(o)
            ls_[nn].append(lse)
    for nn in nns:
        att_ref[nn] = _merge_groups(os_[nn], ls_[nn])


def _cache_rows_on_lanes(cache):
    ly, nb, wb = cache.shape[:3]
    return jnp.transpose(cache, (0, 1, 3, 4, 5, 2)).reshape(ly, nb, 2, GRP_W, wb)


def _attn_sample_call(q, k, v, caches_t, ei):
    nb, t_new, att_w = q.shape
    assert all(c.shape[-1] == w for c, (w, _) in zip(caches_t, DIL_GROUPS))
    n_blk = 2 if nb % 2 == 0 else 1
    qspec = pl.BlockSpec((n_blk, t_new, att_w), lambda i: (i, 0, 0))
    cspecs = [pl.BlockSpec((1, n_blk) + c.shape[2:], lambda i: (ei, i, 0, 0, 0)) for c in caches_t]
    return pl.pallas_call(
        _attn_sample_kernel,
        grid=(nb // n_blk,),
        in_specs=[qspec, qspec, qspec] + cspecs,
        out_specs=pl.BlockSpec((n_blk, t_new, GRP_W), lambda i: (i, 0, 0)),
        out_shape=jax.ShapeDtypeStruct((nb, t_new, GRP_W), F32),
        compiler_params=_cparams(("parallel",)),
        name="attn_sample",
    )(q, k, v, *caches_t)


def _ffn_kernel(*refs, fchunks, final, mix):
    x_ref, sh_ref, sc_ref, g2_ref, ng_ref, wg_ref, wu_ref, wd_ref, fg_ref = refs[:9]
    o_ref = refs[-1]
    b0, b1, d = x_ref.shape
    rows = b0 * b1
    x = x_ref[...]
    if mix:
        g1_ref, yc_ref, att_ref, wa_ref, wb_ref = refs[9:14]
        mixed = _bdot(yc_ref[...].reshape(rows, -1), wa_ref[0]) + _bdot(att_ref[...].reshape(rows, -1), wb_ref[0])
        x = x + g1_ref[0] * mixed.reshape(b0, b1, d)
    hb = _norm_mod(x, ng_ref[0, 0], sc_ref[0], sh_ref[0]).reshape(rows, d).astype(BF16)
    dff = wg_ref.shape[2]
    acc = None
    for f0, fchunk in fchunks:
        g = jnp.dot(hb, wg_ref[0, :, f0:f0 + fchunk], preferred_element_type=F32)
        u = jnp.dot(hb, wu_ref[0, :, f0:f0 + fchunk], preferred_element_type=F32)
        part = jnp.dot((_silu(g) * u).astype(BF16), wd_ref[0, f0:f0 + fchunk, :], preferred_element_type=F32)
        acc = part if acc is None else acc + part
    xn = x + g2_ref[0] * acc.reshape(b0, b1, d)
    if final:
        xn = xn * lax.rsqrt(jnp.mean(xn * xn, axis=-1, keepdims=True) + EPS) * fg_ref[...]
    o_ref[...] = xn


def _ffn_call(x, mod, gains, layer, wg_bf, wu_bf, wd_bf, final_g, b0, b1, final, mix=None):
    a, b, d = x.shape
    dff = wg_bf.shape[2]
    in_specs = [_row_spec(b0, b1, d), _mod_spec(mod, b0, d, 3), _mod_spec(mod, b0, d, 4), _mod_spec(mod, b0, d, 5),
                _gain_spec(d, layer, 1), _weight_spec(wg_bf.shape, layer), _weight_spec(wu_bf.shape, layer),
                _weight_spec(wd_bf.shape, layer), _const_spec((1, d))]
    args = [x, mod[0], mod[0], mod[0], gains, wg_bf, wu_bf, wd_bf, final_g.reshape(1, d)]
    if mix is not None:
        yc, att, wa_bf, wb_bf, ei = mix
        in_specs += [_mod_spec(mod, b0, d, 2), _row_spec(b0, b1, yc.shape[-1]), _row_spec(b0, b1, att.shape[-1]),
                     _weight_spec(wa_bf.shape, ei), _weight_spec(wb_bf.shape, ei)]
        args += [mod[0], yc, att, wa_bf, wb_bf]
    tiles = -(-dff // MXU_TILE)
    f_split = min(dff, -(-tiles // 2) * MXU_TILE)
    fchunks = tuple((f0, f1 - f0) for f0, f1 in ((0, f_split), (f_split, dff)) if f1 > f0)
    return pl.pallas_call(
        functools.partial(_ffn_kernel, fchunks=fchunks, final=final, mix=mix is not None),
        grid=(a // b0, b // b1),
        in_specs=in_specs,
        out_specs=_row_spec(b0, b1, d),
        out_shape=jax.ShapeDtypeStruct((a, b, d), F32),
        compiler_params=_cparams(("parallel", "parallel")),
        name=("mix_" if mix is not None else "") + ("ffn_final" if final else "ffn"),
    )(*args)


def _s5_param_kernel(lr_ref, li_ref, ldt_ref, br_ref, bi_ref, abr_ref, abi_ref, bbr_ref, bbi_ref,
                     sbr_ref, sbi_ref):
    lr, li = lr_ref[0], li_ref[0]
    dt = jnp.exp(ldt_ref[0])
    mag = jnp.exp(lr * dt)
    ph = li * dt
    abr = mag * jnp.cos(ph)
    abi = mag * jnp.sin(ph)
    den = lr * lr + li * li
    nr = abr - 1.0
    fr = (nr * lr + abi * li) / den
    fi = (abi * lr - nr * li) / den
    abr_ref[0] = abr
    abi_ref[0] = abi
    for h in range(br_ref.shape[1]):
        br, bi = br_ref[0, h], bi_ref[0, h]
        bbr = fr * br - fi * bi
        bbi = fr * bi + fi * br
        bbr_ref[0, h] = bbr
        bbi_ref[0, h] = bbi
        sbr_ref[0, h] = abr * bbr - abi * bbi
        sbi_ref[0, h] = abr * bbi + abi * bbr


def _s5_param_call(lam_re, lam_im, log_dt, b_re, b_im):
    n_odd, g, p = lam_re.shape
    hh = b_re.shape[-1]
    bt_re = jnp.transpose(b_re, (0, 3, 1, 2))
    bt_im = jnp.transpose(b_im, (0, 3, 1, 2))
    gp = pl.BlockSpec((1, g, p), lambda l: (l, 0, 0))
    bs = pl.BlockSpec((1, hh, g, p), lambda l: (l, 0, 0, 0))
    return pl.pallas_call(
        _s5_param_kernel,
        grid=(n_odd,),
        in_specs=[gp, gp, pl.BlockSpec((1, g, 1), lambda l: (l, 0, 0)), bs, bs],
        out_specs=(gp, gp, bs, bs, bs, bs),
        out_shape=(jax.ShapeDtypeStruct((n_odd, g, p), F32),) * 2
        + (jax.ShapeDtypeStruct((n_odd, hh, g, p), F32),) * 4,
        compiler_params=_cparams(("arbitrary",)),
        name="s5_params",
    )(lam_re, lam_im, log_dt.reshape(n_odd, g, 1), bt_re, bt_im)


def _s5_block_weights(bbr, bbi, sbr, sbi, c_re, c_im):
    hh, g, p = bbr.shape
    gl = g // S5_JBLK
    eye = jnp.eye(gl, dtype=F32)

    def bblk(t):
        t = t.reshape(hh, S5_JBLK, gl, p).transpose(1, 2, 0, 3)
        t = t[:, :, :, None, :] * eye[None, :, None, :, None]
        return t.reshape(S5_JBLK, gl * hh, gl * p)

    def cblk(t):
        t = t.reshape(S5_JBLK, gl, hh, p).transpose(0, 1, 3, 2)
        t = t[:, :, :, None, :] * eye[None, :, None, :, None]
        return t.reshape(S5_JBLK, gl * p, gl * hh)

    bb = jnp.concatenate([bblk(bbr), bblk(bbi)], axis=-1).astype(BF16)
    sb = jnp.concatenate([bblk(sbr), bblk(sbi)], axis=-1).astype(BF16)
    cc = jnp.concatenate([cblk(c_re), cblk(-c_im)], axis=1).astype(BF16)
    return bb, jnp.concatenate([bb, sb], axis=1), cc


def _gelu_tanh(x):
    return 0.5 * x * (1.0 + jnp.tanh(math.sqrt(2.0 / math.pi) * (x + 0.044715 * (x * x * x))))


def _s5_kernel(x_ref, sh_ref, sc_ref, g1_ref, ng_ref, s0r_ref, s0i_ref, abr_ref, abi_ref, perm_ref, bb_ref, cc_ref,
               dsk_ref, wglu_ref, bglu_ref, xo_ref, sro_ref, sio_ref,
               st_r, st_i, bu_scr, h_scr, yp_scr, y_scr, *, lw, unroll):
    c = pl.program_id(1)
    nbk, tc, d = x_ref.shape
    rows = nbk * tc
    jw = d // S5_JBLK
    sw = bb_ref.shape[2] // 2
    w = lw * LANES
    sub = 8

    two_step = nbk % sub != 0
    if two_step:
        assert 2 * nbk == sub and tc % 2 == 0 and bb_ref.shape[1] == 2 * jw

    @pl.when(c == 0)
    def _():
        if two_step:
            st_r[0:nbk, :] = jnp.zeros((nbk, st_r.shape[1]), F32)
            st_i[0:nbk, :] = jnp.zeros((nbk, st_i.shape[1]), F32)
            st_r[nbk:sub, :] = s0r_ref[...]
            st_i[nbk:sub, :] = s0i_ref[...]
        else:
            st_r[...] = s0r_ref[...]
            st_i[...] = s0i_ref[...]

    x = x_ref[...]
    u2 = _norm_mod(x, ng_ref[0, 0], sc_ref[0], sh_ref[0]).reshape(rows, d)
    ub = u2.astype(BF16)
    up = jnp.dot(perm_ref[0], ub, preferred_element_type=F32).astype(BF16)
    if two_step:
        ups = jnp.dot(perm_ref[1], ub, preferred_element_type=F32).astype(BF16)

    def project_in(j):
        lhs = up[:, j * jw:(j + 1) * jw]
        if two_step:
            lhs = jnp.concatenate([lhs, ups[:, j * jw:(j + 1) * jw]], axis=1)
        bu_scr[j % 2] = jnp.dot(lhs, bb_ref[j], preferred_element_type=F32)

    project_in(0)
    for j in range(S5_JBLK):
        if j + 1 < S5_JBLK:
            project_in(j + 1)
        bu, hs = bu_scr.at[j % 2], h_scr.at[j % 2]
        for l0 in range(0, sw, w):
            c_re, c_im = slice(l0, l0 + w), slice(sw + l0, sw + l0 + w)
            c_st = slice(j * sw + l0, j * sw + l0 + w)
            if nbk % sub == 0:
                ar = jnp.broadcast_to(abr_ref[:, c_st], (nbk, w))
                ai = jnp.broadcast_to(abi_ref[:, c_st], (nbk, w))

                def step(t, carry, c_re=c_re, c_im=c_im, ar=ar, ai=ai):
                    hr, hi = carry
                    rs = pl.ds(pl.multiple_of(t * nbk, nbk), nbk)
                    nhr = ar * hr - ai * hi + bu[rs, c_re]
                    nhi = ar * hi + ai * hr + bu[rs, c_im]
                    hs[rs, c_re] = nhr
                    hs[rs, c_im] = nhi
                    return nhr, nhi

                hr, hi = lax.fori_loop(0, tc, step, (st_r[:, c_st], st_i[:, c_st]), unroll=unroll)
                st_r[:, c_st] = hr
                st_i[:, c_st] = hi
            else:
                ar = jnp.broadcast_to(abr_ref[:, c_st], (sub, w))
                ai = jnp.broadcast_to(abi_ref[:, c_st], (sub, w))
                a2r, a2i = ar * ar - ai * ai, 2.0 * (ar * ai)
                lower = lax.broadcasted_iota(jnp.int32, (sub, w), 0) < nbk
                pr, pi = st_r[:, c_st], st_i[:, c_st]
                xr, xi = bu[0:sub, c_re], bu[0:sub, c_im]
                qr, qi = pltpu.roll(pr, nbk, 0), pltpu.roll(pi, nbk, 0)
                hr = jnp.where(lower, ar * qr - ai * qi, a2r * pr - a2i * pi) + xr
                hi = jnp.where(lower, ar * qi + ai * qr, a2r * pi + a2i * pr) + xi
                hs[0:sub, c_re] = hr
                hs[0:sub, c_im] = hi

                def step2(i, carry, c_re=c_re, c_im=c_im, a2r=a2r, a2i=a2i):
                    sr, si = carry
                    rs = pl.ds(pl.multiple_of(i * sub, sub), sub)
                    nr = a2r * sr - a2i * si + bu[rs, c_re]
                    ni = a2r * si + a2i * sr + bu[rs, c_im]
                    hs[rs, c_re] = nr
                    hs[rs, c_im] = ni
                    return nr, ni

                sr, si = lax.fori_loop(1, tc // 2, step2, (hr, hi), unroll=unroll)
                st_r[:, c_st] = sr
                st_i[:, c_st] = si
        yj = jnp.dot(hs[...].astype(BF16), cc_ref[j], preferred_element_type=F32)
        for k in range(jw // LANES):
            yp_scr[j * (jw // LANES) + k] = yj[:, k * LANES:(k + 1) * LANES]

    nlc = d // LANES
    for k in range(nlc):
        if nbk <= tc:
            for n in range(nbk):
                y_scr[k, n * tc:(n + 1) * tc, :] = yp_scr[k, pl.ds(n, tc, stride=nbk), :]
        else:
            for t in range(tc):
                y_scr[k, pl.ds(t, nbk, stride=tc), :] = yp_scr[k, t * nbk:(t + 1) * nbk, :]
    y = jnp.concatenate([y_scr[k] for k in range(nlc)], axis=-1) + dsk_ref[...] * u2
    g = _bdot(_gelu_tanh(y), wglu_ref[0]) + bglu_ref[...]
    out = g[:, 0:d] * jax.nn.sigmoid(g[:, d:2 * d])
    xo_ref[...] = x + g1_ref[0] * out.reshape(nbk, tc, d)

    @pl.when(c == pl.num_programs(1) - 1)
    def _():
        if two_step:
            sro_ref[...] = st_r[nbk:sub, :]
            sio_ref[...] = st_i[nbk:sub, :]
        else:
            sro_ref[...] = st_r[...]
            sio_ref[...] = st_i[...]


def _s5_call(x, mod, gains, layer, s0r, s0i, abr, abi, bb, cc, d_skip, wglu_bf, oi, b_glu, nbk, tc, lw, unroll):
    a, b, d = x.shape
    ns = s0r.shape[1]
    rows = nbk * tc
    dst = jnp.arange(rows)
    src = (dst % nbk) * tc + dst // nbk
    col = jnp.arange(rows)[None, :]
    perm = jnp.stack([src[:, None] == col,
                      jnp.logical_and(src[:, None] - 1 == col, (dst // nbk > 0)[:, None])]).astype(BF16)
    st_rows = nbk if nbk % 8 == 0 else 2 * nbk
    sspec = pl.BlockSpec((nbk, ns), lambda i, j: (i, 0))
    return pl.pallas_call(
        functools.partial(_s5_kernel, lw=lw, unroll=unroll),
        grid=(a // nbk, b // tc),
        in_specs=[_row_spec(nbk, tc, d), _mod_spec(mod, nbk, d, 0), _mod_spec(mod, nbk, d, 1),
                  _mod_spec(mod, nbk, d, 2), _gain_spec(d, layer, 0), sspec, sspec,
                  _const_spec((1, ns)), _const_spec((1, ns)),
                  _weight_spec(perm.shape), _weight_spec(bb.shape), _weight_spec(cc.shape), _const_spec((1, d)),
                  _weight_spec(wglu_bf.shape, oi), _const_spec((1, 2 * d))],
        out_specs=(_row_spec(nbk, tc, d), sspec, sspec),
        out_shape=(jax.ShapeDtypeStruct((a, b, d), F32),
                   jax.ShapeDtypeStruct((a, ns), F32), jax.ShapeDtypeStruct((a, ns), F32)),
        scratch_shapes=[pltpu.VMEM((st_rows, ns), F32), pltpu.VMEM((st_rows, ns), F32),
                        pltpu.VMEM((2, rows, bb.shape[2]), F32), pltpu.VMEM((2, rows, bb.shape[2]), F32),
                        pltpu.VMEM((d // LANES, rows, LANES), F32), pltpu.VMEM((d // LANES, rows, LANES), F32)],
        compiler_params=_cparams(("parallel", "arbitrary")),
        name="s5_mixer",
    )(x, mod[0], mod[0], mod[0], gains, s0r, s0i, abr.reshape(1, ns), abi.reshape(1, ns), perm, bb, cc,
      d_skip.reshape(1, d), wglu_bf, b_glu.reshape(1, 2 * d))


def _run_trunk(x, mods, tabs, conv_st, kv_st, s5_st, W, b0, b1, s5_tiles):
    a, b, d = x.shape
    depth = mods[0].shape[0]
    conv_ch = W['conv_w'].shape[-1]
    att_w = N_DIL * GRP_W
    new_conv, new_s5 = [], []
    new_kv = [[] for _ in range(N_DIL)]
    ei = oi = 0
    for layer in range(depth):
        mod = (mods[0], layer, mods[1])
        mix = None
        if layer % 2 == 0:
            cw = (W['conv_w'][ei], W['conv_b'][ei], W['conv_ln_g'][ei], W['conv_ln_b'][ei])
            if conv_st is None:
                yc, u_tail, q, k, v = _inproj_call(x, mod, W['gains'], layer, W['w_in_bf'], ei, tabs, b0, b1,
                                                   conv_ch, att_w, conv=cw)
                new_conv.append(u_tail[:, CONV_HALO - (CONV_W - 1):])
                att = _attn_prompt_call(q, k, v)
                for gi, (win, _) in enumerate(DIL_GROUPS):
                    keep = min(win, b)
                    kg = k[:, b - keep:, gi * GRP_W:(gi + 1) * GRP_W].reshape(a, keep, HPG, HEAD_DIM)
                    vg = v[:, b - keep:, gi * GRP_W:(gi + 1) * GRP_W].reshape(a, keep, HPG, HEAD_DIM)
                    new_kv[gi].append(jnp.stack([kg, vg], axis=2))
            else:
                u, q, k, v = _inproj_call(x, mod, W['gains'], layer, W['w_in_bf'], ei, tabs, b0, b1, conv_ch, att_w)
                full = jnp.concatenate([conv_st[ei], u], axis=1)
                yc = jnp.transpose(_conv_sample_call(jnp.transpose(full, (1, 0, 2)), *cw), (1, 0, 2))
                new_conv.append(full[:, -(CONV_W - 1):])
                att = _attn_sample_call(q, k, v, kv_st, ei)
                for gi in range(N_DIL):
                    kg = k[:, :, gi * GRP_W:(gi + 1) * GRP_W].reshape(a, b, HPG, HEAD_DIM)
                    vg = v[:, :, gi * GRP_W:(gi + 1) * GRP_W].reshape(a, b, HPG, HEAD_DIM)
                    new_kv[gi].append(jnp.stack([kg, vg], axis=2))
            mix = (yc, att, W['w_o_a_bf'], W['w_o_b_bf'], ei)
            ei += 1
        else:
            ns = W['abr'].shape[1] * W['abr'].shape[2]
            if s5_st is None:
                s0r = jnp.zeros((a, ns), F32)
                s0i = jnp.zeros((a, ns), F32)
            else:
                s0r = s5_st[oi][..., 0].reshape(a, ns)
                s0i = s5_st[oi][..., 1].reshape(a, ns)
            nbk, tc, lw, unroll = s5_tiles
            bb = W['bb'][oi] if nbk % 8 == 0 else W['bb2'][oi]
            x, sr, si = _s5_call(x, mod, W['gains'], layer, s0r, s0i, W['abr'][oi], W['abi'][oi],
                                 bb, W['cc'][oi], W['s5_d'][oi], W['s5_w_glu_bf'], oi,
                                 W['s5_b_glu'][oi], nbk, tc, lw, unroll)
            g, p = W['abr'].shape[1:]
            new_s5.append(jnp.stack([sr.reshape(a, g, p), si.reshape(a, g, p)], axis=-1))
            oi += 1
        x = _ffn_call(x, mod, W['gains'], layer, W['w_ff_gate_bf'], W['w_ff_up_bf'],
                      W['w_ff_down_bf'], W['final_g'], b0, b1, final=(layer == depth - 1), mix=mix)
    return x, jnp.stack(new_conv), [jnp.stack(kv) for kv in new_kv], jnp.stack(new_s5)


def kernel(x_prompt, x_sample, cache_conv, cache_kv_g0, cache_kv_g1, cache_kv_g2, state_s5, c_prompt, c_sample, norm_g, final_g, w_ada, b_ada, w_in, conv_w, conv_b, conv_ln_g, conv_ln_b, w_o, s5_lam_re, s5_lam_im, s5_log_dt, s5_b_re, s5_b_im, s5_c_re, s5_c_im, s5_d, s5_w_glu, s5_b_glu, w_ff_gate, w_ff_up, w_ff_down):
    n, l, d = x_prompt.shape
    nb, t_new, _ = x_sample.shape
    depth = w_ada.shape[0]
    conv_ch = conv_w.shape[-1]

    m_all = n + nb
    m_pad = -(-m_all // 8) * 8
    c_all = jnp.concatenate([c_sample, c_prompt, jnp.zeros((m_pad - m_all, d), F32)], axis=0)
    mod4 = _ada_call(c_all, w_ada, b_ada).reshape(depth, m_pad, 1, 6 * d)

    abr, abi, bbr, bbi, sbr, sbi = _s5_param_call(s5_lam_re, s5_lam_im, s5_log_dt, s5_b_re, s5_b_im)
    blocks = [_s5_block_weights(bbr[i], bbi[i], sbr[i], sbi[i], s5_c_re[i], s5_c_im[i])
              for i in range(abr.shape[0])]

    W = dict(gains=norm_g.reshape(depth, 2, 1, d), final_g=final_g, conv_w=conv_w, conv_b=conv_b, conv_ln_g=conv_ln_g,
             conv_ln_b=conv_ln_b, s5_d=s5_d, s5_b_glu=s5_b_glu, abr=abr, abi=abi,
             bb=[b[0] for b in blocks], bb2=[b[1] for b in blocks], cc=[b[2] for b in blocks],
             w_in_bf=w_in.astype(BF16), w_o_a_bf=w_o[:, :conv_ch].astype(BF16),
             w_o_b_bf=w_o[:, conv_ch:].astype(BF16), s5_w_glu_bf=s5_w_glu.astype(BF16),
             w_ff_gate_bf=w_ff_gate.astype(BF16), w_ff_up_bf=w_ff_up.astype(BF16),
             w_ff_down_bf=w_ff_down.astype(BF16))

    tabs_p = _rope_tables(jnp.arange(l))
    tabs_s = _rope_tables(PAST_LEN + jnp.arange(t_new))

    tm = min(512, l)
    y_p, conv_p, kvs_p, s5_p = _run_trunk(x_prompt, (mod4, nb), tabs_p, None, None, None, W,
                                          1, tm, (n, 128, 8, True))
    nbk = min(nb, tm // t_new)
    caches_t = [_cache_rows_on_lanes(c) for c in (cache_kv_g0, cache_kv_g1, cache_kv_g2)]
    y_s, conv_s, kvs_s, s5_s = _run_trunk(x_sample, (mod4, 0), tabs_s, cache_conv, caches_t, state_s5, W,
                                          nbk, t_new, (nbk, t_new, 1, True))
    return (y_p, y_s, conv_p, kvs_p[0], kvs_p[1], kvs_p[2], s5_p,
            conv_s, kvs_s[0], kvs_s[1], kvs_s[2], s5_s)
```

```python
import functools
import math

import jax
import jax.numpy as jnp
from jax import lax
from jax.experimental import pallas as pl
from jax.experimental.pallas import tpu as pltpu

F32 = jnp.float32
BF16 = jnp.bfloat16

EPS = 1e-6
CONV_W = 31
HEAD_DIM = 64
HPG = 4
DIL_GROUPS = ((128, 1), (512, 4), (2048, 16))
N_DIL = len(DIL_GROUPS)
GRP_W = HPG * HEAD_DIM
ROT_DIM = HEAD_DIM // 4
ROPE_THETA = 500000.0
BAND_BLOCK = 128
S5_GROUP = 16
S5_STATE = 64
S5_JBLK = 4
PAST_LEN = 2048
LANES = 128
MXU_TILE = 256
SUBLANES = 8
ATTN_UNITS = 4
CONV_HALO = 32
VMEM_LIMIT = 56 * 1024 * 1024


def _cparams(sem):
    return pltpu.CompilerParams(dimension_semantics=sem, vmem_limit_bytes=VMEM_LIMIT)


def _const_spec(shape):
    nd = len(shape)
    return pl.BlockSpec(shape, lambda *_: (0,) * nd)


def _weight_spec(shape, layer=None):
    nd = len(shape)
    if layer is None:
        return pl.BlockSpec(shape, lambda *_: (0,) * nd, pipeline_mode=pl.Buffered(1))
    return pl.BlockSpec((1,) + tuple(shape[1:]), lambda *_: (layer,) + (0,) * (nd - 1),
                        pipeline_mode=pl.Buffered(1))


def _row_spec(b0, b1, cols):
    return pl.BlockSpec((b0, b1, cols), lambda i, j: (i, j, 0))


def _mod_spec(mod, b0, d, k):
    _, layer, row0 = mod
    assert row0 % b0 == 0
    return pl.BlockSpec((1, b0, 1, d), lambda i, j: (layer, row0 // b0 + i, 0, k))


def _gain_spec(d, layer, which):
    return pl.BlockSpec((1, 1, 1, d), lambda *_: (layer, which, 0, 0))


def _norm_mod(x, g, sc, sh):
    y = x * lax.rsqrt(jnp.mean(x * x, axis=-1, keepdims=True) + EPS)
    return (y * g) * (1.0 + sc) + sh


def _silu(x):
    return x * jax.nn.sigmoid(x)


def _log2(n):
    assert n > 0 and n & (n - 1) == 0, n
    return n.bit_length() - 1


def _div_pow2(x, n):
    return x >> _log2(n)


def _mod_pow2(x, n):
    return x & ((1 << _log2(n)) - 1)


def _bdot(a, b):
    return jnp.dot(a.astype(BF16), b.astype(BF16), preferred_element_type=F32)


def _bdot_nt(a, b):
    return lax.dot_general(a.astype(BF16), b.astype(BF16), (((1,), (1,)), ((), ())),
                           preferred_element_type=F32)


def _ada_kernel(c_ref, w_ref, b_ref, o_ref):
    o_ref[0] = _bdot(_silu(c_ref[...]), w_ref[0]) + b_ref[0]


def _ada_call(c_all, w_ada, b_ada):
    depth, d, d6 = w_ada.shape
    m = c_all.shape[0]
    tn = 1536
    return pl.pallas_call(
        _ada_kernel,
        grid=(depth, d6 // tn),
        in_specs=[pl.BlockSpec((m, d), lambda l, j: (0, 0)),
                  pl.BlockSpec((1, d, tn), lambda l, j: (l, 0, j)),
                  pl.BlockSpec((1, 1, tn), lambda l, j: (l, 0, j))],
        out_specs=pl.BlockSpec((1, m, tn), lambda l, j: (l, 0, j)),
        out_shape=jax.ShapeDtypeStruct((depth, m, d6), F32),
        compiler_params=_cparams(("arbitrary", "arbitrary")),
        name="ada_mod",
    )(c_all, w_ada, b_ada.reshape(depth, 1, d6))


def _inproj_kernel(*refs, conv_ch, att_w, fuse_conv):
    x_ref, sh_ref, sc_ref, ng_ref, w_ref, ta_ref, tb_ref, tc_ref = refs[:8]
    if fuse_conv:
        cw_ref, cb_ref, lg_ref, lb_ref, yc_ref, tail_ref, q_ref, k_ref, v_ref, buf, shf, wtile = refs[8:]
    else:
        u_ref, q_ref, k_ref, v_ref = refs[8:]
    b0, b1, d = x_ref.shape
    rows = b0 * b1
    h = _norm_mod(x_ref[...], ng_ref[0, 0], sc_ref[0], sh_ref[0])
    hb = h.reshape(rows, d).astype(BF16)
    a_val = jnp.dot(hb, w_ref[0, :, 0:conv_ch], preferred_element_type=F32)
    a_gate = jnp.dot(hb, w_ref[0, :, conv_ch:2 * conv_ch], preferred_element_type=F32)
    u = a_val * jax.nn.sigmoid(a_gate)
    conv_chunks = []
    if fuse_conv:
        halo = CONV_HALO

        @pl.when(pl.program_id(1) == 0)
        def _():
            buf[0:halo, :] = jnp.zeros((halo, conv_ch), F32)

        buf[halo:halo + rows, :] = u
        conv_chunks = _conv_rows(buf, shf, wtile, cw_ref, cb_ref, lg_ref, lb_ref, yc_ref, rows, halo)
    else:
        u_ref[...] = u.reshape(b0, b1, conv_ch)

    def table(t_ref):
        return jnp.broadcast_to(t_ref[...], (b0, b1, LANES)).reshape(rows, LANES)

    ta, tb, tc = table(ta_ref), table(tb_ref), table(tc_ref)
    half = ROT_DIM // 2

    def rope(seg, scale):
        fwd = pltpu.roll(seg, LANES - half, 1)
        bwd = pltpu.roll(seg, half, 1)
        r = seg * ta + fwd * tb + bwd * tc
        return r * scale if scale != 1.0 else r

    blocks = [(o_ref, c, scale) for o_ref, scale in ((q_ref, HEAD_DIM ** -0.5), (k_ref, 1.0), (v_ref, None))
              for c in range(0, att_w, MXU_TILE)]
    c0 = 2 * conv_ch
    done = 0
    for bi, (o_ref, c, scale) in enumerate(blocks):
        wcol = c0 + bi * MXU_TILE
        z = jnp.dot(hb, w_ref[0, :, wcol:wcol + MXU_TILE], preferred_element_type=F32)
        if scale is not None:
            z = jnp.concatenate([rope(z[:, l:l + LANES], scale) for l in range(0, MXU_TILE, LANES)], axis=-1)
        o_ref[:, :, c:c + MXU_TILE] = z.reshape(b0, b1, MXU_TILE)
        upto = (bi + 1) * len(conv_chunks) // len(blocks)
        for chunk in conv_chunks[done:upto]:
            chunk()
        done = upto
    if fuse_conv:
        tail = buf[rows:rows + halo, :]
        tail_ref[0] = tail
        buf[0:halo, :] = tail


def _inproj_call(x, mod, gains, layer, w_in_bf, ei, tabs, b0, b1, conv_ch, att_w, conv=None):
    a, b, d = x.shape
    ta, tb, tc = tabs
    tspec = pl.BlockSpec((1, b1, LANES), lambda i, j: (0, j, 0))
    qkv_shapes = tuple(jax.ShapeDtypeStruct((a, b, att_w), F32) for _ in range(3))
    qkv_specs = tuple(_row_spec(b0, b1, att_w) for _ in range(3))
    in_specs = [_row_spec(b0, b1, d), _mod_spec(mod, b0, d, 0), _mod_spec(mod, b0, d, 1),
                _gain_spec(d, layer, 0), _weight_spec(w_in_bf.shape, ei), tspec, tspec, tspec]
    args = [x, mod[0], mod[0], gains, w_in_bf, ta, tb, tc]
    if conv is None:
        out_shapes = (jax.ShapeDtypeStruct((a, b, conv_ch), F32),) + qkv_shapes
        out_specs = (_row_spec(b0, b1, conv_ch),) + qkv_specs
        scratch, sem = [], ("parallel", "parallel")
    else:
        assert b0 == 1 and b1 % CONV_HALO == 0
        conv_w, conv_b, ln_g, ln_b = conv
        in_specs += [_const_spec((CONV_W, conv_ch))] + [_const_spec((1, conv_ch))] * 3
        args += [conv_w, conv_b.reshape(1, conv_ch), ln_g.reshape(1, conv_ch), ln_b.reshape(1, conv_ch)]
        out_shapes = (jax.ShapeDtypeStruct((a, b, conv_ch), F32),
                      jax.ShapeDtypeStruct((a, CONV_HALO, conv_ch), F32)) + qkv_shapes
        out_specs = (_row_spec(b0, b1, conv_ch),
                     pl.BlockSpec((1, CONV_HALO, conv_ch), lambda i, j: (i, 0, 0))) + qkv_specs
        scratch = [pltpu.VMEM((b1 + CONV_HALO, conv_ch), F32),
                   pltpu.VMEM((SUBLANES - 1, b1 + CONV_HALO - SUBLANES, conv_ch), F32),
                   pltpu.VMEM((CONV_W, SUBLANES, conv_ch), F32)]
        sem = ("parallel", "arbitrary")
    return pl.pallas_call(
        functools.partial(_inproj_kernel, conv_ch=conv_ch, att_w=att_w, fuse_conv=conv is not None),
        grid=(a // b0, b // b1),
        in_specs=in_specs,
        out_specs=out_specs,
        out_shape=out_shapes,
        scratch_shapes=scratch,
        compiler_params=_cparams(sem),
        name="in_proj_conv" if conv is not None else "in_proj",
    )(*args)


def _rope_tables(pos):
    half = ROT_DIM // 2
    inv = jnp.float32(ROPE_THETA) ** (-(2.0 / ROT_DIM) * jnp.arange(half, dtype=F32))
    ang = pos.astype(F32)[:, None] * inv[None, :]
    cos, sin = jnp.cos(ang), jnp.sin(ang)
    n = pos.shape[0]
    ones = jnp.ones((n, HEAD_DIM - ROT_DIM), F32)
    zeros = jnp.zeros((n, HEAD_DIM - ROT_DIM), F32)
    zh = jnp.zeros((n, half), F32)
    ta = jnp.concatenate([cos, cos, ones], axis=-1)
    tb = jnp.concatenate([-sin, zh, zeros], axis=-1)
    tc = jnp.concatenate([zh, sin, zeros], axis=-1)
    rep = LANES // HEAD_DIM
    return tuple(jnp.tile(t, (1, rep))[None] for t in (ta, tb, tc))


def _ln_silu(acc, g, b):
    mu = jnp.mean(acc, axis=-1, keepdims=True)
    xc = acc - mu
    y = xc * lax.rsqrt(jnp.mean(xc * xc, axis=-1, keepdims=True) + EPS)
    return _silu(y * g + b)


def _conv_rows(buf, shf, wtile, w_ref, cb_ref, g_ref, b_ref, o_ref, tt, halo):
    sub = shf.shape[0] + 1
    n8 = shf.shape[1]
    for s in range(1, sub):
        shf[s - 1] = buf[s:s + n8, :]
    off = halo - (CONV_W - 1)
    rb = 32
    c = buf.shape[1]
    for jt in range(CONV_W):
        wtile[jt] = jnp.broadcast_to(w_ref[jt:jt + 1, :], (sub, c))

    def chunk(r0):
        acc = None
        for jt in range(CONV_W):
            s = (off + jt) % sub
            lo = r0 + off + jt - s
            rows = buf[lo:lo + rb, :] if s == 0 else shf[s - 1, lo:lo + rb, :]
            term = rows.reshape(rb // sub, sub, c) * wtile[jt][None]
            acc = term if acc is None else acc + term
        acc = acc.reshape(rb, c)
        o_ref[0, r0:r0 + rb, :] = _ln_silu(acc + cb_ref[...], g_ref[...], b_ref[...])

    return [functools.partial(chunk, r0) for r0 in range(0, tt, rb)]


def _conv_sample_kernel(full_ref, w_ref, cb_ref, g_ref, b_ref, o_ref):
    t_out = o_ref.shape[0]
    for t in range(t_out):
        acc = w_ref[0:1, :] * full_ref[t]
        for jt in range(1, CONV_W):
            acc = acc + w_ref[jt:jt + 1, :] * full_ref[t + jt]
        o_ref[t] = _ln_silu(acc + cb_ref[...], g_ref[...], b_ref[...])


def _conv_sample_call(full_t, conv_w, conv_b, ln_g, ln_b):
    tf, nb, c = full_t.shape
    t_out = tf - (CONV_W - 1)
    nbk = 32
    return pl.pallas_call(
        _conv_sample_kernel,
        grid=(nb // nbk,),
        in_specs=[pl.BlockSpec((tf, nbk, c), lambda i: (0, i, 0)),
                  _const_spec((CONV_W, c)), _const_spec((1, c)), _const_spec((1, c)), _const_spec((1, c))],
        out_specs=pl.BlockSpec((t_out, nbk, c), lambda i: (0, i, 0)),
        out_shape=jax.ShapeDtypeStruct((t_out, nb, c), F32),
        compiler_params=_cparams(("parallel",)),
        name="conv_sample",
    )(full_t, conv_w, conv_b.reshape(1, c), ln_g.reshape(1, c), ln_b.reshape(1, c))


def _merge_groups(os_, ls_):
    m = functools.reduce(jnp.maximum, ls_)
    es = [jnp.exp(l - m) for l in ls_]
    num = functools.reduce(lambda a, b: a + b, [e * o for e, o in zip(es, os_)])
    return num / functools.reduce(lambda a, b: a + b, es)


def _attn_prompt_kernel(*refs, tb):
    in_refs, att_ref, scr = refs[:5 * N_DIL], refs[5 * N_DIL], refs[5 * N_DIL + 1:]
    kbufs, vbufs, (o_scr, l_scr) = scr[:N_DIL], scr[N_DIL:2 * N_DIL], scr[2 * N_DIL:]
    jb = pl.program_id(1)
    bb = BAND_BLOCK
    qi = lax.broadcasted_iota(jnp.int32, (bb, 2 * bb), 0) + bb
    ki = lax.broadcasted_iota(jnp.int32, (bb, 2 * bb), 1)
    dist = qi - ki
    band = (dist >= 0) & (dist <= bb)
    older = ki < bb
    assert LANES == 2 * HEAD_DIM
    head0 = lax.broadcasted_iota(jnp.int32, (bb, LANES), 1) < HEAD_DIM
    ones_kv = jnp.ones((2 * bb, LANES), BF16)
    for g, (win, dil) in enumerate(DIL_GROUPS):
        q_ref, kc_ref, kp_ref, vc_ref, vp_ref = in_refs[5 * g:5 * g + 5]
        kbuf, vbuf = kbufs[g], vbufs[g]
        pb = bb * dil
        kbuf[0:pb, :] = kp_ref[0]
        kbuf[pb:pb + tb, :] = kc_ref[0]
        vbuf[0:pb, :] = vp_ref[0]
        vbuf[pb:pb + tb, :] = vc_ref[0]

        def units(ug, carry, g=g, dil=dil, pb=pb, q_ref=q_ref, kbuf=kbuf, vbuf=vbuf):
            qsels, qus, kus, vus, masks = [], [], [], [], []
            for kk in range(ATTN_UNITS):
                u = ug * ATTN_UNITS + kk
                i = _div_pow2(u, dil)
                r = _mod_pow2(u, dil)
                if dil == 1:
                    start = pl.multiple_of(i * pb, bb)
                    qsel, ksel = pl.ds(start, bb), pl.ds(start, 2 * bb)
                else:
                    start = i * pb + r
                    qsel, ksel = pl.ds(start, bb, stride=dil), pl.ds(start, 2 * bb, stride=dil)
                qsels.append(qsel)
                qus.append(q_ref[0, qsel, :].astype(BF16))
                kus.append(kbuf[ksel, :].astype(BF16))
                vus.append(vbuf[ksel, :].astype(BF16))
                first = jnp.logical_and(jb == 0, i == 0)
                masks.append(jnp.logical_and(band, jnp.logical_not(jnp.logical_and(older, first))))
            chains = [(kk, slice(h * HEAD_DIM, (h + 1) * HEAD_DIM))
                      for kk in range(ATTN_UNITS) for h in range(LANES // HEAD_DIM)]
            nt = (((1,), (1,)), ((), ()))
            ss = [jnp.where(masks[kk], lax.dot_general(qus[kk][:, cs], kus[kk][:, cs], nt,
                                                       preferred_element_type=F32), -jnp.inf) for kk, cs in chains]
            ms = [jnp.max(s, axis=-1, keepdims=True) for s in ss]
            ps = [jnp.exp(s - m).astype(BF16) for s, m in zip(ss, ms)]
            outs = [jnp.dot(p, vus[kk], preferred_element_type=F32) for p, (kk, _) in zip(ps, chains)]
            sums = [jnp.dot(p, ones_kv, preferred_element_type=F32) for p in ps]
            for kk in range(ATTN_UNITS):
                o_cat = jnp.where(head0, outs[2 * kk], outs[2 * kk + 1])
                l_cat = jnp.where(head0, sums[2 * kk], sums[2 * kk + 1])
                m_cat = jnp.where(head0, ms[2 * kk], ms[2 * kk + 1])
                o_scr[g, qsels[kk], :] = o_cat / l_cat
                l_scr[g, qsels[kk], :] = m_cat + jnp.log(l_cat)
            return carry

        lax.fori_loop(0, tb // bb // ATTN_UNITS, units, 0)

    mr = 256
    for r0 in range(0, tb, mr):
        att_ref[0, r0:r0 + mr, :] = _merge_groups([o_scr[g, r0:r0 + mr, :] for g in range(N_DIL)],
                                                  [l_scr[g, r0:r0 + mr, :] for g in range(N_DIL)])


def _attn_prompt_call(q, k, v):
    n, l, att_w = q.shape
    bb = BAND_BLOCK
    tb = bb * max(d for _, d in DIL_GROUPS)
    assert l % tb == 0 and all(w // d == bb for w, d in DIL_GROUPS) and att_w == N_DIL * GRP_W
    hp = GRP_W // LANES
    in_specs, args, kv_scr = [], [], []
    for g, (_, dil) in enumerate(DIL_GROUPS):
        pb = bb * dil
        ratio = tb // pb
        cur = pl.BlockSpec((1, tb, LANES), lambda i, j, h, g=g: (i, j, g * hp + h))
        prev = pl.BlockSpec((1, pb, LANES),
                            lambda i, j, h, g=g, ratio=ratio: (i, jnp.maximum(j * ratio - 1, 0), g * hp + h))
        in_specs += [cur, cur, prev, cur, prev]
        args += [q, k, k, v, v]
        kv_scr.append(pltpu.VMEM((pb + tb, LANES), F32))
    return pl.pallas_call(
        functools.partial(_attn_prompt_kernel, tb=tb),
        grid=(n, l // tb, hp),
        in_specs=in_specs,
        out_specs=pl.BlockSpec((1, tb, LANES), lambda i, j, h: (i, j, h)),
        out_shape=jax.ShapeDtypeStruct((n, l, GRP_W), F32),
        scratch_shapes=kv_scr + kv_scr + [pltpu.VMEM((N_DIL, tb, LANES), F32)] * 2,
        compiler_params=_cparams(("parallel", "arbitrary", "arbitrary")),
        name="attn_prompt",
    )(*args)


def _attn_sample_kernel(q_ref, kn_ref, vn_ref, c0_ref, c1_ref, c2_ref, att_ref):
    t_new = q_ref.shape[1]
    rows = HPG * t_new
    ri = lax.broadcasted_iota(jnp.int32, (rows, GRP_W), 0)
    ci = lax.broadcasted_iota(jnp.int32, (rows, GRP_W), 1)
    head_mask = _div_pow2(ci, HEAD_DIM) == _div_pow2(ri, t_new)
    c_refs = (c0_ref, c1_ref, c2_ref)
    valid = []
    for (win, dil), c_ref in zip(DIL_GROUPS, c_refs):
        wb = c_ref.shape[-1]
        tq = _mod_pow2(lax.broadcasted_iota(jnp.int32, (rows, wb), 0), t_new)
        ki = lax.broadcasted_iota(jnp.int32, (rows, wb), 1)
        dist = wb + tq - ki
        tqn = _mod_pow2(lax.broadcasted_iota(jnp.int32, (rows, t_new), 0), t_new)
        jn = lax.broadcasted_iota(jnp.int32, (rows, t_new), 1)
        dn = tqn - jn
        valid.append(((dist >= 0) & (dist <= win) & (_mod_pow2(dist, dil) == 0),
                      (dn >= 0) & (dn <= win) & (_mod_pow2(dn, dil) == 0)))
    nns = range(q_ref.shape[0])
    os_, ls_ = [[] for _ in nns], [[] for _ in nns]
    for gi, c_ref in enumerate(c_refs):
        cs = slice(gi * GRP_W, (gi + 1) * GRP_W)
        valid_c, valid_n = valid[gi]
        qbs = [jnp.where(head_mask, jnp.concatenate([q_ref[nn, :, cs]] * HPG, axis=0), 0.0) for nn in nns]
        s_cs = [jnp.where(valid_c, _bdot(qbs[nn], c_ref[0, nn, 0]), -jnp.inf) for nn in nns]
        s_ns = [jnp.where(valid_n, _bdot_nt(qbs[nn], kn_ref[nn, :, cs]), -jnp.inf) for nn in nns]
        ms = [jnp.maximum(jnp.max(s_cs[nn], axis=-1, keepdims=True), jnp.max(s_ns[nn], axis=-1, keepdims=True))
              for nn in nns]
        p_cs = [jnp.exp(s_cs[nn] - ms[nn]) for nn in nns]
        p_ns = [jnp.exp(s_ns[nn] - ms[nn]) for nn in nns]
        lsums = [jnp.sum(p_cs[nn], axis=-1, keepdims=True) + jnp.sum(p_ns[nn], axis=-1, keepdims=True)
                 for nn in nns]
        o_fulls = [(_bdot_nt(p_cs[nn], c_ref[0, nn, 1]) + _bdot(p_ns[nn], vn_ref[nn, :, cs])) / lsums[nn]
                   for nn in nns]
        for nn in nns:
            o_full = jnp.where(head_mask, o_fulls[nn], 0.0)
            lse_full = jnp.where(head_mask, jnp.broadcast_to(ms[nn] + jnp.log(lsums[nn]), (rows, GRP_W)), 0.0)
            o = o_full[0:t_new]
            lse = lse_full[0:t_new]
            for h in range(1, HPG):
                o = o + o_full[h * t_new:(h + 1) * t_new]
                lse = lse + lse_full[h * t_new:(h + 1) * t_new]
            os_[nn].append(o)
            ls_[nn].append(lse)
    for nn in nns:
        att_ref[nn] = _merge_groups(os_[nn], ls_[nn])


def _cache_rows_on_lanes(cache):
    ly, nb, wb = cache.shape[:3]
    return jnp.transpose(cache, (0, 1, 3, 4, 5, 2)).reshape(ly, nb, 2, GRP_W, wb)


def _attn_sample_call(q, k, v, caches_t, ei):
    nb, t_new, att_w = q.shape
    assert all(c.shape[-1] == w for c, (w, _) in zip(caches_t, DIL_GROUPS))
    n_blk = 2 if nb % 2 == 0 else 1
    qspec = pl.BlockSpec((n_blk, t_new, att_w), lambda i: (i, 0, 0))
    cspecs = [pl.BlockSpec((1, n_blk) + c.shape[2:], lambda i: (ei, i, 0, 0, 0)) for c in caches_t]
    return pl.pallas_call(
        _attn_sample_kernel,
        grid=(nb // n_blk,),
        in_specs=[qspec, qspec, qspec] + cspecs,
        out_specs=pl.BlockSpec((n_blk, t_new, GRP_W), lambda i: (i, 0, 0)),
        out_shape=jax.ShapeDtypeStruct((nb, t_new, GRP_W), F32),
        compiler_params=_cparams(("parallel",)),
        name="attn_sample",
    )(q, k, v, *caches_t)


def _ffn_kernel(*refs, fchunks, final, mix):
    x_ref, sh_ref, sc_ref, g2_ref, ng_ref, wg_ref, wu_ref, wd_ref, fg_ref = refs[:9]
    o_ref = refs[-1]
    b0, b1, d = x_ref.shape
    rows = b0 * b1
    x = x_ref[...]
    if mix:
        g1_ref, yc_ref, att_ref, wa_ref, wb_ref = refs[9:14]
        mixed = _bdot(yc_ref[...].reshape(rows, -1), wa_ref[0]) + _bdot(att_ref[...].reshape(rows, -1), wb_ref[0])
        x = x + g1_ref[0] * mixed.reshape(b0, b1, d)
    hb = _norm_mod(x, ng_ref[0, 0], sc_ref[0], sh_ref[0]).reshape(rows, d).astype(BF16)
    dff = wg_ref.shape[2]
    acc = None
    for f0, fchunk in fchunks:
        g = jnp.dot(hb, wg_ref[0, :, f0:f0 + fchunk], preferred_element_type=F32)
        u = jnp.dot(hb, wu_ref[0, :, f0:f0 + fchunk], preferred_element_type=F32)
        part = jnp.dot((_silu(g) * u).astype(BF16), wd_ref[0, f0:f0 + fchunk, :], preferred_element_type=F32)
        acc = part if acc is None else acc + part
    xn = x + g2_ref[0] * acc.reshape(b0, b1, d)
    if final:
        xn = xn * lax.rsqrt(jnp.mean(xn * xn, axis=-1, keepdims=True) + EPS) * fg_ref[...]
    o_ref[...] = xn


def _ffn_call(x, mod, gains, layer, wg_bf, wu_bf, wd_bf, final_g, b0, b1, final, mix=None):
    a, b, d = x.shape
    dff = wg_bf.shape[2]
    in_specs = [_row_spec(b0, b1, d), _mod_spec(mod, b0, d, 3), _mod_spec(mod, b0, d, 4), _mod_spec(mod, b0, d, 5),
                _gain_spec(d, layer, 1), _weight_spec(wg_bf.shape, layer), _weight_spec(wu_bf.shape, layer),
                _weight_spec(wd_bf.shape, layer), _const_spec((1, d))]
    args = [x, mod[0], mod[0], mod[0], gains, wg_bf, wu_bf, wd_bf, final_g.reshape(1, d)]
    if mix is not None:
        yc, att, wa_bf, wb_bf, ei = mix
        in_specs += [_mod_spec(mod, b0, d, 2), _row_spec(b0, b1, yc.shape[-1]), _row_spec(b0, b1, att.shape[-1]),
                     _weight_spec(wa_bf.shape, ei), _weight_spec(wb_bf.shape, ei)]
        args += [mod[0], yc, att, wa_bf, wb_bf]
    tiles = -(-dff // MXU_TILE)
    f_split = min(dff, -(-tiles // 2) * MXU_TILE)
    fchunks = tuple((f0, f1 - f0) for f0, f1 in ((0, f_split), (f_split, dff)) if f1 > f0)
    return pl.pallas_call(
        functools.partial(_ffn_kernel, fchunks=fchunks, final=final, mix=mix is not None),
        grid=(a // b0, b // b1),
        in_specs=in_specs,
        out_specs=_row_spec(b0, b1, d),
        out_shape=jax.ShapeDtypeStruct((a, b, d), F32),
        compiler_params=_cparams(("parallel", "parallel")),
        name=("mix_" if mix is not None else "") + ("ffn_final" if final else "ffn"),
    )(*args)


def _s5_param_kernel(lr_ref, li_ref, ldt_ref, br_ref, bi_ref, abr_ref, abi_ref, bbr_ref, bbi_ref,
                     sbr_ref, sbi_ref):
    lr, li = lr_ref[0], li_ref[0]
    dt = jnp.exp(ldt_ref[0])
    mag = jnp.exp(lr * dt)
    ph = li * dt
    abr = mag * jnp.cos(ph)
    abi = mag * jnp.sin(ph)
    den = lr * lr + li * li
    nr = abr - 1.0
    fr = (nr * lr + abi * li) / den
    fi = (abi * lr - nr * li) / den
    abr_ref[0] = abr
    abi_ref[0] = abi
    for h in range(br_ref.shape[1]):
        br, bi = br_ref[0, h], bi_ref[0, h]
        bbr = fr * br - fi * bi
        bbi = fr * bi + fi * br
        bbr_ref[0, h] = bbr
        bbi_ref[0, h] = bbi
        sbr_ref[0, h] = abr * bbr - abi * bbi
        sbi_ref[0, h] = abr * bbi + abi * bbr


def _s5_param_call(lam_re, lam_im, log_dt, b_re, b_im):
    n_odd, g, p = lam_re.shape
    hh = b_re.shape[-1]
    bt_re = jnp.transpose(b_re, (0, 3, 1, 2))
    bt_im = jnp.transpose(b_im, (0, 3, 1, 2))
    gp = pl.BlockSpec((1, g, p), lambda l: (l, 0, 0))
    bs = pl.BlockSpec((1, hh, g, p), lambda l: (l, 0, 0, 0))
    return pl.pallas_call(
        _s5_param_kernel,
        grid=(n_odd,),
        in_specs=[gp, gp, pl.BlockSpec((1, g, 1), lambda l: (l, 0, 0)), bs, bs],
        out_specs=(gp, gp, bs, bs, bs, bs),
        out_shape=(jax.ShapeDtypeStruct((n_odd, g, p), F32),) * 2
        + (jax.ShapeDtypeStruct((n_odd, hh, g, p), F32),) * 4,
        compiler_params=_cparams(("arbitrary",)),
        name="s5_params",
    )(lam_re, lam_im, log_dt.reshape(n_odd, g, 1), bt_re, bt_im)


def _s5_block_weights(bbr, bbi, sbr, sbi, c_re, c_im):
    hh, g, p = bbr.shape
    gl = g // S5_JBLK
    eye = jnp.eye(gl, dtype=F32)
    gh = LANES // hh
    nhb = g // gh
    eye_h = jnp.eye(gh, dtype=F32)

    def bblk(t):
        t = t.reshape(hh, nhb, gh, p).transpose(1, 2, 0, 3)
        t = t[:, :, :, None, :] * eye_h[None, :, None, :, None]
        return t.reshape(nhb, gh * hh, gh * p)

    def cblk(t):
        t = t.reshape(S5_JBLK, gl, hh, p).transpose(0, 1, 3, 2)
        t = t[:, :, :, None, :] * eye[None, :, None, :, None]
        return t.reshape(S5_JBLK, gl * p, gl * hh)

    bb = jnp.concatenate([bblk(bbr), bblk(bbi)], axis=-1).astype(BF16)
    sb = jnp.concatenate([bblk(sbr), bblk(sbi)], axis=-1).astype(BF16)
    cc = jnp.concatenate([cblk(c_re), cblk(-c_im)], axis=1).astype(BF16)
    return bb, jnp.concatenate([bb, sb], axis=1), cc


def _gelu_tanh(x):
    return 0.5 * x * (1.0 + jnp.tanh(math.sqrt(2.0 / math.pi) * (x + 0.044715 * (x * x * x))))


_S5_TILE_GROUPS = 8


def _s5_kernel(x_ref, sh_ref, sc_ref, g1_ref, ng_ref, s0r_ref, s0i_ref, abr_ref, abi_ref, perm_ref, bb_ref, cc_ref,
               dsk_ref, wglu_ref, bglu_ref, xo_ref, sro_ref, sio_ref,
               st_r, st_i, bu_scr, h_scr, yp_scr, y_scr, *, lw, unroll):
    c = pl.program_id(1)
    nbk, tc, d = x_ref.shape
    rows = nbk * tc
    jw = d // S5_JBLK
    hw = bb_ref.shape[2] // 2
    nhalf = jw // LANES
    sw = nhalf * hw
    w = lw * LANES
    sub = 8

    two_step = nbk % sub != 0
    if two_step:
        assert 2 * nbk == sub and tc % 2 == 0 and bb_ref.shape[1] == 2 * LANES

    @pl.when(c == 0)
    def _():
        if two_step:
            st_r[0:nbk, :] = jnp.zeros((nbk, st_r.shape[1]), F32)
            st_i[0:nbk, :] = jnp.zeros((nbk, st_i.shape[1]), F32)
            st_r[nbk:sub, :] = s0r_ref[...]
            st_i[nbk:sub, :] = s0i_ref[...]
        else:
            st_r[...] = s0r_ref[...]
            st_i[...] = s0i_ref[...]

    x = x_ref[...]
    u2 = _norm_mod(x, ng_ref[0, 0], sc_ref[0], sh_ref[0]).reshape(rows, d)
    ub = u2.astype(BF16)
    up = jnp.dot(perm_ref[0], ub, preferred_element_type=F32).astype(BF16)
    if two_step:
        ups = jnp.dot(perm_ref[1], ub, preferred_element_type=F32).astype(BF16)

    nblk = 2 * sw // MXU_TILE
    tph = hw // MXU_TILE
    lhs_cache, y_acc = {}, {}

    def project_in(j, cb):
        hb, part, tile = cb // (2 * tph), cb // tph % 2, cb % tph
        if (j, hb) not in lhs_cache:
            ucols = slice(j * jw + hb * LANES, j * jw + (hb + 1) * LANES)
            lhs = up[:, ucols]
            if two_step:
                lhs = jnp.concatenate([lhs, ups[:, ucols]], axis=1)
            lhs_cache[(j, hb)] = lhs
        wcols = slice(part * hw + tile * MXU_TILE, part * hw + (tile + 1) * MXU_TILE)
        dst = part * sw + hb * hw + tile * MXU_TILE
        bu_scr[j % 2, :, dst:dst + MXU_TILE] = jnp.dot(lhs_cache[(j, hb)], bb_ref[nhalf * j + hb, :, wcols],
                                                       preferred_element_type=F32)

    def project_out(j, kb):
        ks = slice(kb * MXU_TILE, (kb + 1) * MXU_TILE)
        part = jnp.dot(h_scr[j % 2, :, ks].astype(BF16), cc_ref[j, ks, :], preferred_element_type=F32)
        y_acc[j] = part if kb == 0 else y_acc[j] + part
        if kb == nblk - 1:
            for k in range(jw // LANES):
                yp_scr[j * (jw // LANES) + k] = y_acc[j][:, k * LANES:(k + 1) * LANES]

    for cb in range(nblk):
        project_in(0, cb)
    for j in range(S5_JBLK + 1):
        tasks = []
        for b in range(nblk):
            if j + 1 < S5_JBLK:
                tasks.append(functools.partial(project_in, j + 1, b))
            if j > 0:
                tasks.append(functools.partial(project_out, j - 1, b))
        if j == S5_JBLK:
            for task in tasks:
                task()
            break
        slots = (sw // w) * _S5_TILE_GROUPS
        slot = [0]

        def run_tasks(tasks=tasks, slot=slot, slots=slots):
            slot[0] += 1
            upto = slot[0] * len(tasks) // slots
            start = (slot[0] - 1) * len(tasks) // slots
            for task in tasks[start:upto]:
                task()

        bu, hs = bu_scr.at[j % 2], h_scr.at[j % 2]
        for l0 in range(0, sw, w):
            c_re, c_im = slice(l0, l0 + w), slice(sw + l0, sw + l0 + w)
            c_st = slice(j * sw + l0, j * sw + l0 + w)
            if nbk % sub == 0:
                ar = jnp.broadcast_to(abr_ref[:, c_st], (nbk, w))
                ai = jnp.broadcast_to(abi_ref[:, c_st], (nbk, w))

                def step(t, carry, c_re=c_re, c_im=c_im, ar=ar, ai=ai):
                    hr, hi = carry
                    rs = pl.ds(pl.multiple_of(t * nbk, nbk), nbk)
                    nhr = ar * hr - ai * hi + bu[rs, c_re]
                    nhi = ar * hi + ai * hr + bu[rs, c_im]
                    hs[rs, c_re] = nhr
                    hs[rs, c_im] = nhi
                    return nhr, nhi

                hr, hi = lax.fori_loop(0, tc, step, (st_r[:, c_st], st_i[:, c_st]), unroll=unroll)
                st_r[:, c_st] = hr
                st_i[:, c_st] = hi
                for _ in range(_S5_TILE_GROUPS):
                    run_tasks()
            else:
                ar = jnp.broadcast_to(abr_ref[:, c_st], (sub, w))
                ai = jnp.broadcast_to(abi_ref[:, c_st], (sub, w))
                a2r, a2i = ar * ar - ai * ai, 2.0 * (ar * ai)
                lower = lax.broadcasted_iota(jnp.int32, (sub, w), 0) < nbk
                pr, pi = st_r[:, c_st], st_i[:, c_st]
                xr, xi = bu[0:sub, c_re], bu[0:sub, c_im]
                qr, qi = pltpu.roll(pr, nbk, 0), pltpu.roll(pi, nbk, 0)
                hr = jnp.where(lower, ar * qr - ai * qi, a2r * pr - a2i * pi) + xr
                hi = jnp.where(lower, ar * qi + ai * qr, a2r * pi + a2i * pr) + xi
                hs[0:sub, c_re] = hr
                hs[0:sub, c_im] = hi

                sr, si = hr, hi
                ntile = tc // 2
                for grp in range(_S5_TILE_GROUPS):
                    for i in range(max(1, grp * ntile // _S5_TILE_GROUPS), (grp + 1) * ntile // _S5_TILE_GROUPS):
                        rs = slice(i * sub, (i + 1) * sub)
                        sr, si = (a2r * sr - a2i * si + bu[rs, c_re], a2r * si + a2i * sr + bu[rs, c_im])
                        hs[rs, c_re] = sr
                        hs[rs, c_im] = si
                    run_tasks()
                st_r[:, c_st] = sr
                st_i[:, c_st] = si

    nlc = d // LANES
    for k in range(nlc):
        if nbk <= tc:
            for n in range(nbk):
                y_scr[k, n * tc:(n + 1) * tc, :] = yp_scr[k, pl.ds(n, tc, stride=nbk), :]
        else:
            for t in range(tc):
                y_scr[k, pl.ds(t, nbk, stride=tc), :] = yp_scr[k, t * nbk:(t + 1) * nbk, :]
    y = jnp.concatenate([y_scr[k] for k in range(nlc)], axis=-1) + dsk_ref[...] * u2
    g = _bdot(_gelu_tanh(y), wglu_ref[0]) + bglu_ref[...]
    out = g[:, 0:d] * jax.nn.sigmoid(g[:, d:2 * d])
    xo_ref[...] = x + g1_ref[0] * out.reshape(nbk, tc, d)

    @pl.when(c == pl.num_programs(1) - 1)
    def _():
        if two_step:
            sro_ref[...] = st_r[nbk:sub, :]
            sio_ref[...] = st_i[nbk:sub, :]
        else:
            sro_ref[...] = st_r[...]
            sio_ref[...] = st_i[...]


def _s5_call(x, mod, gains, layer, s0r, s0i, abr, abi, bb, cc, d_skip, wglu_bf, oi, b_glu, nbk, tc, lw, unroll):
    a, b, d = x.shape
    ns = s0r.shape[1]
    rows = nbk * tc
    dst = jnp.arange(rows)
    src = (dst % nbk) * tc + dst // nbk
    col = jnp.arange(rows)[None, :]
    perm = jnp.stack([src[:, None] == col,
                      jnp.logical_and(src[:, None] - 1 == col, (dst // nbk > 0)[:, None])]).astype(BF16)
    st_rows = nbk if nbk % 8 == 0 else 2 * nbk
    sspec = pl.BlockSpec((nbk, ns), lambda i, j: (i, 0))
    return pl.pallas_call(
        functools.partial(_s5_kernel, lw=lw, unroll=unroll),
        grid=(a // nbk, b // tc),
        in_specs=[_row_spec(nbk, tc, d), _mod_spec(mod, nbk, d, 0), _mod_spec(mod, nbk, d, 1),
                  _mod_spec(mod, nbk, d, 2), _gain_spec(d, layer, 0), sspec, sspec,
                  _const_spec((1, ns)), _const_spec((1, ns)),
                  _weight_spec(perm.shape), _weight_spec(bb.shape), _weight_spec(cc.shape), _const_spec((1, d)),
                  _weight_spec(wglu_bf.shape, oi), _const_spec((1, 2 * d))],
        out_specs=(_row_spec(nbk, tc, d), sspec, sspec),
        out_shape=(jax.ShapeDtypeStruct((a, b, d), F32),
                   jax.ShapeDtypeStruct((a, ns), F32), jax.ShapeDtypeStruct((a, ns), F32)),
        scratch_shapes=[pltpu.VMEM((st_rows, ns), F32), pltpu.VMEM((st_rows, ns), F32),
                        pltpu.VMEM((2, rows, 2 * ns // S5_JBLK), F32), pltpu.VMEM((2, rows, 2 * ns // S5_JBLK), F32),
                        pltpu.VMEM((d // LANES, rows, LANES), F32), pltpu.VMEM((d // LANES, rows, LANES), F32)],
        compiler_params=_cparams(("parallel", "arbitrary")),
        name="s5_mixer",
    )(x, mod[0], mod[0], mod[0], gains, s0r, s0i, abr.reshape(1, ns), abi.reshape(1, ns), perm, bb, cc,
      d_skip.reshape(1, d), wglu_bf, b_glu.reshape(1, 2 * d))


def _run_trunk(x, mods, tabs, conv_st, kv_st, s5_st, W, b0, b1, s5_tiles):
    a, b, d = x.shape
    depth = mods[0].shape[0]
    conv_ch = W['conv_w'].shape[-1]
    att_w = N_DIL * GRP_W
    new_conv, new_s5 = [], []
    new_kv = [[] for _ in range(N_DIL)]
    ei = oi = 0
    for layer in range(depth):
        mod = (mods[0], layer, mods[1])
        mix = None
        if layer % 2 == 0:
            cw = (W['conv_w'][ei], W['conv_b'][ei], W['conv_ln_g'][ei], W['conv_ln_b'][ei])
            if conv_st is None:
                yc, u_tail, q, k, v = _inproj_call(x, mod, W['gains'], layer, W['w_in_bf'], ei, tabs, b0, b1,
                                                   conv_ch, att_w, conv=cw)
                new_conv.append(u_tail[:, CONV_HALO - (CONV_W - 1):])
                att = _attn_prompt_call(q, k, v)
                for gi, (win, _) in enumerate(DIL_GROUPS):
                    keep = min(win, b)
                    kg = k[:, b - keep:, gi * GRP_W:(gi + 1) * GRP_W].reshape(a, keep, HPG, HEAD_DIM)
                    vg = v[:, b - keep:, gi * GRP_W:(gi + 1) * GRP_W].reshape(a, keep, HPG, HEAD_DIM)
                    new_kv[gi].append(jnp.stack([kg, vg], axis=2))
            else:
                u, q, k, v = _inproj_call(x, mod, W['gains'], layer, W['w_in_bf'], ei, tabs, b0, b1, conv_ch, att_w)
                full = jnp.concatenate([conv_st[ei], u], axis=1)
                yc = jnp.transpose(_conv_sample_call(jnp.transpose(full, (1, 0, 2)), *cw), (1, 0, 2))
                new_conv.append(full[:, -(CONV_W - 1):])
                att = _attn_sample_call(q, k, v, kv_st, ei)
                for gi in range(N_DIL):
                    kg = k[:, :, gi * GRP_W:(gi + 1) * GRP_W].reshape(a, b, HPG, HEAD_DIM)
                    vg = v[:, :, gi * GRP_W:(gi + 1) * GRP_W].reshape(a, b, HPG, HEAD_DIM)
                    new_kv[gi].append(jnp.stack([kg, vg], axis=2))
            mix = (yc, att, W['w_o_a_bf'], W['w_o_b_bf'], ei)
            ei += 1
        else:
            ns = W['abr'].shape[1] * W['abr'].shape[2]
            if s5_st is None:
                s0r = jnp.zeros((a, ns), F32)
                s0i = jnp.zeros((a, ns), F32)
            else:
                s0r = s5_st[oi][..., 0].reshape(a, ns)
                s0i = s5_st[oi][..., 1].reshape(a, ns)
            nbk, tc, lw, unroll = s5_tiles
            bb = W['bb'][oi] if nbk % 8 == 0 else W['bb2'][oi]
            x, sr, si = _s5_call(x, mod, W['gains'], layer, s0r, s0i, W['abr'][oi], W['abi'][oi],
                                 bb, W['cc'][oi], W['s5_d'][oi], W['s5_w_glu_bf'], oi,
                                 W['s5_b_glu'][oi], nbk, tc, lw, unroll)
            g, p = W['abr'].shape[1:]
            new_s5.append(jnp.stack([sr.reshape(a, g, p), si.reshape(a, g, p)], axis=-1))
            oi += 1
        x = _ffn_call(x, mod, W['gains'], layer, W['w_ff_gate_bf'], W['w_ff_up_bf'],
                      W['w_ff_down_bf'], W['final_g'], b0, b1, final=(layer == depth - 1), mix=mix)
    return x, jnp.stack(new_conv), [jnp.stack(kv) for kv in new_kv], jnp.stack(new_s5)


def kernel(x_prompt, x_sample, cache_conv, cache_kv_g0, cache_kv_g1, cache_kv_g2, state_s5, c_prompt, c_sample, norm_g, final_g, w_ada, b_ada, w_in, conv_w, conv_b, conv_ln_g, conv_ln_b, w_o, s5_lam_re, s5_lam_im, s5_log_dt, s5_b_re, s5_b_im, s5_c_re, s5_c_im, s5_d, s5_w_glu, s5_b_glu, w_ff_gate, w_ff_up, w_ff_down):
    n, l, d = x_prompt.shape
    nb, t_new, _ = x_sample.shape
    depth = w_ada.shape[0]
    conv_ch = conv_w.shape[-1]

    m_all = n + nb
    m_pad = -(-m_all // 8) * 8
    c_all = jnp.concatenate([c_sample, c_prompt, jnp.zeros((m_pad - m_all, d), F32)], axis=0)
    mod4 = _ada_call(c_all, w_ada, b_ada).reshape(depth, m_pad, 1, 6 * d)

    abr, abi, bbr, bbi, sbr, sbi = _s5_param_call(s5_lam_re, s5_lam_im, s5_log_dt, s5_b_re, s5_b_im)
    blocks = [_s5_block_weights(bbr[i], bbi[i], sbr[i], sbi[i], s5_c_re[i], s5_c_im[i])
              for i in range(abr.shape[0])]

    W = dict(gains=norm_g.reshape(depth, 2, 1, d), final_g=final_g, conv_w=conv_w, conv_b=conv_b, conv_ln_g=conv_ln_g,
             conv_ln_b=conv_ln_b, s5_d=s5_d, s5_b_glu=s5_b_glu, abr=abr, abi=abi,
             bb=[b[0] for b in blocks], bb2=[b[1] for b in blocks], cc=[b[2] for b in blocks],
             w_in_bf=w_in.astype(BF16), w_o_a_bf=w_o[:, :conv_ch].astype(BF16),
             w_o_b_bf=w_o[:, conv_ch:].astype(BF16), s5_w_glu_bf=s5_w_glu.astype(BF16),
             w_ff_gate_bf=w_ff_gate.astype(BF16), w_ff_up_bf=w_ff_up.astype(BF16),
             w_ff_down_bf=w_ff_down.astype(BF16))

    tabs_p = _rope_tables(jnp.arange(l))
    tabs_s = _rope_tables(PAST_LEN + jnp.arange(t_new))

    tm = min(512, l)
    y_p, conv_p, kvs_p, s5_p = _run_trunk(x_prompt, (mod4, nb), tabs_p, None, None, None, W,
                                          1, tm, (n, 128, 8, True))
    nbk = min(nb, tm // t_new)
    caches_t = [_cache_rows_on_lanes(c) for c in (cache_kv_g0, cache_kv_g1, cache_kv_g2)]
    y_s, conv_s, kvs_s, s5_s = _run_trunk(x_sample, (mod4, 0), tabs_s, cache_conv, caches_t, state_s5, W,
                                          nbk, t_new, (nbk, t_new, 1, True))
    return (y_p, y_s, conv_p, kvs_p[0], kvs_p[1], kvs_p[2], s5_p,
            conv_s, kvs_s[0], kvs_s[1], kvs_s[2], s5_s)
```

```python
import functools
import math

import jax
import jax.numpy as jnp
from jax import lax
from jax.experimental import pallas as pl
from jax.experimental.pallas import tpu as pltpu

F32 = jnp.float32
BF16 = jnp.bfloat16

EPS = 1e-6
CONV_W = 31
HEAD_DIM = 64
HPG = 4
DIL_GROUPS = ((128, 1), (512, 4), (2048, 16))
N_DIL = len(DIL_GROUPS)
GRP_W = HPG * HEAD_DIM
ROT_DIM = HEAD_DIM // 4
ROPE_THETA = 500000.0
BAND_BLOCK = 128
S5_GROUP = 16
S5_STATE = 64
S5_JBLK = 4
PAST_LEN = 2048
LANES = 128
MXU_TILE = 256
SUBLANES = 8
ATTN_UNITS = 4
CONV_HALO = 32
VMEM_LIMIT = 56 * 1024 * 1024


def _cparams(sem):
    return pltpu.CompilerParams(dimension_semantics=sem, vmem_limit_bytes=VMEM_LIMIT)


def _const_spec(shape):
    nd = len(shape)
    return pl.BlockSpec(shape, lambda *_: (0,) * nd)


def _weight_spec(shape, layer=None):
    nd = len(shape)
    if layer is None:
        return pl.BlockSpec(shape, lambda *_: (0,) * nd, pipeline_mode=pl.Buffered(1))
    return pl.BlockSpec((1,) + tuple(shape[1:]), lambda *_: (layer,) + (0,) * (nd - 1),
                        pipeline_mode=pl.Buffered(1))


def _row_spec(b0, b1, cols):
    return pl.BlockSpec((b0, b1, cols), lambda i, j: (i, j, 0))


def _mod_spec(mod, b0, d, k):
    _, layer, row0 = mod
    assert row0 % b0 == 0
    return pl.BlockSpec((1, b0, 1, d), lambda i, j: (layer, row0 // b0 + i, 0, k))


def _gain_spec(d, layer, which):
    return pl.BlockSpec((1, 1, 1, d), lambda *_: (layer, which, 0, 0))


def _norm_mod(x, g, sc, sh):
    y = x * lax.rsqrt(jnp.mean(x * x, axis=-1, keepdims=True) + EPS)
    return (y * g) * (1.0 + sc) + sh


def _silu(x):
    return x * jax.nn.sigmoid(x)


def _log2(n):
    assert n > 0 and n & (n - 1) == 0, n
    return n.bit_length() - 1


def _div_pow2(x, n):
    return x >> _log2(n)


def _mod_pow2(x, n):
    return x & ((1 << _log2(n)) - 1)


def _bdot(a, b):
    return jnp.dot(a.astype(BF16), b.astype(BF16), preferred_element_type=F32)


def _bdot_nt(a, b):
    return lax.dot_general(a.astype(BF16), b.astype(BF16), (((1,), (1,)), ((), ())),
                           preferred_element_type=F32)


def _ada_kernel(c_ref, w_ref, b_ref, o_ref):
    o_ref[0] = _bdot(_silu(c_ref[...]), w_ref[0]) + b_ref[0]


def _ada_call(c_all, w_ada, b_ada):
    depth, d, d6 = w_ada.shape
    m = c_all.shape[0]
    tn = 1536
    return pl.pallas_call(
        _ada_kernel,
        grid=(depth, d6 // tn),
        in_specs=[pl.BlockSpec((m, d), lambda l, j: (0, 0)),
                  pl.BlockSpec((1, d, tn), lambda l, j: (l, 0, j)),
                  pl.BlockSpec((1, 1, tn), lambda l, j: (l, 0, j))],
        out_specs=pl.BlockSpec((1, m, tn), lambda l, j: (l, 0, j)),
        out_shape=jax.ShapeDtypeStruct((depth, m, d6), F32),
        compiler_params=_cparams(("arbitrary", "arbitrary")),
        name="ada_mod",
    )(c_all, w_ada, b_ada.reshape(depth, 1, d6))


def _inproj_kernel(*refs, conv_ch, att_w, fuse_conv):
    x_ref, sh_ref, sc_ref, ng_ref, w_ref, ta_ref, tb_ref, tc_ref = refs[:8]
    if fuse_conv:
        cw_ref, cb_ref, lg_ref, lb_ref, yc_ref, tail_ref, q_ref, k_ref, v_ref, buf, shf, wtile = refs[8:]
    else:
        u_ref, q_ref, k_ref, v_ref = refs[8:]
    b0, b1, d = x_ref.shape
    rows = b0 * b1
    h = _norm_mod(x_ref[...], ng_ref[0, 0], sc_ref[0], sh_ref[0])
    hb = h.reshape(rows, d).astype(BF16)
    a_val = jnp.dot(hb, w_ref[0, :, 0:conv_ch], preferred_element_type=F32)
    a_gate = jnp.dot(hb, w_ref[0, :, conv_ch:2 * conv_ch], preferred_element_type=F32)
    u = a_val * jax.nn.sigmoid(a_gate)
    conv_chunks = []
    if fuse_conv:
        halo = CONV_HALO

        @pl.when(pl.program_id(1) == 0)
        def _():
            buf[0:halo, :] = jnp.zeros((halo, conv_ch), F32)

        buf[halo:halo + rows, :] = u
        conv_chunks = _conv_rows(buf, shf, wtile, cw_ref, cb_ref, lg_ref, lb_ref, yc_ref, rows, halo)
    else:
        u_ref[...] = u.reshape(b0, b1, conv_ch)

    def table(t_ref):
        return jnp.broadcast_to(t_ref[...], (b0, b1, LANES)).reshape(rows, LANES)

    ta, tb, tc = table(ta_ref), table(tb_ref), table(tc_ref)
    half = ROT_DIM // 2

    def rope(seg, scale):
        fwd = pltpu.roll(seg, LANES - half, 1)
        bwd = pltpu.roll(seg, half, 1)
        r = seg * ta + fwd * tb + bwd * tc
        return r * scale if scale != 1.0 else r

    blocks = [(o_ref, c, scale) for o_ref, scale in ((q_ref, HEAD_DIM ** -0.5), (k_ref, 1.0), (v_ref, None))
              for c in range(0, att_w, MXU_TILE)]
    c0 = 2 * conv_ch
    done = 0
    for bi, (o_ref, c, scale) in enumerate(blocks):
        wcol = c0 + bi * MXU_TILE
        z = jnp.dot(hb, w_ref[0, :, wcol:wcol + MXU_TILE], preferred_element_type=F32)
        if scale is not None:
            z = jnp.concatenate([rope(z[:, l:l + LANES], scale) for l in range(0, MXU_TILE, LANES)], axis=-1)
        o_ref[:, :, c:c + MXU_TILE] = z.reshape(b0, b1, MXU_TILE)
        upto = (bi + 1) * len(conv_chunks) // len(blocks)
        for chunk in conv_chunks[done:upto]:
            chunk()
        done = upto
    if fuse_conv:
        tail = buf[rows:rows + halo, :]
        tail_ref[0] = tail
        buf[0:halo, :] = tail


def _inproj_call(x, mod, gains, layer, w_in_bf, ei, tabs, b0, b1, conv_ch, att_w, conv=None):
    a, b, d = x.shape
    ta, tb, tc = tabs
    tspec = pl.BlockSpec((1, b1, LANES), lambda i, j: (0, j, 0))
    qkv_shapes = tuple(jax.ShapeDtypeStruct((a, b, att_w), F32) for _ in range(3))
    qkv_specs = tuple(_row_spec(b0, b1, att_w) for _ in range(3))
    in_specs = [_row_spec(b0, b1, d), _mod_spec(mod, b0, d, 0), _mod_spec(mod, b0, d, 1),
                _gain_spec(d, layer, 0), _weight_spec(w_in_bf.shape, ei), tspec, tspec, tspec]
    args = [x, mod[0], mod[0], gains, w_in_bf, ta, tb, tc]
    if conv is None:
        out_shapes = (jax.ShapeDtypeStruct((a, b, conv_ch), F32),) + qkv_shapes
        out_specs = (_row_spec(b0, b1, conv_ch),) + qkv_specs
        scratch, sem = [], ("parallel", "parallel")
    else:
        assert b0 == 1 and b1 % CONV_HALO == 0
        conv_w, conv_b, ln_g, ln_b = conv
        in_specs += [_const_spec((CONV_W, conv_ch))] + [_const_spec((1, conv_ch))] * 3
        args += [conv_w, conv_b.reshape(1, conv_ch), ln_g.reshape(1, conv_ch), ln_b.reshape(1, conv_ch)]
        out_shapes = (jax.ShapeDtypeStruct((a, b, conv_ch), F32),
                      jax.ShapeDtypeStruct((a, CONV_HALO, conv_ch), F32)) + qkv_shapes
        out_specs = (_row_spec(b0, b1, conv_ch),
                     pl.BlockSpec((1, CONV_HALO, conv_ch), lambda i, j: (i, 0, 0))) + qkv_specs
        scratch = [pltpu.VMEM((b1 + CONV_HALO, conv_ch), F32),
                   pltpu.VMEM((SUBLANES - 1, b1 + CONV_HALO - SUBLANES, conv_ch), F32),
                   pltpu.VMEM((CONV_W, SUBLANES, conv_ch), F32)]
        sem = ("parallel", "arbitrary")
    return pl.pallas_call(
        functools.partial(_inproj_kernel, conv_ch=conv_ch, att_w=att_w, fuse_conv=conv is not None),
        grid=(a // b0, b // b1),
        in_specs=in_specs,
        out_specs=out_specs,
        out_shape=out_shapes,
        scratch_shapes=scratch,
        compiler_params=_cparams(sem),
        name="in_proj_conv" if conv is not None else "in_proj",
    )(*args)


def _rope_tables(pos):
    half = ROT_DIM // 2
    inv = jnp.float32(ROPE_THETA) ** (-(2.0 / ROT_DIM) * jnp.arange(half, dtype=F32))
    ang = pos.astype(F32)[:, None] * inv[None, :]
    cos, sin = jnp.cos(ang), jnp.sin(ang)
    n = pos.shape[0]
    ones = jnp.ones((n, HEAD_DIM - ROT_DIM), F32)
    zeros = jnp.zeros((n, HEAD_DIM - ROT_DIM), F32)
    zh = jnp.zeros((n, half), F32)
    ta = jnp.concatenate([cos, cos, ones], axis=-1)
    tb = jnp.concatenate([-sin, zh, zeros], axis=-1)
    tc = jnp.concatenate([zh, sin, zeros], axis=-1)
    rep = LANES // HEAD_DIM
    return tuple(jnp.tile(t, (1, rep))[None] for t in (ta, tb, tc))


def _ln_silu(acc, g, b):
    mu = jnp.mean(acc, axis=-1, keepdims=True)
    xc = acc - mu
    y = xc * lax.rsqrt(jnp.mean(xc * xc, axis=-1, keepdims=True) + EPS)
    return _silu(y * g + b)


def _conv_rows(buf, shf, wtile, w_ref, cb_ref, g_ref, b_ref, o_ref, tt, halo):
    sub = shf.shape[0] + 1
    n8 = shf.shape[1]
    for s in range(1, sub):
        shf[s - 1] = buf[s:s + n8, :]
    off = halo - (CONV_W - 1)
    rb = 16
    c = buf.shape[1]
    for jt in range(CONV_W):
        wtile[jt] = jnp.broadcast_to(w_ref[jt:jt + 1, :], (sub, c))

    def chunk(r0):
        acc = None
        for jt in range(CONV_W):
            s = (off + jt) % sub
            lo = r0 + off + jt - s
            rows = buf[lo:lo + rb, :] if s == 0 else shf[s - 1, lo:lo + rb, :]
            term = rows.reshape(rb // sub, sub, c) * wtile[jt][None]
            acc = term if acc is None else acc + term
        acc = acc.reshape(rb, c)
        o_ref[0, r0:r0 + rb, :] = _ln_silu(acc + cb_ref[...], g_ref[...], b_ref[...])

    return [functools.partial(chunk, r0) for r0 in range(0, tt, rb)]


def _conv_sample_kernel(full_ref, w_ref, cb_ref, g_ref, b_ref, o_ref):
    t_out = o_ref.shape[0]
    for t in range(t_out):
        acc = w_ref[0:1, :] * full_ref[t]
        for jt in range(1, CONV_W):
            acc = acc + w_ref[jt:jt + 1, :] * full_ref[t + jt]
        o_ref[t] = _ln_silu(acc + cb_ref[...], g_ref[...], b_ref[...])


def _conv_sample_call(full_t, conv_w, conv_b, ln_g, ln_b):
    tf, nb, c = full_t.shape
    t_out = tf - (CONV_W - 1)
    nbk = 32
    return pl.pallas_call(
        _conv_sample_kernel,
        grid=(nb // nbk,),
        in_specs=[pl.BlockSpec((tf, nbk, c), lambda i: (0, i, 0)),
                  _const_spec((CONV_W, c)), _const_spec((1, c)), _const_spec((1, c)), _const_spec((1, c))],
        out_specs=pl.BlockSpec((t_out, nbk, c), lambda i: (0, i, 0)),
        out_shape=jax.ShapeDtypeStruct((t_out, nb, c), F32),
        compiler_params=_cparams(("parallel",)),
        name="conv_sample",
    )(full_t, conv_w, conv_b.reshape(1, c), ln_g.reshape(1, c), ln_b.reshape(1, c))


def _merge_groups(os_, ls_):
    m = functools.reduce(jnp.maximum, ls_)
    es = [jnp.exp(l - m) for l in ls_]
    num = functools.reduce(lambda a, b: a + b, [e * o for e, o in zip(es, os_)])
    return num / functools.reduce(lambda a, b: a + b, es)


def _attn_prompt_kernel(*refs, tb):
    in_refs, att_ref, (o_scr, l_scr) = refs[:5 * N_DIL], refs[5 * N_DIL], refs[5 * N_DIL + 1:]
    jb = pl.program_id(1)
    bb = BAND_BLOCK
    qi = lax.broadcasted_iota(jnp.int32, (bb, 2 * bb), 0) + bb
    ki = lax.broadcasted_iota(jnp.int32, (bb, 2 * bb), 1)
    dist = qi - ki
    band = (dist >= 0) & (dist <= bb)
    older = ki < bb
    band_first = jnp.logical_and(band, jnp.logical_not(jnp.logical_and(older, jb == 0)))
    assert LANES == 2 * HEAD_DIM
    head0 = lax.broadcasted_iota(jnp.int32, (bb, LANES), 1) < HEAD_DIM
    ones_kv = jnp.ones((2 * bb, LANES), BF16)
    for g, (win, dil) in enumerate(DIL_GROUPS):
        q_ref, kc_ref, kp_ref, vc_ref, vp_ref = in_refs[5 * g:5 * g + 5]
        pb = bb * dil

        def rows(start, count, dil=dil):
            if dil == 1:
                return pl.ds(start if isinstance(start, int) else pl.multiple_of(start, bb), count)
            return pl.ds(start, count, stride=dil)

        def units(ug, carry, g=g, dil=dil, pb=pb, q_ref=q_ref, kc_ref=kc_ref, kp_ref=kp_ref, vc_ref=vc_ref,
                  vp_ref=vp_ref, rows=rows):
            qsels, qus, kus, vus, masks = [], [], [], [], []
            for kk in range(ATTN_UNITS):
                u = ug * ATTN_UNITS + kk
                if isinstance(ug, int):
                    i, r = u // dil, u % dil
                else:
                    i, r = _div_pow2(u, dil), _mod_pow2(u, dil)
                start = i * pb + r
                qsel = rows(start, bb)
                qsels.append(qsel)
                qus.append(q_ref[0, qsel, :].astype(BF16))
                if isinstance(i, int) and i == 0:
                    k2 = jnp.concatenate([kp_ref[0, rows(r, bb), :], kc_ref[0, rows(r, bb), :]], axis=0)
                    v2 = jnp.concatenate([vp_ref[0, rows(r, bb), :], vc_ref[0, rows(r, bb), :]], axis=0)
                    masks.append(band_first)
                else:
                    ksel = rows(start - pb, 2 * bb)
                    k2, v2 = kc_ref[0, ksel, :], vc_ref[0, ksel, :]
                    masks.append(band)
                kus.append(k2.astype(BF16))
                vus.append(v2.astype(BF16))
            chains = [(kk, slice(h * HEAD_DIM, (h + 1) * HEAD_DIM))
                      for kk in range(ATTN_UNITS) for h in range(LANES // HEAD_DIM)]
            nt = (((1,), (1,)), ((), ()))
            ss = [jnp.where(masks[kk], lax.dot_general(qus[kk][:, cs], kus[kk][:, cs], nt,
                                                       preferred_element_type=F32), -jnp.inf) for kk, cs in chains]
            ms = [jnp.max(s, axis=-1, keepdims=True) for s in ss]
            ps = [jnp.exp(s - m).astype(BF16) for s, m in zip(ss, ms)]
            outs = [jnp.dot(p, vus[kk], preferred_element_type=F32) for p, (kk, _) in zip(ps, chains)]
            sums = [jnp.dot(p, ones_kv, preferred_element_type=F32) for p in ps]
            for kk in range(ATTN_UNITS):
                o_cat = jnp.where(head0, outs[2 * kk], outs[2 * kk + 1])
                l_cat = jnp.where(head0, sums[2 * kk], sums[2 * kk + 1])
                m_cat = jnp.where(head0, ms[2 * kk], ms[2 * kk + 1])
                o_scr[g, qsels[kk], :] = o_cat / l_cat
                l_scr[g, qsels[kk], :] = m_cat + jnp.log(l_cat)
            return carry

        ngroups = tb // bb // ATTN_UNITS
        nstatic = min(ngroups, -(-dil // ATTN_UNITS))
        for ug in range(nstatic):
            units(ug, 0)
        if nstatic < ngroups:
            lax.fori_loop(nstatic, ngroups, units, 0)

    mr = 256
    for r0 in range(0, tb, mr):
        att_ref[0, r0:r0 + mr, :] = _merge_groups([o_scr[g, r0:r0 + mr, :] for g in range(N_DIL)],
                                                  [l_scr[g, r0:r0 + mr, :] for g in range(N_DIL)])


def _attn_prompt_call(q, k, v):
    n, l, att_w = q.shape
    bb = BAND_BLOCK
    tb = bb * max(d for _, d in DIL_GROUPS)
    assert l % tb == 0 and all(w // d == bb for w, d in DIL_GROUPS) and att_w == N_DIL * GRP_W
    hp = GRP_W // LANES
    in_specs, args = [], []
    for g, (_, dil) in enumerate(DIL_GROUPS):
        pb = bb * dil
        ratio = tb // pb
        cur = pl.BlockSpec((1, tb, LANES), lambda i, j, h, g=g: (i, j, g * hp + h))
        prev = pl.BlockSpec((1, pb, LANES),
                            lambda i, j, h, g=g, ratio=ratio: (i, jnp.maximum(j * ratio - 1, 0), g * hp + h))
        in_specs += [cur, cur, prev, cur, prev]
        args += [q, k, k, v, v]
    return pl.pallas_call(
        functools.partial(_attn_prompt_kernel, tb=tb),
        grid=(n, l // tb, hp),
        in_specs=in_specs,
        out_specs=pl.BlockSpec((1, tb, LANES), lambda i, j, h: (i, j, h)),
        out_shape=jax.ShapeDtypeStruct((n, l, GRP_W), F32),
        scratch_shapes=[pltpu.VMEM((N_DIL, tb, LANES), F32)] * 2,
        compiler_params=_cparams(("parallel", "arbitrary", "arbitrary")),
        name="attn_prompt",
    )(*args)


def _attn_sample_kernel(q_ref, kn_ref, vn_ref, c0_ref, c1_ref, c2_ref, att_ref):
    t_new = q_ref.shape[1]
    rows = HPG * t_new
    ri = lax.broadcasted_iota(jnp.int32, (rows, GRP_W), 0)
    ci = lax.broadcasted_iota(jnp.int32, (rows, GRP_W), 1)
    head_mask = _div_pow2(ci, HEAD_DIM) == _div_pow2(ri, t_new)
    c_refs = (c0_ref, c1_ref, c2_ref)
    valid = []
    for (win, dil), c_ref in zip(DIL_GROUPS, c_refs):
        wb = c_ref.shape[-1]
        tq = _mod_pow2(lax.broadcasted_iota(jnp.int32, (rows, wb), 0), t_new)
        ki = lax.broadcasted_iota(jnp.int32, (rows, wb), 1)
        dist = wb + tq - ki
        tqn = _mod_pow2(lax.broadcasted_iota(jnp.int32, (rows, t_new), 0), t_new)
        jn = lax.broadcasted_iota(jnp.int32, (rows, t_new), 1)
        dn = tqn - jn
        valid.append(((dist >= 0) & (dist <= win) & (_mod_pow2(dist, dil) == 0),
                      (dn >= 0) & (dn <= win) & (_mod_pow2(dn, dil) == 0)))
    nns = range(q_ref.shape[0])
    os_, ls_ = [[] for _ in nns], [[] for _ in nns]
    for gi, c_ref in enumerate(c_refs):
        cs = slice(gi * GRP_W, (gi + 1) * GRP_W)
        valid_c, valid_n = valid[gi]
        qbs = [jnp.where(head_mask, jnp.concatenate([q_ref[nn, :, cs]] * HPG, axis=0), 0.0) for nn in nns]
        s_cs = [jnp.where(valid_c, _bdot(qbs[nn], c_ref[0, nn, 0]), -jnp.inf) for nn in nns]
        s_ns = [jnp.where(valid_n, _bdot_nt(qbs[nn], kn_ref[nn, :, cs]), -jnp.inf) for nn in nns]
        ms = [jnp.maximum(jnp.max(s_cs[nn], axis=-1, keepdims=True), jnp.max(s_ns[nn], axis=-1, keepdims=True))
              for nn in nns]
        p_cs = [jnp.exp(s_cs[nn] - ms[nn]) for nn in nns]
        p_ns = [jnp.exp(s_ns[nn] - ms[nn]) for nn in nns]
        lsums = [jnp.sum(p_cs[nn], axis=-1, keepdims=True) + jnp.sum(p_ns[nn], axis=-1, keepdims=True)
                 for nn in nns]
        o_fulls = [(_bdot_nt(p_cs[nn], c_ref[0, nn, 1]) + _bdot(p_ns[nn], vn_ref[nn, :, cs])) / lsums[nn]
                   for nn in nns]
        for nn in nns:
            o_full = jnp.where(head_mask, o_fulls[nn], 0.0)
            lse_full = jnp.where(head_mask, jnp.broadcast_to(ms[nn] + jnp.log(lsums[nn]), (rows, GRP_W)), 0.0)
            o = o_full[0:t_new]
            lse = lse_full[0:t_new]
            for h in range(1, HPG):
                o = o + o_full[h * t_new:(h + 1) * t_new]
                lse = lse + lse_full[h * t_new:(h + 1) * t_new]
            os_[nn].append(o)
            ls_[nn].append(lse)
    for nn in nns:
        att_ref[nn] = _merge_groups(os_[nn], ls_[nn])


def _cache_rows_on_lanes(cache):
    ly, nb, wb = cache.shape[:3]
    return jnp.transpose(cache, (0, 1, 3, 4, 5, 2)).reshape(ly, nb, 2, GRP_W, wb)


def _attn_sample_call(q, k, v, caches_t, ei):
    nb, t_new, att_w = q.shape
    assert all(c.shape[-1] == w for c, (w, _) in zip(caches_t, DIL_GROUPS))
    n_blk = 2 if nb % 2 == 0 else 1
    qspec = pl.BlockSpec((n_blk, t_new, att_w), lambda i: (i, 0, 0))
    cspecs = [pl.BlockSpec((1, n_blk) + c.shape[2:], lambda i: (ei, i, 0, 0, 0)) for c in caches_t]
    return pl.pallas_call(
        _attn_sample_kernel,
        grid=(nb // n_blk,),
        in_specs=[qspec, qspec, qspec] + cspecs,
        out_specs=pl.BlockSpec((n_blk, t_new, GRP_W), lambda i: (i, 0, 0)),
        out_shape=jax.ShapeDtypeStruct((nb, t_new, GRP_W), F32),
        compiler_params=_cparams(("parallel",)),
        name="attn_sample",
    )(q, k, v, *caches_t)


def _ffn_kernel(*refs, fchunks, final, mix):
    x_ref, sh_ref, sc_ref, g2_ref, ng_ref, wg_ref, wu_ref, wd_ref, fg_ref = refs[:9]
    o_ref = refs[-1]
    b0, b1, d = x_ref.shape
    rows = b0 * b1
    x = x_ref[...]
    if mix:
        g1_ref, yc_ref, att_ref, wa_ref, wb_ref = refs[9:14]
        mixed = _bdot(yc_ref[...].reshape(rows, -1), wa_ref[0]) + _bdot(att_ref[...].reshape(rows, -1), wb_ref[0])
        x = x + g1_ref[0] * mixed.reshape(b0, b1, d)
    hb = _norm_mod(x, ng_ref[0, 0], sc_ref[0], sh_ref[0]).reshape(rows, d).astype(BF16)
    dff = wg_ref.shape[2]
    acc = None
    for f0, fchunk in fchunks:
        g = jnp.dot(hb, wg_ref[0, :, f0:f0 + fchunk], preferred_element_type=F32)
        u = jnp.dot(hb, wu_ref[0, :, f0:f0 + fchunk], preferred_element_type=F32)
        part = jnp.dot((_silu(g) * u).astype(BF16), wd_ref[0, f0:f0 + fchunk, :], preferred_element_type=F32)
        acc = part if acc is None else acc + part
    xn = x + g2_ref[0] * acc.reshape(b0, b1, d)
    if final:
        xn = xn * lax.rsqrt(jnp.mean(xn * xn, axis=-1, keepdims=True) + EPS) * fg_ref[...]
    o_ref[...] = xn


def _ffn_call(x, mod, gains, layer, wg_bf, wu_bf, wd_bf, final_g, b0, b1, final, mix=None):
    a, b, d = x.shape
    dff = wg_bf.shape[2]
    in_specs = [_row_spec(b0, b1, d), _mod_spec(mod, b0, d, 3), _mod_spec(mod, b0, d, 4), _mod_spec(mod, b0, d, 5),
                _gain_spec(d, layer, 1), _weight_spec(wg_bf.shape, layer), _weight_spec(wu_bf.shape, layer),
                _weight_spec(wd_bf.shape, layer), _const_spec((1, d))]
    args = [x, mod[0], mod[0], mod[0], gains, wg_bf, wu_bf, wd_bf, final_g.reshape(1, d)]
    if mix is not None:
        yc, att, wa_bf, wb_bf, ei = mix
        in_specs += [_mod_spec(mod, b0, d, 2), _row_spec(b0, b1, yc.shape[-1]), _row_spec(b0, b1, att.shape[-1]),
                     _weight_spec(wa_bf.shape, ei), _weight_spec(wb_bf.shape, ei)]
        args += [mod[0], yc, att, wa_bf, wb_bf]
    tiles = -(-dff // MXU_TILE)
    f_split = min(dff, -(-tiles // 2) * MXU_TILE)
    fchunks = tuple((f0, f1 - f0) for f0, f1 in ((0, f_split), (f_split, dff)) if f1 > f0)
    return pl.pallas_call(
        functools.partial(_ffn_kernel, fchunks=fchunks, final=final, mix=mix is not None),
        grid=(a // b0, b // b1),
        in_specs=in_specs,
        out_specs=_row_spec(b0, b1, d),
        out_shape=jax.ShapeDtypeStruct((a, b, d), F32),
        compiler_params=_cparams(("parallel", "parallel")),
        name=("mix_" if mix is not None else "") + ("ffn_final" if final else "ffn"),
    )(*args)


def _s5_param_kernel(lr_ref, li_ref, ldt_ref, br_ref, bi_ref, abr_ref, abi_ref, bbr_ref, bbi_ref,
                     sbr_ref, sbi_ref):
    lr, li = lr_ref[0], li_ref[0]
    dt = jnp.exp(ldt_ref[0])
    mag = jnp.exp(lr * dt)
    ph = li * dt
    abr = mag * jnp.cos(ph)
    abi = mag * jnp.sin(ph)
    den = lr * lr + li * li
    nr = abr - 1.0
    fr = (nr * lr + abi * li) / den
    fi = (abi * lr - nr * li) / den
    abr_ref[0] = abr
    abi_ref[0] = abi
    for h in range(br_ref.shape[1]):
        br, bi = br_ref[0, h], bi_ref[0, h]
        bbr = fr * br - fi * bi
        bbi = fr * bi + fi * br
        bbr_ref[0, h] = bbr
        bbi_ref[0, h] = bbi
        sbr_ref[0, h] = abr * bbr - abi * bbi
        sbi_ref[0, h] = abr * bbi + abi * bbr


def _s5_param_call(lam_re, lam_im, log_dt, b_re, b_im):
    n_odd, g, p = lam_re.shape
    hh = b_re.shape[-1]
    bt_re = jnp.transpose(b_re, (0, 3, 1, 2))
    bt_im = jnp.transpose(b_im, (0, 3, 1, 2))
    gp = pl.BlockSpec((1, g, p), lambda l: (l, 0, 0))
    bs = pl.BlockSpec((1, hh, g, p), lambda l: (l, 0, 0, 0))
    return pl.pallas_call(
        _s5_param_kernel,
        grid=(n_odd,),
        in_specs=[gp, gp, pl.BlockSpec((1, g, 1), lambda l: (l, 0, 0)), bs, bs],
        out_specs=(gp, gp, bs, bs, bs, bs),
        out_shape=(jax.ShapeDtypeStruct((n_odd, g, p), F32),) * 2
        + (jax.ShapeDtypeStruct((n_odd, hh, g, p), F32),) * 4,
        compiler_params=_cparams(("arbitrary",)),
        name="s5_params",
    )(lam_re, lam_im, log_dt.reshape(n_odd, g, 1), bt_re, bt_im)


def _s5_block_weights(bbr, bbi, sbr, sbi, c_re, c_im):
    hh, g, p = bbr.shape
    gl = g // S5_JBLK
    eye = jnp.eye(gl, dtype=F32)
    gh = LANES // hh
    nhb = g // gh
    eye_h = jnp.eye(gh, dtype=F32)

    def bblk(t):
        t = t.reshape(hh, nhb, gh, p).transpose(1, 2, 0, 3)
        t = t[:, :, :, None, :] * eye_h[None, :, None, :, None]
        return t.reshape(nhb, gh * hh, gh * p)

    def cblk(t):
        t = t.reshape(S5_JBLK, gl, hh, p).transpose(0, 1, 3, 2)
        t = t[:, :, :, None, :] * eye[None, :, None, :, None]
        return t.reshape(S5_JBLK, gl * p, gl * hh)

    bb = jnp.concatenate([bblk(bbr), bblk(bbi)], axis=-1).astype(BF16)
    sb = jnp.concatenate([bblk(sbr), bblk(sbi)], axis=-1).astype(BF16)
    cc = jnp.concatenate([cblk(c_re), cblk(-c_im)], axis=1).astype(BF16)
    return bb, jnp.concatenate([bb, sb], axis=1), cc


def _gelu_tanh(x):
    return 0.5 * x * (1.0 + jnp.tanh(math.sqrt(2.0 / math.pi) * (x + 0.044715 * (x * x * x))))


_S5_TILE_GROUPS = 8


def _s5_kernel(x_ref, sh_ref, sc_ref, g1_ref, ng_ref, s0r_ref, s0i_ref, abr_ref, abi_ref, perm_ref, bb_ref, cc_ref,
               dsk_ref, wglu_ref, bglu_ref, xo_ref, sro_ref, sio_ref,
               st_r, st_i, bu_scr, h_scr, yp_scr, y_scr, *, lw, unroll):
    c = pl.program_id(1)
    nbk, tc, d = x_ref.shape
    rows = nbk * tc
    jw = d // S5_JBLK
    hw = bb_ref.shape[2] // 2
    nhalf = jw // LANES
    sw = nhalf * hw
    w = lw * LANES
    sub = 8

    two_step = nbk % sub != 0
    if two_step:
        assert 2 * nbk == sub and tc % 2 == 0 and bb_ref.shape[1] == 2 * LANES

    @pl.when(c == 0)
    def _():
        if two_step:
            st_r[0:nbk, :] = jnp.zeros((nbk, st_r.shape[1]), F32)
            st_i[0:nbk, :] = jnp.zeros((nbk, st_i.shape[1]), F32)
            st_r[nbk:sub, :] = s0r_ref[...]
            st_i[nbk:sub, :] = s0i_ref[...]
        else:
            st_r[...] = s0r_ref[...]
            st_i[...] = s0i_ref[...]

    x = x_ref[...]
    u2 = _norm_mod(x, ng_ref[0, 0], sc_ref[0], sh_ref[0]).reshape(rows, d)
    ub = u2.astype(BF16)
    up = jnp.dot(perm_ref[0], ub, preferred_element_type=F32).astype(BF16)
    if two_step:
        ups = jnp.dot(perm_ref[1], ub, preferred_element_type=F32).astype(BF16)

    nblk = 2 * sw // MXU_TILE
    tph = hw // MXU_TILE
    lhs_cache, y_acc = {}, {}

    def project_in(j, cb):
        hb, part, tile = cb // (2 * tph), cb // tph % 2, cb % tph
        if (j, hb) not in lhs_cache:
            ucols = slice(j * jw + hb * LANES, j * jw + (hb + 1) * LANES)
            lhs = up[:, ucols]
            if two_step:
                lhs = jnp.concatenate([lhs, ups[:, ucols]], axis=1)
            lhs_cache[(j, hb)] = lhs
        wcols = slice(part * hw + tile * MXU_TILE, part * hw + (tile + 1) * MXU_TILE)
        dst = part * sw + hb * hw + tile * MXU_TILE
        bu_scr[j % 2, :, dst:dst + MXU_TILE] = jnp.dot(lhs_cache[(j, hb)], bb_ref[nhalf * j + hb, :, wcols],
                                                       preferred_element_type=F32)

    def project_out(j, kb):
        ks = slice(kb * MXU_TILE, (kb + 1) * MXU_TILE)
        part = jnp.dot(h_scr[j % 2, :, ks].astype(BF16), cc_ref[j, ks, :], preferred_element_type=F32)
        y_acc[j] = part if kb == 0 else y_acc[j] + part
        if kb == nblk - 1:
            for k in range(jw // LANES):
                yp_scr[j * (jw // LANES) + k] = y_acc[j][:, k * LANES:(k + 1) * LANES]

    for cb in range(nblk):
        project_in(0, cb)
    for j in range(S5_JBLK + 1):
        tasks = []
        for b in range(nblk):
            if j + 1 < S5_JBLK:
                tasks.append(functools.partial(project_in, j + 1, b))
            if j > 0:
                tasks.append(functools.partial(project_out, j - 1, b))
        if j == S5_JBLK:
            for task in tasks:
                task()
            break
        slots = (sw // w) * _S5_TILE_GROUPS
        slot = [0]

        def run_tasks(tasks=tasks, slot=slot, slots=slots):
            slot[0] += 1
            upto = slot[0] * len(tasks) // slots
            start = (slot[0] - 1) * len(tasks) // slots
            for task in tasks[start:upto]:
                task()

        bu, hs = bu_scr.at[j % 2], h_scr.at[j % 2]
        for l0 in range(0, sw, w):
            c_re, c_im = slice(l0, l0 + w), slice(sw + l0, sw + l0 + w)
            c_st = slice(j * sw + l0, j * sw + l0 + w)
            if nbk % sub == 0:
                ar = jnp.broadcast_to(abr_ref[:, c_st], (nbk, w))
                ai = jnp.broadcast_to(abi_ref[:, c_st], (nbk, w))

                def step(t, carry, c_re=c_re, c_im=c_im, ar=ar, ai=ai):
                    hr, hi = carry
                    rs = pl.ds(pl.multiple_of(t * nbk, nbk), nbk)
                    nhr = ar * hr - ai * hi + bu[rs, c_re]
                    nhi = ar * hi + ai * hr + bu[rs, c_im]
                    hs[rs, c_re] = nhr
                    hs[rs, c_im] = nhi
                    return nhr, nhi

                hr, hi = lax.fori_loop(0, tc, step, (st_r[:, c_st], st_i[:, c_st]), unroll=unroll)
                st_r[:, c_st] = hr
                st_i[:, c_st] = hi
                for _ in range(_S5_TILE_GROUPS):
                    run_tasks()
            else:
                ar = jnp.broadcast_to(abr_ref[:, c_st], (sub, w))
                ai = jnp.broadcast_to(abi_ref[:, c_st], (sub, w))
                a2r, a2i = ar * ar - ai * ai, 2.0 * (ar * ai)
                lower = lax.broadcasted_iota(jnp.int32, (sub, w), 0) < nbk
                pr, pi = st_r[:, c_st], st_i[:, c_st]
                xr, xi = bu[0:sub, c_re], bu[0:sub, c_im]
                qr, qi = pltpu.roll(pr, nbk, 0), pltpu.roll(pi, nbk, 0)
                hr = jnp.where(lower, ar * qr - ai * qi, a2r * pr - a2i * pi) + xr
                hi = jnp.where(lower, ar * qi + ai * qr, a2r * pi + a2i * pr) + xi
                hs[0:sub, c_re] = hr
                hs[0:sub, c_im] = hi

                sr, si = hr, hi
                ntile = tc // 2
                for grp in range(_S5_TILE_GROUPS):
                    for i in range(max(1, grp * ntile // _S5_TILE_GROUPS), (grp + 1) * ntile // _S5_TILE_GROUPS):
                        rs = slice(i * sub, (i + 1) * sub)
                        sr, si = (a2r * sr - a2i * si + bu[rs, c_re], a2r * si + a2i * sr + bu[rs, c_im])
                        hs[rs, c_re] = sr
                        hs[rs, c_im] = si
                    run_tasks()
                st_r[:, c_st] = sr
                st_i[:, c_st] = si

    nlc = d // LANES
    for k in range(nlc):
        if nbk <= tc:
            for n in range(nbk):
                y_scr[k, n * tc:(n + 1) * tc, :] = yp_scr[k, pl.ds(n, tc, stride=nbk), :]
        else:
            for t in range(tc):
                y_scr[k, pl.ds(t, nbk, stride=tc), :] = yp_scr[k, t * nbk:(t + 1) * nbk, :]
    y = jnp.concatenate([y_scr[k] for k in range(nlc)], axis=-1) + dsk_ref[...] * u2
    g = _bdot(_gelu_tanh(y), wglu_ref[0]) + bglu_ref[...]
    out = g[:, 0:d] * jax.nn.sigmoid(g[:, d:2 * d])
    xo_ref[...] = x + g1_ref[0] * out.reshape(nbk, tc, d)

    @pl.when(c == pl.num_programs(1) - 1)
    def _():
        if two_step:
            sro_ref[...] = st_r[nbk:sub, :]
            sio_ref[...] = st_i[nbk:sub, :]
        else:
            sro_ref[...] = st_r[...]
            sio_ref[...] = st_i[...]


def _s5_call(x, mod, gains, layer, s0r, s0i, abr, abi, bb, cc, d_skip, wglu_bf, oi, b_glu, nbk, tc, lw, unroll):
    a, b, d = x.shape
    ns = s0r.shape[1]
    rows = nbk * tc
    dst = jnp.arange(rows)
    src = (dst % nbk) * tc + dst // nbk
    col = jnp.arange(rows)[None, :]
    perm = jnp.stack([src[:, None] == col,
                      jnp.logical_and(src[:, None] - 1 == col, (dst // nbk > 0)[:, None])]).astype(BF16)
    st_rows = nbk if nbk % 8 == 0 else 2 * nbk
    sspec = pl.BlockSpec((nbk, ns), lambda i, j: (i, 0))
    return pl.pallas_call(
        functools.partial(_s5_kernel, lw=lw, unroll=unroll),
        grid=(a // nbk, b // tc),
        in_specs=[_row_spec(nbk, tc, d), _mod_spec(mod, nbk, d, 0), _mod_spec(mod, nbk, d, 1),
                  _mod_spec(mod, nbk, d, 2), _gain_spec(d, layer, 0), sspec, sspec,
                  _const_spec((1, ns)), _const_spec((1, ns)),
                  _weight_spec(perm.shape), _weight_spec(bb.shape), _weight_spec(cc.shape), _const_spec((1, d)),
                  _weight_spec(wglu_bf.shape, oi), _const_spec((1, 2 * d))],
        out_specs=(_row_spec(nbk, tc, d), sspec, sspec),
        out_shape=(jax.ShapeDtypeStruct((a, b, d), F32),
                   jax.ShapeDtypeStruct((a, ns), F32), jax.ShapeDtypeStruct((a, ns), F32)),
        scratch_shapes=[pltpu.VMEM((st_rows, ns), F32), pltpu.VMEM((st_rows, ns), F32),
                        pltpu.VMEM((2, rows, 2 * ns // S5_JBLK), F32), pltpu.VMEM((2, rows, 2 * ns // S5_JBLK), F32),
                        pltpu.VMEM((d // LANES, rows, LANES), F32), pltpu.VMEM((d // LANES, rows, LANES), F32)],
        compiler_params=_cparams(("parallel", "arbitrary")),
        name="s5_mixer",
    )(x, mod[0], mod[0], mod[0], gains, s0r, s0i, abr.reshape(1, ns), abi.reshape(1, ns), perm, bb, cc,
      d_skip.reshape(1, d), wglu_bf, b_glu.reshape(1, 2 * d))


def _run_trunk(x, mods, tabs, conv_st, kv_st, s5_st, W, b0, b1, s5_tiles):
    a, b, d = x.shape
    depth = mods[0].shape[0]
    conv_ch = W['conv_w'].shape[-1]
    att_w = N_DIL * GRP_W
    new_conv, new_s5 = [], []
    new_kv = [[] for _ in range(N_DIL)]
    ei = oi = 0
    for layer in range(depth):
        mod = (mods[0], layer, mods[1])
        mix = None
        if layer % 2 == 0:
            cw = (W['conv_w'][ei], W['conv_b'][ei], W['conv_ln_g'][ei], W['conv_ln_b'][ei])
            if conv_st is None:
                yc, u_tail, q, k, v = _inproj_call(x, mod, W['gains'], layer, W['w_in_bf'], ei, tabs, b0, b1,
                                                   conv_ch, att_w, conv=cw)
                new_conv.append(u_tail[:, CONV_HALO - (CONV_W - 1):])
                att = _attn_prompt_call(q, k, v)
                for gi, (win, _) in enumerate(DIL_GROUPS):
                    keep = min(win, b)
                    kg = k[:, b - keep:, gi * GRP_W:(gi + 1) * GRP_W].reshape(a, keep, HPG, HEAD_DIM)
                    vg = v[:, b - keep:, gi * GRP_W:(gi + 1) * GRP_W].reshape(a, keep, HPG, HEAD_DIM)
                    new_kv[gi].append(jnp.stack([kg, vg], axis=2))
            else:
                u, q, k, v = _inproj_call(x, mod, W['gains'], layer, W['w_in_bf'], ei, tabs, b0, b1, conv_ch, att_w)
                full = jnp.concatenate([conv_st[ei], u], axis=1)
                yc = jnp.transpose(_conv_sample_call(jnp.transpose(full, (1, 0, 2)), *cw), (1, 0, 2))
                new_conv.append(full[:, -(CONV_W - 1):])
                att = _attn_sample_call(q, k, v, kv_st, ei)
                for gi in range(N_DIL):
                    kg = k[:, :, gi * GRP_W:(gi + 1) * GRP_W].reshape(a, b, HPG, HEAD_DIM)
                    vg = v[:, :, gi * GRP_W:(gi + 1) * GRP_W].reshape(a, b, HPG, HEAD_DIM)
                    new_kv[gi].append(jnp.stack([kg, vg], axis=2))
            mix = (yc, att, W['w_o_a_bf'], W['w_o_b_bf'], ei)
            ei += 1
        else:
            ns = W['abr'].shape[1] * W['abr'].shape[2]
            if s5_st is None:
                s0r = jnp.zeros((a, ns), F32)
                s0i = jnp.zeros((a, ns), F32)
            else:
                s0r = s5_st[oi][..., 0].reshape(a, ns)
                s0i = s5_st[oi][..., 1].reshape(a, ns)
            nbk, tc, lw, unroll = s5_tiles
            bb = W['bb'][oi] if nbk % 8 == 0 else W['bb2'][oi]
            x, sr, si = _s5_call(x, mod, W['gains'], layer, s0r, s0i, W['abr'][oi], W['abi'][oi],
                                 bb, W['cc'][oi], W['s5_d'][oi], W['s5_w_glu_bf'], oi,
                                 W['s5_b_glu'][oi], nbk, tc, lw, unroll)
            g, p = W['abr'].shape[1:]
            new_s5.append(jnp.stack([sr.reshape(a, g, p), si.reshape(a, g, p)], axis=-1))
            oi += 1
        x = _ffn_call(x, mod, W['gains'], layer, W['w_ff_gate_bf'], W['w_ff_up_bf'],
                      W['w_ff_down_bf'], W['final_g'], b0, b1, final=(layer == depth - 1), mix=mix)
    return x, jnp.stack(new_conv), [jnp.stack(kv) for kv in new_kv], jnp.stack(new_s5)


def kernel(x_prompt, x_sample, cache_conv, cache_kv_g0, cache_kv_g1, cache_kv_g2, state_s5, c_prompt, c_sample, norm_g, final_g, w_ada, b_ada, w_in, conv_w, conv_b, conv_ln_g, conv_ln_b, w_o, s5_lam_re, s5_lam_im, s5_log_dt, s5_b_re, s5_b_im, s5_c_re, s5_c_im, s5_d, s5_w_glu, s5_b_glu, w_ff_gate, w_ff_up, w_ff_down):
    n, l, d = x_prompt.shape
    nb, t_new, _ = x_sample.shape
    depth = w_ada.shape[0]
    conv_ch = conv_w.shape[-1]

    m_all = n + nb
    m_pad = -(-m_all // 8) * 8
    c_all = jnp.concatenate([c_sample, c_prompt, jnp.zeros((m_pad - m_all, d), F32)], axis=0)
    mod4 = _ada_call(c_all, w_ada, b_ada).reshape(depth, m_pad, 1, 6 * d)

    abr, abi, bbr, bbi, sbr, sbi = _s5_param_call(s5_lam_re, s5_lam_im, s5_log_dt, s5_b_re, s5_b_im)
    blocks = [_s5_block_weights(bbr[i], bbi[i], sbr[i], sbi[i], s5_c_re[i], s5_c_im[i])
              for i in range(abr.shape[0])]

    W = dict(gains=norm_g.reshape(depth, 2, 1, d), final_g=final_g, conv_w=conv_w, conv_b=conv_b, conv_ln_g=conv_ln_g,
             conv_ln_b=conv_ln_b, s5_d=s5_d, s5_b_glu=s5_b_glu, abr=abr, abi=abi,
             bb=[b[0] for b in blocks], bb2=[b[1] for b in blocks], cc=[b[2] for b in blocks],
             w_in_bf=w_in.astype(BF16), w_o_a_bf=w_o[:, :conv_ch].astype(BF16),
             w_o_b_bf=w_o[:, conv_ch:].astype(BF16), s5_w_glu_bf=s5_w_glu.astype(BF16),
             w_ff_gate_bf=w_ff_gate.astype(BF16), w_ff_up_bf=w_ff_up.astype(BF16),
             w_ff_down_bf=w_ff_down.astype(BF16))

    tabs_p = _rope_tables(jnp.arange(l))
    tabs_s = _rope_tables(PAST_LEN + jnp.arange(t_new))

    tm = min(512, l)
    y_p, conv_p, kvs_p, s5_p = _run_trunk(x_prompt, (mod4, nb), tabs_p, None, None, None, W,
                                          1, tm, (n, 128, 8, True))
    nbk = min(nb, tm // t_new)
    caches_t = [_cache_rows_on_lanes(c) for c in (cache_kv_g0, cache_kv_g1, cache_kv_g2)]
    y_s, conv_s, kvs_s, s5_s = _run_trunk(x_sample, (mod4, 0), tabs_s, cache_conv, caches_t, state_s5, W,
                                          nbk, t_new, (nbk, t_new, 1, True))
    return (y_p, y_s, conv_p, kvs_p[0], kvs_p[1], kvs_p[2], s5_p,
            conv_s, kvs_s[0], kvs_s[1], kvs_s[2], s5_s)
```

```python
import functools
import math

import jax
import jax.numpy as jnp
from jax import lax
from jax.experimental import pallas as pl
from jax.experimental.pallas import tpu as pltpu

F32 = jnp.float32
BF16 = jnp.bfloat16

EPS = 1e-6
CONV_W = 31
HEAD_DIM = 64
HPG = 4
DIL_GROUPS = ((128, 1), (512, 4), (2048, 16))
N_DIL = len(DIL_GROUPS)
GRP_W = HPG * HEAD_DIM
ROT_DIM = HEAD_DIM // 4
ROPE_THETA = 500000.0
BAND_BLOCK = 128
S5_GROUP = 16
S5_STATE = 64
S5_JBLK = 4
PAST_LEN = 2048
LANES = 128
MXU_TILE = 256
SUBLANES = 8
ATTN_UNITS = 8
CONV_HALO = 32
VMEM_LIMIT = 56 * 1024 * 1024


def _cparams(sem):
    return pltpu.CompilerParams(dimension_semantics=sem, vmem_limit_bytes=VMEM_LIMIT)


def _const_spec(shape):
    nd = len(shape)
    return pl.BlockSpec(shape, lambda *_: (0,) * nd)


def _weight_spec(shape, layer=None):
    nd = len(shape)
    if layer is None:
        return pl.BlockSpec(shape, lambda *_: (0,) * nd, pipeline_mode=pl.Buffered(1))
    return pl.BlockSpec((1,) + tuple(shape[1:]), lambda *_: (layer,) + (0,) * (nd - 1),
                        pipeline_mode=pl.Buffered(1))


def _row_spec(b0, b1, cols):
    return pl.BlockSpec((b0, b1, cols), lambda i, j: (i, j, 0))


def _mod_spec(mod, b0, d, k):
    _, layer, row0 = mod
    assert row0 % b0 == 0
    return pl.BlockSpec((1, b0, 1, d), lambda i, j: (layer, row0 // b0 + i, 0, k))


def _gain_spec(d, layer, which):
    return pl.BlockSpec((1, 1, 1, d), lambda *_: (layer, which, 0, 0))


def _norm_mod(x, g, sc, sh):
    y = x * lax.rsqrt(jnp.mean(x * x, axis=-1, keepdims=True) + EPS)
    return (y * g) * (1.0 + sc) + sh


def _silu(x):
    return x * jax.nn.sigmoid(x)


def _log2(n):
    assert n > 0 and n & (n - 1) == 0, n
    return n.bit_length() - 1


def _div_pow2(x, n):
    return x >> _log2(n)


def _mod_pow2(x, n):
    return x & ((1 << _log2(n)) - 1)


def _bdot(a, b):
    return jnp.dot(a.astype(BF16), b.astype(BF16), preferred_element_type=F32)


def _bdot_nt(a, b):
    return lax.dot_general(a.astype(BF16), b.astype(BF16), (((1,), (1,)), ((), ())),
                           preferred_element_type=F32)


def _ada_kernel(c_ref, w_ref, b_ref, o_ref):
    o_ref[0] = _bdot(_silu(c_ref[...]), w_ref[0]) + b_ref[0]


def _ada_call(c_all, w_ada, b_ada):
    depth, d, d6 = w_ada.shape
    m = c_all.shape[0]
    tn = 1536
    return pl.pallas_call(
        _ada_kernel,
        grid=(depth, d6 // tn),
        in_specs=[pl.BlockSpec((m, d), lambda l, j: (0, 0)),
                  pl.BlockSpec((1, d, tn), lambda l, j: (l, 0, j)),
                  pl.BlockSpec((1, 1, tn), lambda l, j: (l, 0, j))],
        out_specs=pl.BlockSpec((1, m, tn), lambda l, j: (l, 0, j)),
        out_shape=jax.ShapeDtypeStruct((depth, m, d6), F32),
        compiler_params=_cparams(("arbitrary", "arbitrary")),
        name="ada_mod",
    )(c_all, w_ada, b_ada.reshape(depth, 1, d6))


def _inproj_kernel(*refs, conv_ch, att_w, fuse_conv):
    x_ref, sh_ref, sc_ref, ng_ref, w_ref, ta_ref, tb_ref, tc_ref = refs[:8]
    if fuse_conv:
        cw_ref, cb_ref, lg_ref, lb_ref, yc_ref, tail_ref, q_ref, k_ref, v_ref, buf, shf, wtile = refs[8:]
    else:
        u_ref, q_ref, k_ref, v_ref = refs[8:]
    b0, b1, d = x_ref.shape
    rows = b0 * b1
    h = _norm_mod(x_ref[...], ng_ref[0, 0], sc_ref[0], sh_ref[0])
    hb = h.reshape(rows, d).astype(BF16)
    a_val = jnp.dot(hb, w_ref[0, :, 0:conv_ch], preferred_element_type=F32)
    a_gate = jnp.dot(hb, w_ref[0, :, conv_ch:2 * conv_ch], preferred_element_type=F32)
    u = a_val * jax.nn.sigmoid(a_gate)
    conv_chunks = []
    if fuse_conv:
        halo = CONV_HALO

        @pl.when(pl.program_id(1) == 0)
        def _():
            buf[0:halo, :] = jnp.zeros((halo, conv_ch), F32)

        buf[halo:halo + rows, :] = u
        conv_chunks = _conv_rows(buf, shf, wtile, cw_ref, cb_ref, lg_ref, lb_ref, yc_ref, rows, halo)
    else:
        u_ref[...] = u.reshape(b0, b1, conv_ch)

    def table(t_ref):
        return jnp.broadcast_to(t_ref[...], (b0, b1, LANES)).reshape(rows, LANES)

    ta, tb, tc = table(ta_ref), table(tb_ref), table(tc_ref)
    half = ROT_DIM // 2

    def rope(seg, scale):
        fwd = pltpu.roll(seg, LANES - half, 1)
        bwd = pltpu.roll(seg, half, 1)
        r = seg * ta + fwd * tb + bwd * tc
        return r * scale if scale != 1.0 else r

    blocks = [(o_ref, c, scale) for o_ref, scale in ((q_ref, HEAD_DIM ** -0.5), (k_ref, 1.0), (v_ref, None))
              for c in range(0, att_w, MXU_TILE)]
    c0 = 2 * conv_ch
    done = 0
    for bi, (o_ref, c, scale) in enumerate(blocks):
        wcol = c0 + bi * MXU_TILE
        z = jnp.dot(hb, w_ref[0, :, wcol:wcol + MXU_TILE], preferred_element_type=F32)
        if scale is not None:
            z = jnp.concatenate([rope(z[:, l:l + LANES], scale) for l in range(0, MXU_TILE, LANES)], axis=-1)
        o_ref[:, :, c:c + MXU_TILE] = z.reshape(b0, b1, MXU_TILE)
        upto = (bi + 1) * len(conv_chunks) // len(blocks)
        for chunk in conv_chunks[done:upto]:
            chunk()
        done = upto
    if fuse_conv:
        tail = buf[rows:rows + halo, :]
        tail_ref[0] = tail
        buf[0:halo, :] = tail


def _inproj_call(x, mod, gains, layer, w_in_bf, ei, tabs, b0, b1, conv_ch, att_w, conv=None):
    a, b, d = x.shape
    ta, tb, tc = tabs
    tspec = pl.BlockSpec((1, b1, LANES), lambda i, j: (0, j, 0))
    qkv_shapes = tuple(jax.ShapeDtypeStruct((a, b, att_w), F32) for _ in range(3))
    qkv_specs = tuple(_row_spec(b0, b1, att_w) for _ in range(3))
    in_specs = [_row_spec(b0, b1, d), _mod_spec(mod, b0, d, 0), _mod_spec(mod, b0, d, 1),
                _gain_spec(d, layer, 0), _weight_spec(w_in_bf.shape, ei), tspec, tspec, tspec]
    args = [x, mod[0], mod[0], gains, w_in_bf, ta, tb, tc]
    if conv is None:
        out_shapes = (jax.ShapeDtypeStruct((a, b, conv_ch), F32),) + qkv_shapes
        out_specs = (_row_spec(b0, b1, conv_ch),) + qkv_specs
        scratch, sem = [], ("parallel", "parallel")
    else:
        assert b0 == 1 and b1 % CONV_HALO == 0
        conv_w, conv_b, ln_g, ln_b = conv
        in_specs += [_const_spec((CONV_W, conv_ch))] + [_const_spec((1, conv_ch))] * 3
        args += [conv_w, conv_b.reshape(1, conv_ch), ln_g.reshape(1, conv_ch), ln_b.reshape(1, conv_ch)]
        out_shapes = (jax.ShapeDtypeStruct((a, b, conv_ch), F32),
                      jax.ShapeDtypeStruct((a, CONV_HALO, conv_ch), F32)) + qkv_shapes
        out_specs = (_row_spec(b0, b1, conv_ch),
                     pl.BlockSpec((1, CONV_HALO, conv_ch), lambda i, j: (i, 0, 0))) + qkv_specs
        scratch = [pltpu.VMEM((b1 + CONV_HALO, conv_ch), F32),
                   pltpu.VMEM((SUBLANES - 1, b1 + CONV_HALO - SUBLANES, conv_ch), F32),
                   pltpu.VMEM((CONV_W, SUBLANES, conv_ch), F32)]
        sem = ("parallel", "arbitrary")
    return pl.pallas_call(
        functools.partial(_inproj_kernel, conv_ch=conv_ch, att_w=att_w, fuse_conv=conv is not None),
        grid=(a // b0, b // b1),
        in_specs=in_specs,
        out_specs=out_specs,
        out_shape=out_shapes,
        scratch_shapes=scratch,
        compiler_params=_cparams(sem),
        name="in_proj_conv" if conv is not None else "in_proj",
    )(*args)


def _rope_tables(pos):
    half = ROT_DIM // 2
    inv = jnp.float32(ROPE_THETA) ** (-(2.0 / ROT_DIM) * jnp.arange(half, dtype=F32))
    ang = pos.astype(F32)[:, None] * inv[None, :]
    cos, sin = jnp.cos(ang), jnp.sin(ang)
    n = pos.shape[0]
    ones = jnp.ones((n, HEAD_DIM - ROT_DIM), F32)
    zeros = jnp.zeros((n, HEAD_DIM - ROT_DIM), F32)
    zh = jnp.zeros((n, half), F32)
    ta = jnp.concatenate([cos, cos, ones], axis=-1)
    tb = jnp.concatenate([-sin, zh, zeros], axis=-1)
    tc = jnp.concatenate([zh, sin, zeros], axis=-1)
    rep = LANES // HEAD_DIM
    return tuple(jnp.tile(t, (1, rep))[None] for t in (ta, tb, tc))


def _ln_silu(acc, g, b):
    mu = jnp.mean(acc, axis=-1, keepdims=True)
    xc = acc - mu
    y = xc * lax.rsqrt(jnp.mean(xc * xc, axis=-1, keepdims=True) + EPS)
    return _silu(y * g + b)


def _conv_rows(buf, shf, wtile, w_ref, cb_ref, g_ref, b_ref, o_ref, tt, halo):
    sub = shf.shape[0] + 1
    n8 = shf.shape[1]
    for s in range(1, sub):
        shf[s - 1] = buf[s:s + n8, :]
    off = halo - (CONV_W - 1)
    rb = 16
    c = buf.shape[1]
    for jt in range(CONV_W):
        wtile[jt] = jnp.broadcast_to(w_ref[jt:jt + 1, :], (sub, c))

    def chunk(r0):
        acc = None
        for jt in range(CONV_W):
            s = (off + jt) % sub
            lo = r0 + off + jt - s
            rows = buf[lo:lo + rb, :] if s == 0 else shf[s - 1, lo:lo + rb, :]
            term = rows.reshape(rb // sub, sub, c) * wtile[jt][None]
            acc = term if acc is None else acc + term
        acc = acc.reshape(rb, c)
        o_ref[0, r0:r0 + rb, :] = _ln_silu(acc + cb_ref[...], g_ref[...], b_ref[...])

    return [functools.partial(chunk, r0) for r0 in range(0, tt, rb)]


def _conv_sample_kernel(full_ref, w_ref, cb_ref, g_ref, b_ref, o_ref):
    t_out = o_ref.shape[0]
    for t in range(t_out):
        acc = w_ref[0:1, :] * full_ref[t]
        for jt in range(1, CONV_W):
            acc = acc + w_ref[jt:jt + 1, :] * full_ref[t + jt]
        o_ref[t] = _ln_silu(acc + cb_ref[...], g_ref[...], b_ref[...])


def _conv_sample_call(full_t, conv_w, conv_b, ln_g, ln_b):
    tf, nb, c = full_t.shape
    t_out = tf - (CONV_W - 1)
    nbk = 32
    return pl.pallas_call(
        _conv_sample_kernel,
        grid=(nb // nbk,),
        in_specs=[pl.BlockSpec((tf, nbk, c), lambda i: (0, i, 0)),
                  _const_spec((CONV_W, c)), _const_spec((1, c)), _const_spec((1, c)), _const_spec((1, c))],
        out_specs=pl.BlockSpec((t_out, nbk, c), lambda i: (0, i, 0)),
        out_shape=jax.ShapeDtypeStruct((t_out, nb, c), F32),
        compiler_params=_cparams(("parallel",)),
        name="conv_sample",
    )(full_t, conv_w, conv_b.reshape(1, c), ln_g.reshape(1, c), ln_b.reshape(1, c))


def _merge_groups(os_, ls_):
    m = functools.reduce(jnp.maximum, ls_)
    es = [jnp.exp(l - m) for l in ls_]
    num = functools.reduce(lambda a, b: a + b, [e * o for e, o in zip(es, os_)])
    return num / functools.reduce(lambda a, b: a + b, es)


def _attn_prompt_kernel(*refs, tb):
    in_refs, att_ref, (o_scr, l_scr) = refs[:5 * N_DIL], refs[5 * N_DIL], refs[5 * N_DIL + 1:]
    jb = pl.program_id(1)
    bb = BAND_BLOCK
    qi = lax.broadcasted_iota(jnp.int32, (bb, 2 * bb), 0) + bb
    ki = lax.broadcasted_iota(jnp.int32, (bb, 2 * bb), 1)
    dist = qi - ki
    band = (dist >= 0) & (dist <= bb)
    older = ki < bb
    band_first = jnp.logical_and(band, jnp.logical_not(jnp.logical_and(older, jb == 0)))
    assert LANES == 2 * HEAD_DIM
    head0 = lax.broadcasted_iota(jnp.int32, (bb, LANES), 1) < HEAD_DIM
    ones_kv = jnp.ones((2 * bb, LANES), BF16)
    for g, (win, dil) in enumerate(DIL_GROUPS):
        q_ref, kc_ref, kp_ref, vc_ref, vp_ref = in_refs[5 * g:5 * g + 5]
        pb = bb * dil

        def rows(start, count, dil=dil):
            if dil == 1:
                return pl.ds(start if isinstance(start, int) else pl.multiple_of(start, bb), count)
            return pl.ds(start, count, stride=dil)

        def units(ug, carry, g=g, dil=dil, pb=pb, q_ref=q_ref, kc_ref=kc_ref, kp_ref=kp_ref, vc_ref=vc_ref,
                  vp_ref=vp_ref, rows=rows):
            qsels, qus, kus, vus, masks = [], [], [], [], []
            for kk in range(ATTN_UNITS):
                u = ug * ATTN_UNITS + kk
                if isinstance(ug, int):
                    i, r = u // dil, u % dil
                else:
                    i, r = _div_pow2(u, dil), _mod_pow2(u, dil)
                start = i * pb + r
                qsel = rows(start, bb)
                qsels.append(qsel)
                qus.append(q_ref[0, qsel, :].astype(BF16))
                if isinstance(i, int) and i == 0:
                    k2 = jnp.concatenate([kp_ref[0, rows(r, bb), :], kc_ref[0, rows(r, bb), :]], axis=0)
                    v2 = jnp.concatenate([vp_ref[0, rows(r, bb), :], vc_ref[0, rows(r, bb), :]], axis=0)
                    masks.append(band_first)
                else:
                    ksel = rows(start - pb, 2 * bb)
                    k2, v2 = kc_ref[0, ksel, :], vc_ref[0, ksel, :]
                    masks.append(band)
                kus.append(k2.astype(BF16))
                vus.append(v2.astype(BF16))
            chains = [(kk, slice(h * HEAD_DIM, (h + 1) * HEAD_DIM))
                      for kk in range(ATTN_UNITS) for h in range(LANES // HEAD_DIM)]
            nt = (((1,), (1,)), ((), ()))
            ss = [jnp.where(masks[kk], lax.dot_general(qus[kk][:, cs], kus[kk][:, cs], nt,
                                                       preferred_element_type=F32), -jnp.inf) for kk, cs in chains]
            ms = [jnp.max(s, axis=-1, keepdims=True) for s in ss]
            ps = [jnp.exp(s - m).astype(BF16) for s, m in zip(ss, ms)]
            outs = [jnp.dot(p, vus[kk], preferred_element_type=F32) for p, (kk, _) in zip(ps, chains)]
            sums = [jnp.dot(p, ones_kv, preferred_element_type=F32) for p in ps]
            for kk in range(ATTN_UNITS):
                o_cat = jnp.where(head0, outs[2 * kk], outs[2 * kk + 1])
                l_cat = jnp.where(head0, sums[2 * kk], sums[2 * kk + 1])
                m_cat = jnp.where(head0, ms[2 * kk], ms[2 * kk + 1])
                o_scr[g, qsels[kk], :] = o_cat / l_cat
                l_scr[g, qsels[kk], :] = m_cat + jnp.log(l_cat)
            return carry

        ngroups = tb // bb // ATTN_UNITS
        nstatic = min(ngroups, -(-dil // ATTN_UNITS))
        for ug in range(nstatic):
            units(ug, 0)
        if nstatic < ngroups:
            lax.fori_loop(nstatic, ngroups, units, 0)

    mr = 256
    for r0 in range(0, tb, mr):
        att_ref[0, r0:r0 + mr, :] = _merge_groups([o_scr[g, r0:r0 + mr, :] for g in range(N_DIL)],
                                                  [l_scr[g, r0:r0 + mr, :] for g in range(N_DIL)])


def _attn_prompt_call(q, k, v):
    n, l, att_w = q.shape
    bb = BAND_BLOCK
    tb = bb * max(d for _, d in DIL_GROUPS)
    assert l % tb == 0 and all(w // d == bb for w, d in DIL_GROUPS) and att_w == N_DIL * GRP_W
    hp = GRP_W // LANES
    in_specs, args = [], []
    for g, (_, dil) in enumerate(DIL_GROUPS):
        pb = bb * dil
        ratio = tb // pb
        cur = pl.BlockSpec((1, tb, LANES), lambda i, j, h, g=g: (i, j, g * hp + h))
        prev = pl.BlockSpec((1, pb, LANES),
                            lambda i, j, h, g=g, ratio=ratio: (i, jnp.maximum(j * ratio - 1, 0), g * hp + h))
        in_specs += [cur, cur, prev, cur, prev]
        args += [q, k, k, v, v]
    return pl.pallas_call(
        functools.partial(_attn_prompt_kernel, tb=tb),
        grid=(n, l // tb, hp),
        in_specs=in_specs,
        out_specs=pl.BlockSpec((1, tb, LANES), lambda i, j, h: (i, j, h)),
        out_shape=jax.ShapeDtypeStruct((n, l, GRP_W), F32),
        scratch_shapes=[pltpu.VMEM((N_DIL, tb, LANES), F32)] * 2,
        compiler_params=_cparams(("parallel", "arbitrary", "arbitrary")),
        name="attn_prompt",
    )(*args)


def _attn_sample_kernel(q_ref, kn_ref, vn_ref, c0_ref, c1_ref, c2_ref, att_ref):
    t_new = q_ref.shape[1]
    rows = HPG * t_new
    ri = lax.broadcasted_iota(jnp.int32, (rows, GRP_W), 0)
    ci = lax.broadcasted_iota(jnp.int32, (rows, GRP_W), 1)
    head_mask = _div_pow2(ci, HEAD_DIM) == _div_pow2(ri, t_new)
    c_refs = (c0_ref, c1_ref, c2_ref)
    valid = []
    for (win, dil), c_ref in zip(DIL_GROUPS, c_refs):
        wb = c_ref.shape[-1]
        tq = _mod_pow2(lax.broadcasted_iota(jnp.int32, (rows, wb), 0), t_new)
        ki = lax.broadcasted_iota(jnp.int32, (rows, wb), 1)
        dist = wb + tq - ki
        tqn = _mod_pow2(lax.broadcasted_iota(jnp.int32, (rows, t_new), 0), t_new)
        jn = lax.broadcasted_iota(jnp.int32, (rows, t_new), 1)
        dn = tqn - jn
        valid.append(((dist >= 0) & (dist <= win) & (_mod_pow2(dist, dil) == 0),
                      (dn >= 0) & (dn <= win) & (_mod_pow2(dn, dil) == 0)))
    nns = range(q_ref.shape[0])
    os_, ls_ = [[] for _ in nns], [[] for _ in nns]
    for gi, c_ref in enumerate(c_refs):
        cs = slice(gi * GRP_W, (gi + 1) * GRP_W)
        valid_c, valid_n = valid[gi]
        qbs = [jnp.where(head_mask, jnp.concatenate([q_ref[nn, :, cs]] * HPG, axis=0), 0.0) for nn in nns]
        s_cs = [jnp.where(valid_c, _bdot(qbs[nn], c_ref[0, nn, 0]), -jnp.inf) for nn in nns]
        s_ns = [jnp.where(valid_n, _bdot_nt(qbs[nn], kn_ref[nn, :, cs]), -jnp.inf) for nn in nns]
        ms = [jnp.maximum(jnp.max(s_cs[nn], axis=-1, keepdims=True), jnp.max(s_ns[nn], axis=-1, keepdims=True))
              for nn in nns]
        p_cs = [jnp.exp(s_cs[nn] - ms[nn]) for nn in nns]
        p_ns = [jnp.exp(s_ns[nn] - ms[nn]) for nn in nns]
        lsums = [jnp.sum(p_cs[nn], axis=-1, keepdims=True) + jnp.sum(p_ns[nn], axis=-1, keepdims=True)
                 for nn in nns]
        o_fulls = [(_bdot_nt(p_cs[nn], c_ref[0, nn, 1]) + _bdot(p_ns[nn], vn_ref[nn, :, cs])) / lsums[nn]
                   for nn in nns]
        for nn in nns:
            o_full = jnp.where(head_mask, o_fulls[nn], 0.0)
            lse_full = jnp.where(head_mask, jnp.broadcast_to(ms[nn] + jnp.log(lsums[nn]), (rows, GRP_W)), 0.0)
            o = o_full[0:t_new]
            lse = lse_full[0:t_new]
            for h in range(1, HPG):
                o = o + o_full[h * t_new:(h + 1) * t_new]
                lse = lse + lse_full[h * t_new:(h + 1) * t_new]
            os_[nn].append(o)
            ls_[nn].append(lse)
    for nn in nns:
        att_ref[nn] = _merge_groups(os_[nn], ls_[nn])


def _cache_rows_on_lanes(cache):
    ly, nb, wb = cache.shape[:3]
    return jnp.transpose(cache, (0, 1, 3, 4, 5, 2)).reshape(ly, nb, 2, GRP_W, wb)


def _attn_sample_call(q, k, v, caches_t, ei):
    nb, t_new, att_w = q.shape
    assert all(c.shape[-1] == w for c, (w, _) in zip(caches_t, DIL_GROUPS))
    n_blk = 2 if nb % 2 == 0 else 1
    qspec = pl.BlockSpec((n_blk, t_new, att_w), lambda i: (i, 0, 0))
    cspecs = [pl.BlockSpec((1, n_blk) + c.shape[2:], lambda i: (ei, i, 0, 0, 0)) for c in caches_t]
    return pl.pallas_call(
        _attn_sample_kernel,
        grid=(nb // n_blk,),
        in_specs=[qspec, qspec, qspec] + cspecs,
        out_specs=pl.BlockSpec((n_blk, t_new, GRP_W), lambda i: (i, 0, 0)),
        out_shape=jax.ShapeDtypeStruct((nb, t_new, GRP_W), F32),
        compiler_params=_cparams(("parallel",)),
        name="attn_sample",
    )(q, k, v, *caches_t)


def _ffn_kernel(*refs, fchunks, final, mix):
    x_ref, sh_ref, sc_ref, g2_ref, ng_ref, wg_ref, wu_ref, wd_ref, fg_ref = refs[:9]
    o_ref = refs[-1]
    b0, b1, d = x_ref.shape
    rows = b0 * b1
    x = x_ref[...]
    if mix:
        g1_ref, yc_ref, att_ref, wa_ref, wb_ref = refs[9:14]
        mixed = _bdot(yc_ref[...].reshape(rows, -1), wa_ref[0]) + _bdot(att_ref[...].reshape(rows, -1), wb_ref[0])
        x = x + g1_ref[0] * mixed.reshape(b0, b1, d)
    hb = _norm_mod(x, ng_ref[0, 0], sc_ref[0], sh_ref[0]).reshape(rows, d).astype(BF16)
    dff = wg_ref.shape[2]
    acc = None
    for f0, fchunk in fchunks:
        g = jnp.dot(hb, wg_ref[0, :, f0:f0 + fchunk], preferred_element_type=F32)
        u = jnp.dot(hb, wu_ref[0, :, f0:f0 + fchunk], preferred_element_type=F32)
        part = jnp.dot((_silu(g) * u).astype(BF16), wd_ref[0, f0:f0 + fchunk, :], preferred_element_type=F32)
        acc = part if acc is None else acc + part
    xn = x + g2_ref[0] * acc.reshape(b0, b1, d)
    if final:
        xn = xn * lax.rsqrt(jnp.mean(xn * xn, axis=-1, keepdims=True) + EPS) * fg_ref[...]
    o_ref[...] = xn


def _ffn_call(x, mod, gains, layer, wg_bf, wu_bf, wd_bf, final_g, b0, b1, final, mix=None):
    a, b, d = x.shape
    dff = wg_bf.shape[2]
    in_specs = [_row_spec(b0, b1, d), _mod_spec(mod, b0, d, 3), _mod_spec(mod, b0, d, 4), _mod_spec(mod, b0, d, 5),
                _gain_spec(d, layer, 1), _weight_spec(wg_bf.shape, layer), _weight_spec(wu_bf.shape, layer),
                _weight_spec(wd_bf.shape, layer), _const_spec((1, d))]
    args = [x, mod[0], mod[0], mod[0], gains, wg_bf, wu_bf, wd_bf, final_g.reshape(1, d)]
    if mix is not None:
        yc, att, wa_bf, wb_bf, ei = mix
        in_specs += [_mod_spec(mod, b0, d, 2), _row_spec(b0, b1, yc.shape[-1]), _row_spec(b0, b1, att.shape[-1]),
                     _weight_spec(wa_bf.shape, ei), _weight_spec(wb_bf.shape, ei)]
        args += [mod[0], yc, att, wa_bf, wb_bf]
    tiles = -(-dff // MXU_TILE)
    f_split = min(dff, -(-tiles // 2) * MXU_TILE)
    fchunks = tuple((f0, f1 - f0) for f0, f1 in ((0, f_split), (f_split, dff)) if f1 > f0)
    return pl.pallas_call(
        functools.partial(_ffn_kernel, fchunks=fchunks, final=final, mix=mix is not None),
        grid=(a // b0, b // b1),
        in_specs=in_specs,
        out_specs=_row_spec(b0, b1, d),
        out_shape=jax.ShapeDtypeStruct((a, b, d), F32),
        compiler_params=_cparams(("parallel", "parallel")),
        name=("mix_" if mix is not None else "") + ("ffn_final" if final else "ffn"),
    )(*args)


def _s5_param_kernel(lr_ref, li_ref, ldt_ref, br_ref, bi_ref, abr_ref, abi_ref, bbr_ref, bbi_ref,
                     sbr_ref, sbi_ref):
    lr, li = lr_ref[0], li_ref[0]
    dt = jnp.exp(ldt_ref[0])
    mag = jnp.exp(lr * dt)
    ph = li * dt
    abr = mag * jnp.cos(ph)
    abi = mag * jnp.sin(ph)
    den = lr * lr + li * li
    nr = abr - 1.0
    fr = (nr * lr + abi * li) / den
    fi = (abi * lr - nr * li) / den
    abr_ref[0] = abr
    abi_ref[0] = abi
    for h in range(br_ref.shape[1]):
        br, bi = br_ref[0, h], bi_ref[0, h]
        bbr = fr * br - fi * bi
        bbi = fr * bi + fi * br
        bbr_ref[0, h] = bbr
        bbi_ref[0, h] = bbi
        sbr_ref[0, h] = abr * bbr - abi * bbi
        sbi_ref[0, h] = abr * bbi + abi * bbr


def _s5_param_call(lam_re, lam_im, log_dt, b_re, b_im):
    n_odd, g, p = lam_re.shape
    hh = b_re.shape[-1]
    bt_re = jnp.transpose(b_re, (0, 3, 1, 2))
    bt_im = jnp.transpose(b_im, (0, 3, 1, 2))
    gp = pl.BlockSpec((1, g, p), lambda l: (l, 0, 0))
    bs = pl.BlockSpec((1, hh, g, p), lambda l: (l, 0, 0, 0))
    return pl.pallas_call(
        _s5_param_kernel,
        grid=(n_odd,),
        in_specs=[gp, gp, pl.BlockSpec((1, g, 1), lambda l: (l, 0, 0)), bs, bs],
        out_specs=(gp, gp, bs, bs, bs, bs),
        out_shape=(jax.ShapeDtypeStruct((n_odd, g, p), F32),) * 2
        + (jax.ShapeDtypeStruct((n_odd, hh, g, p), F32),) * 4,
        compiler_params=_cparams(("arbitrary",)),
        name="s5_params",
    )(lam_re, lam_im, log_dt.reshape(n_odd, g, 1), bt_re, bt_im)


def _s5_block_weights(bbr, bbi, sbr, sbi, c_re, c_im):
    hh, g, p = bbr.shape
    gl = g // S5_JBLK
    eye = jnp.eye(gl, dtype=F32)
    gh = LANES // hh
    nhb = g // gh
    eye_h = jnp.eye(gh, dtype=F32)

    def bblk(t):
        t = t.reshape(hh, nhb, gh, p).transpose(1, 2, 0, 3)
        t = t[:, :, :, None, :] * eye_h[None, :, None, :, None]
        return t.reshape(nhb, gh * hh, gh * p)

    def cblk(t):
        t = t.reshape(S5_JBLK, gl, hh, p).transpose(0, 1, 3, 2)
        t = t[:, :, :, None, :] * eye[None, :, None, :, None]
        return t.reshape(S5_JBLK, gl * p, gl * hh)

    bb = jnp.concatenate([bblk(bbr), bblk(bbi)], axis=-1).astype(BF16)
    sb = jnp.concatenate([bblk(sbr), bblk(sbi)], axis=-1).astype(BF16)
    cc = jnp.concatenate([cblk(c_re), cblk(-c_im)], axis=1).astype(BF16)
    return bb, jnp.concatenate([bb, sb], axis=1), cc


def _gelu_tanh(x):
    return 0.5 * x * (1.0 + jnp.tanh(math.sqrt(2.0 / math.pi) * (x + 0.044715 * (x * x * x))))


_S5_TILE_GROUPS = 8


def _s5_kernel(x_ref, sh_ref, sc_ref, g1_ref, ng_ref, s0r_ref, s0i_ref, abr_ref, abi_ref, perm_ref, bb_ref, cc_ref,
               dsk_ref, wglu_ref, bglu_ref, xo_ref, sro_ref, sio_ref,
               st_r, st_i, bu_scr, h_scr, yp_scr, y_scr, *, lw, unroll):
    c = pl.program_id(1)
    nbk, tc, d = x_ref.shape
    rows = nbk * tc
    jw = d // S5_JBLK
    hw = bb_ref.shape[2] // 2
    nhalf = jw // LANES
    sw = nhalf * hw
    w = lw * LANES
    sub = 8

    two_step = nbk % sub != 0
    if two_step:
        assert 2 * nbk == sub and tc % 2 == 0 and bb_ref.shape[1] == 2 * LANES

    @pl.when(c == 0)
    def _():
        if two_step:
            st_r[0:nbk, :] = jnp.zeros((nbk, st_r.shape[1]), F32)
            st_i[0:nbk, :] = jnp.zeros((nbk, st_i.shape[1]), F32)
            st_r[nbk:sub, :] = s0r_ref[...]
            st_i[nbk:sub, :] = s0i_ref[...]
        else:
            st_r[...] = s0r_ref[...]
            st_i[...] = s0i_ref[...]

    x = x_ref[...]
    u2 = _norm_mod(x, ng_ref[0, 0], sc_ref[0], sh_ref[0]).reshape(rows, d)
    ub = u2.astype(BF16)
    up = jnp.dot(perm_ref[0], ub, preferred_element_type=F32).astype(BF16)
    if two_step:
        ups = jnp.dot(perm_ref[1], ub, preferred_element_type=F32).astype(BF16)

    nblk = 2 * sw // MXU_TILE
    tph = hw // MXU_TILE
    lhs_cache, y_acc = {}, {}

    def project_in(j, cb):
        hb, part, tile = cb // (2 * tph), cb // tph % 2, cb % tph
        if (j, hb) not in lhs_cache:
            ucols = slice(j * jw + hb * LANES, j * jw + (hb + 1) * LANES)
            lhs = up[:, ucols]
            if two_step:
                lhs = jnp.concatenate([lhs, ups[:, ucols]], axis=1)
            lhs_cache[(j, hb)] = lhs
        wcols = slice(part * hw + tile * MXU_TILE, part * hw + (tile + 1) * MXU_TILE)
        dst = part * sw + hb * hw + tile * MXU_TILE
        bu_scr[j % 2, :, dst:dst + MXU_TILE] = jnp.dot(lhs_cache[(j, hb)], bb_ref[nhalf * j + hb, :, wcols],
                                                       preferred_element_type=F32)

    def project_out(j, kb):
        ks = slice(kb * MXU_TILE, (kb + 1) * MXU_TILE)
        part = jnp.dot(h_scr[j % 2, :, ks].astype(BF16), cc_ref[j, ks, :], preferred_element_type=F32)
        y_acc[j] = part if kb == 0 else y_acc[j] + part
        if kb == nblk - 1:
            for k in range(jw // LANES):
                yp_scr[j * (jw // LANES) + k] = y_acc[j][:, k * LANES:(k + 1) * LANES]

    for cb in range(nblk):
        project_in(0, cb)
    for j in range(S5_JBLK + 1):
        tasks = []
        for b in range(nblk):
            if j + 1 < S5_JBLK:
                tasks.append(functools.partial(project_in, j + 1, b))
            if j > 0:
                tasks.append(functools.partial(project_out, j - 1, b))
        if j == S5_JBLK:
            for task in tasks:
                task()
            break
        slots = (sw // w) * _S5_TILE_GROUPS
        slot = [0]

        def run_tasks(tasks=tasks, slot=slot, slots=slots):
            slot[0] += 1
            upto = slot[0] * len(tasks) // slots
            start = (slot[0] - 1) * len(tasks) // slots
            for task in tasks[start:upto]:
                task()

        bu, hs = bu_scr.at[j % 2], h_scr.at[j % 2]
        for l0 in range(0, sw, w):
            c_re, c_im = slice(l0, l0 + w), slice(sw + l0, sw + l0 + w)
            c_st = slice(j * sw + l0, j * sw + l0 + w)
            if nbk % sub == 0:
                ar = jnp.broadcast_to(abr_ref[:, c_st], (nbk, w))
                ai = jnp.broadcast_to(abi_ref[:, c_st], (nbk, w))

                def step(t, carry, c_re=c_re, c_im=c_im, ar=ar, ai=ai):
                    hr, hi = carry
                    rs = pl.ds(pl.multiple_of(t * nbk, nbk), nbk)
                    nhr = ar * hr - ai * hi + bu[rs, c_re]
                    nhi = ar * hi + ai * hr + bu[rs, c_im]
                    hs[rs, c_re] = nhr
                    hs[rs, c_im] = nhi
                    return nhr, nhi

                hr, hi = lax.fori_loop(0, tc, step, (st_r[:, c_st], st_i[:, c_st]), unroll=unroll)
                st_r[:, c_st] = hr
                st_i[:, c_st] = hi
                for _ in range(_S5_TILE_GROUPS):
                    run_tasks()
            else:
                ar = jnp.broadcast_to(abr_ref[:, c_st], (sub, w))
                ai = jnp.broadcast_to(abi_ref[:, c_st], (sub, w))
                a2r, a2i = ar * ar - ai * ai, 2.0 * (ar * ai)
                lower = lax.broadcasted_iota(jnp.int32, (sub, w), 0) < nbk
                pr, pi = st_r[:, c_st], st_i[:, c_st]
                xr, xi = bu[0:sub, c_re], bu[0:sub, c_im]
                qr, qi = pltpu.roll(pr, nbk, 0), pltpu.roll(pi, nbk, 0)
                hr = jnp.where(lower, ar * qr - ai * qi, a2r * pr - a2i * pi) + xr
                hi = jnp.where(lower, ar * qi + ai * qr, a2r * pi + a2i * pr) + xi
                hs[0:sub, c_re] = hr
                hs[0:sub, c_im] = hi

                sr, si = hr, hi
                ntile = tc // 2
                for grp in range(_S5_TILE_GROUPS):
                    for i in range(max(1, grp * ntile // _S5_TILE_GROUPS), (grp + 1) * ntile // _S5_TILE_GROUPS):
                        rs = slice(i * sub, (i + 1) * sub)
                        sr, si = (a2r * sr - a2i * si + bu[rs, c_re], a2r * si + a2i * sr + bu[rs, c_im])
                        hs[rs, c_re] = sr
                        hs[rs, c_im] = si
                    run_tasks()
                st_r[:, c_st] = sr
                st_i[:, c_st] = si

    nlc = d // LANES
    for k in range(nlc):
        if nbk <= tc:
            for n in range(nbk):
                y_scr[k, n * tc:(n + 1) * tc, :] = yp_scr[k, pl.ds(n, tc, stride=nbk), :]
        else:
            for t in range(tc):
                y_scr[k, pl.ds(t, nbk, stride=tc), :] = yp_scr[k, t * nbk:(t + 1) * nbk, :]
    y = jnp.concatenate([y_scr[k] for k in range(nlc)], axis=-1) + dsk_ref[...] * u2
    g = _bdot(_gelu_tanh(y), wglu_ref[0]) + bglu_ref[...]
    out = g[:, 0:d] * jax.nn.sigmoid(g[:, d:2 * d])
    xo_ref[...] = x + g1_ref[0] * out.reshape(nbk, tc, d)

    @pl.when(c == pl.num_programs(1) - 1)
    def _():
        if two_step:
            sro_ref[...] = st_r[nbk:sub, :]
            sio_ref[...] = st_i[nbk:sub, :]
        else:
            sro_ref[...] = st_r[...]
            sio_ref[...] = st_i[...]


def _s5_call(x, mod, gains, layer, s0r, s0i, abr, abi, bb, cc, d_skip, wglu_bf, oi, b_glu, nbk, tc, lw, unroll):
    a, b, d = x.shape
    ns = s0r.shape[1]
    rows = nbk * tc
    dst = jnp.arange(rows)
    src = (dst % nbk) * tc + dst // nbk
    col = jnp.arange(rows)[None, :]
    perm = jnp.stack([src[:, None] == col,
                      jnp.logical_and(src[:, None] - 1 == col, (dst // nbk > 0)[:, None])]).astype(BF16)
    st_rows = nbk if nbk % 8 == 0 else 2 * nbk
    sspec = pl.BlockSpec((nbk, ns), lambda i, j: (i, 0))
    return pl.pallas_call(
        functools.partial(_s5_kernel, lw=lw, unroll=unroll),
        grid=(a // nbk, b // tc),
        in_specs=[_row_spec(nbk, tc, d), _mod_spec(mod, nbk, d, 0), _mod_spec(mod, nbk, d, 1),
                  _mod_spec(mod, nbk, d, 2), _gain_spec(d, layer, 0), sspec, sspec,
                  _const_spec((1, ns)), _const_spec((1, ns)),
                  _weight_spec(perm.shape), _weight_spec(bb.shape), _weight_spec(cc.shape), _const_spec((1, d)),
                  _weight_spec(wglu_bf.shape, oi), _const_spec((1, 2 * d))],
        out_specs=(_row_spec(nbk, tc, d), sspec, sspec),
        out_shape=(jax.ShapeDtypeStruct((a, b, d), F32),
                   jax.ShapeDtypeStruct((a, ns), F32), jax.ShapeDtypeStruct((a, ns), F32)),
        scratch_shapes=[pltpu.VMEM((st_rows, ns), F32), pltpu.VMEM((st_rows, ns), F32),
                        pltpu.VMEM((2, rows, 2 * ns // S5_JBLK), F32), pltpu.VMEM((2, rows, 2 * ns // S5_JBLK), F32),
                        pltpu.VMEM((d // LANES, rows, LANES), F32), pltpu.VMEM((d // LANES, rows, LANES), F32)],
        compiler_params=_cparams(("parallel", "arbitrary")),
        name="s5_mixer",
    )(x, mod[0], mod[0], mod[0], gains, s0r, s0i, abr.reshape(1, ns), abi.reshape(1, ns), perm, bb, cc,
      d_skip.reshape(1, d), wglu_bf, b_glu.reshape(1, 2 * d))


def _run_trunk(x, mods, tabs, conv_st, kv_st, s5_st, W, b0, b1, s5_tiles):
    a, b, d = x.shape
    depth = mods[0].shape[0]
    conv_ch = W['conv_w'].shape[-1]
    att_w = N_DIL * GRP_W
    new_conv, new_s5 = [], []
    new_kv = [[] for _ in range(N_DIL)]
    ei = oi = 0
    for layer in range(depth):
        mod = (mods[0], layer, mods[1])
        mix = None
        if layer % 2 == 0:
            cw = (W['conv_w'][ei], W['conv_b'][ei], W['conv_ln_g'][ei], W['conv_ln_b'][ei])
            if conv_st is None:
                yc, u_tail, q, k, v = _inproj_call(x, mod, W['gains'], layer, W['w_in_bf'], ei, tabs, b0, b1,
                                                   conv_ch, att_w, conv=cw)
                new_conv.append(u_tail[:, CONV_HALO - (CONV_W - 1):])
                att = _attn_prompt_call(q, k, v)
                for gi, (win, _) in enumerate(DIL_GROUPS):
                    keep = min(win, b)
                    kg = k[:, b - keep:, gi * GRP_W:(gi + 1) * GRP_W].reshape(a, keep, HPG, HEAD_DIM)
                    vg = v[:, b - keep:, gi * GRP_W:(gi + 1) * GRP_W].reshape(a, keep, HPG, HEAD_DIM)
                    new_kv[gi].append(jnp.stack([kg, vg], axis=2))
            else:
                u, q, k, v = _inproj_call(x, mod, W['gains'], layer, W['w_in_bf'], ei, tabs, b0, b1, conv_ch, att_w)
                full = jnp.concatenate([conv_st[ei], u], axis=1)
                yc = jnp.transpose(_conv_sample_call(jnp.transpose(full, (1, 0, 2)), *cw), (1, 0, 2))
                new_conv.append(full[:, -(CONV_W - 1):])
                att = _attn_sample_call(q, k, v, kv_st, ei)
                for gi in range(N_DIL):
                    kg = k[:, :, gi * GRP_W:(gi + 1) * GRP_W].reshape(a, b, HPG, HEAD_DIM)
                    vg = v[:, :, gi * GRP_W:(gi + 1) * GRP_W].reshape(a, b, HPG, HEAD_DIM)
                    new_kv[gi].append(jnp.stack([kg, vg], axis=2))
            mix = (yc, att, W['w_o_a_bf'], W['w_o_b_bf'], ei)
            ei += 1
        else:
            ns = W['abr'].shape[1] * W['abr'].shape[2]
            if s5_st is None:
                s0r = jnp.zeros((a, ns), F32)
                s0i = jnp.zeros((a, ns), F32)
            else:
                s0r = s5_st[oi][..., 0].reshape(a, ns)
                s0i = s5_st[oi][..., 1].reshape(a, ns)
            nbk, tc, lw, unroll = s5_tiles
            bb = W['bb'][oi] if nbk % 8 == 0 else W['bb2'][oi]
            x, sr, si = _s5_call(x, mod, W['gains'], layer, s0r, s0i, W['abr'][oi], W['abi'][oi],
                                 bb, W['cc'][oi], W['s5_d'][oi], W['s5_w_glu_bf'], oi,
                                 W['s5_b_glu'][oi], nbk, tc, lw, unroll)
            g, p = W['abr'].shape[1:]
            new_s5.append(jnp.stack([sr.reshape(a, g, p), si.reshape(a, g, p)], axis=-1))
            oi += 1
        x = _ffn_call(x, mod, W['gains'], layer, W['w_ff_gate_bf'], W['w_ff_up_bf'],
                      W['w_ff_down_bf'], W['final_g'], b0, b1, final=(layer == depth - 1), mix=mix)
    return x, jnp.stack(new_conv), [jnp.stack(kv) for kv in new_kv], jnp.stack(new_s5)


def kernel(x_prompt, x_sample, cache_conv, cache_kv_g0, cache_kv_g1, cache_kv_g2, state_s5, c_prompt, c_sample, norm_g, final_g, w_ada, b_ada, w_in, conv_w, conv_b, conv_ln_g, conv_ln_b, w_o, s5_lam_re, s5_lam_im, s5_log_dt, s5_b_re, s5_b_im, s5_c_re, s5_c_im, s5_d, s5_w_glu, s5_b_glu, w_ff_gate, w_ff_up, w_ff_down):
    n, l, d = x_prompt.shape
    nb, t_new, _ = x_sample.shape
    depth = w_ada.shape[0]
    conv_ch = conv_w.shape[-1]

    m_all = n + nb
    m_pad = -(-m_all // 8) * 8
    c_all = jnp.concatenate([c_sample, c_prompt, jnp.zeros((m_pad - m_all, d), F32)], axis=0)
    mod4 = _ada_call(c_all, w_ada, b_ada).reshape(depth, m_pad, 1, 6 * d)

    abr, abi, bbr, bbi, sbr, sbi = _s5_param_call(s5_lam_re, s5_lam_im, s5_log_dt, s5_b_re, s5_b_im)
    blocks = [_s5_block_weights(bbr[i], bbi[i], sbr[i], sbi[i], s5_c_re[i], s5_c_im[i])
              for i in range(abr.shape[0])]

    W = dict(gains=norm_g.reshape(depth, 2, 1, d), final_g=final_g, conv_w=conv_w, conv_b=conv_b, conv_ln_g=conv_ln_g,
             conv_ln_b=conv_ln_b, s5_d=s5_d, s5_b_glu=s5_b_glu, abr=abr, abi=abi,
             bb=[b[0] for b in blocks], bb2=[b[1] for b in blocks], cc=[b[2] for b in blocks],
             w_in_bf=w_in.astype(BF16), w_o_a_bf=w_o[:, :conv_ch].astype(BF16),
             w_o_b_bf=w_o[:, conv_ch:].astype(BF16), s5_w_glu_bf=s5_w_glu.astype(BF16),
             w_ff_gate_bf=w_ff_gate.astype(BF16), w_ff_up_bf=w_ff_up.astype(BF16),
             w_ff_down_bf=w_ff_down.astype(BF16))

    tabs_p = _rope_tables(jnp.arange(l))
    tabs_s = _rope_tables(PAST_LEN + jnp.arange(t_new))

    tm = min(512, l)
    y_p, conv_p, kvs_p, s5_p = _run_trunk(x_prompt, (mod4, nb), tabs_p, None, None, None, W,
                                          1, tm, (n, 128, 8, True))
    nbk = min(nb, tm // t_new)
    caches_t = [_cache_rows_on_lanes(c) for c in (cache_kv_g0, cache_kv_g1, cache_kv_g2)]
    y_s, conv_s, kvs_s, s5_s = _run_trunk(x_sample, (mod4, 0), tabs_s, cache_conv, caches_t, state_s5, W,
                                          nbk, t_new, (nbk, t_new, 1, True))
    return (y_p, y_s, conv_p, kvs_p[0], kvs_p[1], kvs_p[2], s5_p,
            conv_s, kvs_s[0], kvs_s[1], kvs_s[2], s5_s)
```
